```python
import jax, jax.numpy as jnp
from jax import lax
import numpy as np

D_MODEL = 1024
BATCH = 4
SEQ = 4096
DEPTH = 1

MLA_HEADS = 4
V_HEAD_DIM = D_MODEL // (2 * MLA_HEADS)
QK_NOPE_DIM = V_HEAD_DIM
QK_ROPE_DIM = V_HEAD_DIM // 2
Q_LORA_RANK = D_MODEL // 4
KV_LORA_RANK = D_MODEL // 8
MLA_WIDTH = MLA_HEADS * V_HEAD_DIM
POOL_WIDTH = D_MODEL - MLA_WIDTH
POOL_WINDOWS = (2, 4, 8, 16)
POOL_GROUPS = len(POOL_WINDOWS)
POOL_CH = POOL_WIDTH // POOL_GROUPS
IN_PROJ_DIM = Q_LORA_RANK + KV_LORA_RANK + QK_ROPE_DIM + POOL_WIDTH
D_FF = ((8 * D_MODEL // 3 + 255) // 256) * 256
N_MEM = 256
MEM_HEADS = 4
MEM_HEAD_DIM = D_MODEL // MEM_HEADS
ROPE_BASE = 10000.0
RMS_EPS = 1e-6
BLOCK_Q = 128

kernel_name = "hymba_mla_pool_macaron_memxattn"


def rmsnorm(x, g):
    xf = x.astype(jnp.float32)
    y = xf * lax.rsqrt(jnp.mean(xf * xf, axis=-1, keepdims=True) + RMS_EPS)
    return (y * g.astype(jnp.float32)).astype(x.dtype)


def swiglu(x, w_gate, w_up, w_down):
    return (jax.nn.silu(x @ w_gate) * (x @ w_up)) @ w_down


def rope(x, positions):
    d = x.shape[-1]
    freqs = 1.0 / (ROPE_BASE ** (jnp.arange(0, d, 2, dtype=jnp.float32) / d))
    ang = positions.astype(jnp.float32)[..., None] * freqs
    cos = jnp.cos(ang)[:, :, None, :]
    sin = jnp.sin(ang)[:, :, None, :]
    xf = x.astype(jnp.float32)
    x1, x2 = xf[..., : d // 2], xf[..., d // 2:]
    out = jnp.concatenate([x1 * cos - x2 * sin, x1 * sin + x2 * cos], axis=-1)
    return out.astype(x.dtype)


def causal_block_attention(q, k, v, scale):
    B, S, H, Dqk = q.shape
    nblk = S // BLOCK_Q
    qb = q.reshape(B, nblk, BLOCK_Q, H, Dqk).transpose(1, 0, 2, 3, 4)
    key_pos = jnp.arange(S)

    def one_block(args):
        qi, i = args
        s = jnp.einsum('bqhd,bkhd->bhqk', qi, k).astype(jnp.float32) * scale
        q_pos = i * BLOCK_Q + jnp.arange(BLOCK_Q)
        mask = key_pos[None, :] <= q_pos[:, None]
        s = jnp.where(mask[None, None], s, -jnp.inf)
        p = jax.nn.softmax(s, axis=-1).astype(v.dtype)
        return jnp.einsum('bhqk,bkhd->bqhd', p, v)

    out = lax.map(one_block, (qb, jnp.arange(nblk)))
    return out.transpose(1, 0, 2, 3, 4).reshape(B, S, H, v.shape[-1])


def mla_group(c_q, c_kv, k_rope, positions, q_norm, w_q_up, kv_norm, w_kv_up):
    B, S, _ = c_q.shape
    q = (rmsnorm(c_q, q_norm) @ w_q_up).reshape(B, S, MLA_HEADS, QK_NOPE_DIM + QK_ROPE_DIM)
    q_nope, q_pe = q[..., :QK_NOPE_DIM], q[..., QK_NOPE_DIM:]
    q_pe = rope(q_pe, positions)
    kv = (rmsnorm(c_kv, kv_norm) @ w_kv_up).reshape(B, S, MLA_HEADS, QK_NOPE_DIM + V_HEAD_DIM)
    k_nope, v = kv[..., :QK_NOPE_DIM], kv[..., QK_NOPE_DIM:]
    k_pe = rope(k_rope[:, :, None, :], positions)
    k_pe = jnp.broadcast_to(k_pe, (B, S, MLA_HEADS, QK_ROPE_DIM))
    q_full = jnp.concatenate([q_nope, q_pe], axis=-1)
    k_full = jnp.concatenate([k_nope, k_pe], axis=-1)
    scale = (QK_NOPE_DIM + QK_ROPE_DIM) ** -0.5
    o = causal_block_attention(q_full, k_full, v, scale)
    return o.reshape(B, S, MLA_WIDTH)


def pool_group(z, pool_w, pool_scale):
    B, S, _ = z.shape
    zg = z.reshape(B, S, POOL_GROUPS, POOL_CH)
    cs = jnp.cumsum(zg.astype(jnp.float32), axis=1)
    cs = jnp.pad(cs, ((0, 0), (1, 0), (0, 0), (0, 0)))
    t1 = jnp.arange(1, S + 1, dtype=jnp.float32)
    pooled = []
    for g, w in enumerate(POOL_WINDOWS):
        hi = cs[:, 1:, g]
        lo = jnp.pad(cs[:, : S + 1 - w, g], ((0, 0), (w - 1, 0), (0, 0)))
        count = jnp.minimum(t1, float(w))[None, :, None]
        pooled.append((hi - lo) / count)
    mean = jnp.stack(pooled, axis=2).astype(z.dtype)
    y = jnp.einsum('bsgc,gcd->bsgd', mean - zg, pool_w).reshape(B, S, POOL_WIDTH)
    return y * pool_scale


def memory_cross_attention(hn, memn, w_mq, w_mkv, w_mo):
    B, S, _ = hn.shape
    q = (hn @ w_mq).reshape(B, S, MEM_HEADS, MEM_HEAD_DIM)
    kv = (memn @ w_mkv).reshape(B, N_MEM, 2, MEM_HEADS, MEM_HEAD_DIM)
    k, v = kv[:, :, 0], kv[:, :, 1]
    s = jnp.einsum('bshd,bmhd->bhsm', q, k).astype(jnp.float32) * (MEM_HEAD_DIM ** -0.5)
    p = jax.nn.softmax(s, axis=-1).astype(v.dtype)
    o = jnp.einsum('bhsm,bmhd->bshd', p, v).reshape(B, S, D_MODEL)
    return o @ w_mo


def setup_inputs(seed: int = 0) -> dict:
    key = jax.random.key(seed)
    ks = iter(jax.random.split(key, 32))

    def dense(shape, fan_in, mult=1.0):
        return jax.random.normal(next(ks), shape, jnp.float32) * (mult * fan_in ** -0.5)

    def gain(shape):
        return 1.0 + 0.02 * jax.random.normal(next(ks), shape, jnp.float32)

    L = DEPTH
    x = jax.random.normal(next(ks), (BATCH, SEQ, D_MODEL), jnp.float32)
    mem = jax.random.normal(next(ks), (BATCH, N_MEM, D_MODEL), jnp.float32)
    offset = jax.random.randint(next(ks), (BATCH, 1), 0, 1024, dtype=jnp.int32)
    positions = (jnp.arange(SEQ, dtype=jnp.int32)[None, :] + offset).astype(jnp.int32)
    return {
        "x": x,
        "mem": mem,
        "positions": positions,
        "ffn1_norm": gain((L, D_MODEL)),
        "ffn1_w_gate": dense((L, D_MODEL, D_FF), D_MODEL),
        "ffn1_w_up": dense((L, D_MODEL, D_FF), D_MODEL),
        "ffn1_w_down": dense((L, D_FF, D_MODEL), D_FF),
        "mix_norm": gain((L, D_MODEL)),
        "w_in": dense((L, D_MODEL, IN_PROJ_DIM), D_MODEL),
        "q_norm": gain((L, Q_LORA_RANK)),
        "w_q_up": dense((L, Q_LORA_RANK, MLA_HEADS * (QK_NOPE_DIM + QK_ROPE_DIM)), Q_LORA_RANK),
        "kv_norm": gain((L, KV_LORA_RANK)),
        "w_kv_up": dense((L, KV_LORA_RANK, MLA_HEADS * (QK_NOPE_DIM + V_HEAD_DIM)), KV_LORA_RANK),
        "pool_w": dense((L, POOL_GROUPS, POOL_CH, POOL_CH), POOL_CH),
        "pool_scale": 0.5 + 0.05 * jax.random.normal(next(ks), (L, POOL_WIDTH), jnp.float32),
        "w_out": dense((L, MLA_WIDTH + POOL_WIDTH, D_MODEL), MLA_WIDTH + POOL_WIDTH),
        "xattn_norm": gain((L, D_MODEL)),
        "mem_norm": gain((L, D_MODEL)),
        "w_mq": dense((L, D_MODEL, D_MODEL), D_MODEL),
        "w_mkv": dense((L, D_MODEL, 2 * D_MODEL), D_MODEL),
        "w_mo": dense((L, D_MODEL, D_MODEL), D_MODEL),
        "ffn2_norm": gain((L, D_MODEL)),
        "ffn2_w_gate": dense((L, D_MODEL, D_FF), D_MODEL),
        "ffn2_w_up": dense((L, D_MODEL, D_FF), D_MODEL),
        "ffn2_w_down": dense((L, D_FF, D_MODEL), D_FF),
        "final_norm": gain((D_MODEL,)),
    }


def reference(x, mem, positions, ffn1_norm, ffn1_w_gate, ffn1_w_up, ffn1_w_down, mix_norm, w_in,
              q_norm, w_q_up, kv_norm, w_kv_up, pool_w, pool_scale, w_out, xattn_norm, mem_norm,
              w_mq, w_mkv, w_mo, ffn2_norm, ffn2_w_gate, ffn2_w_up, ffn2_w_down, final_norm):
    h = x
    split_pts = [Q_LORA_RANK, Q_LORA_RANK + KV_LORA_RANK, Q_LORA_RANK + KV_LORA_RANK + QK_ROPE_DIM]
    for l in range(DEPTH):
        h = h + 0.5 * swiglu(rmsnorm(h, ffn1_norm[l]), ffn1_w_gate[l], ffn1_w_up[l], ffn1_w_down[l])
        z = rmsnorm(h, mix_norm[l]) @ w_in[l]
        c_q, c_kv, k_rope, z_pool = jnp.split(z, split_pts, axis=-1)
        a = mla_group(c_q, c_kv, k_rope, positions, q_norm[l], w_q_up[l], kv_norm[l], w_kv_up[l])
        p = pool_group(z_pool, pool_w[l], pool_scale[l])
        h = h + jnp.concatenate([a, p], axis=-1) @ w_out[l]
        h = h + memory_cross_attention(rmsnorm(h, xattn_norm[l]), rmsnorm(mem, mem_norm[l]),
                                       w_mq[l], w_mkv[l], w_mo[l])
        h = h + 0.5 * swiglu(rmsnorm(h, ffn2_norm[l]), ffn2_w_gate[l], ffn2_w_up[l], ffn2_w_down[l])
    return rmsnorm(h, final_norm)
```

```python
import functools

import jax
import jax.numpy as jnp
from jax import lax
from jax.experimental import pallas as pl
from jax.experimental.pallas import tpu as pltpu

F32 = jnp.float32
BF16 = jnp.bfloat16

D_MODEL = 1024
N_HEADS = 4
HEAD_DIM = 128
ROPE_DIM = 64
ROPE_HALF = ROPE_DIM // 2
Q_RANK = 256
KV_RANK = 128
ATT_WIDTH = N_HEADS * HEAD_DIM
POOL_WIDTH = D_MODEL - ATT_WIDTH
POOL_WINDOWS = (2, 4, 8, 16)
POOL_CH = POOL_WIDTH // len(POOL_WINDOWS)
POOL_HALO = 16
D_FF = 2816
N_MEM = 256
MEM_HEADS = 4
MEM_HEAD_DIM = D_MODEL // MEM_HEADS
ROPE_BASE = 10000.0
RMS_EPS = 1e-6
ATT_SCALE = (HEAD_DIM + ROPE_DIM) ** -0.5
MEM_SCALE = MEM_HEAD_DIM ** -0.5

LANES = 128
MXU_DIM = 256
VMEM_LIMIT_BYTES = 56 * 1024 * 1024

TOKEN_TILE = 512
FF_CHUNK = MXU_DIM
Q_TILE = 512
KV_TILE = 512
QK_DIM = 2 * HEAD_DIM


def _rmsnorm(x, g):
    ms = jnp.mean(x * x, axis=-1, keepdims=True)
    return (x * lax.rsqrt(ms + RMS_EPS)) * g


def _dot(a, b):
    return jnp.dot(a, b, preferred_element_type=F32)


def _dot_nt(a, b):
    return lax.dot_general(a, b, (((1,), (1,)), ((), ())), preferred_element_type=F32)


def _swiglu(xn, wg_ref, wu_ref, wd_ref):
    acc = None
    for c in range(D_FF // FF_CHUNK):
        sl = slice(c * FF_CHUNK, (c + 1) * FF_CHUNK)
        g = _dot(xn, wg_ref[:, sl])
        u = _dot(xn, wu_ref[:, sl])
        a = (g / (1.0 + jnp.exp(-g))) * u
        d = _dot(a.astype(BF16), wd_ref[sl, :])
        acc = d if acc is None else acc + d
    return acc


def _rope(x, cos4, sin4):
    return x * cos4 + pltpu.roll(x, 2 * ROPE_HALF, 1) * sin4


def _rope_table_body(pos_ref, freq_ref, cos_ref, sin_ref):
    ang = pos_ref[...].astype(F32) * freq_ref[...]
    cos_ref[...] = jnp.cos(ang)
    sin_ref[...] = jnp.sin(ang)


def _rope_tables(positions):
    n_tok = positions.size
    per_row = LANES // ROPE_HALF
    rows = n_tok // per_row
    pos = jnp.repeat(positions.reshape(rows, per_row), ROPE_HALF, axis=1)
    freqs = 1.0 / (ROPE_BASE ** (jnp.arange(0, ROPE_DIM, 2, dtype=F32) / ROPE_DIM))
    freqs = jnp.tile(freqs, per_row).reshape(1, LANES)
    blk = 512
    cos, sin = pl.pallas_call(
        _rope_table_body,
        grid=(rows // blk,),
        in_specs=[pl.BlockSpec((blk, LANES), lambda i: (i, 0)),
                  pl.BlockSpec((1, LANES), lambda i: (0, 0))],
        out_specs=[pl.BlockSpec((blk, LANES), lambda i: (i, 0))] * 2,
        out_shape=[jax.ShapeDtypeStruct((rows, LANES), F32)] * 2,
        name="rope_tables",
    )(pos, freqs)
    return cos.reshape(n_tok, ROPE_HALF), sin.reshape(n_tok, ROPE_HALF)


def _mem_kv_body(mem_ref, g_ref, w_ref, out_ref):
    mn = _rmsnorm(mem_ref[0], g_ref[...]).astype(BF16)
    out_ref[0] = _dot(mn, w_ref[...]).astype(BF16)


def _mem_kv(mem, mem_norm, w_mkv):
    batch = mem.shape[0]
    return pl.pallas_call(
        _mem_kv_body,
        grid=(batch,),
        in_specs=[pl.BlockSpec((1, N_MEM, D_MODEL), lambda b: (b, 0, 0)),
                  pl.BlockSpec((1, D_MODEL), lambda b: (0, 0)),
                  pl.BlockSpec((D_MODEL, 2 * D_MODEL), lambda b: (0, 0))],
        out_specs=pl.BlockSpec((1, N_MEM, 2 * D_MODEL), lambda b: (b, 0, 0)),
        out_shape=jax.ShapeDtypeStruct((batch, N_MEM, 2 * D_MODEL), BF16),
        name="mem_kv",
    )(mem, mem_norm, w_mkv)


def _pre_body(tiles_per_seq,
              x_ref, cos_ref, sin_ref, n1_ref, wg_ref, wu_ref, wd_ref, mixn_ref, win_ref,
              qn_ref, wq_ref, kvn_ref, wkv_ref, poolw_ref, pscale_ref, woutp_ref,
              h_ref, q_ref, k_ref, v_ref, ext_ref):
    seq_tile = pl.program_id(0) % tiles_per_seq
    tm = x_ref.shape[0]

    x = x_ref[...]
    xn = _rmsnorm(x, n1_ref[...]).astype(BF16)
    h1 = x + 0.5 * _swiglu(xn, wg_ref, wu_ref, wd_ref)

    un = _rmsnorm(h1, mixn_ref[...]).astype(BF16)
    z = _dot(un, win_ref[...])

    cos = cos_ref[...]
    sin = sin_ref[...]
    cos4 = jnp.concatenate([cos, cos, cos, cos], axis=1)
    sin4 = jnp.concatenate([-sin, -sin, sin, sin], axis=1)

    qn = _rmsnorm(z[:, :Q_RANK], qn_ref[...]).astype(BF16)
    q = _dot(qn, wq_ref[...])
    for h in range(N_HEADS):
        base = h * QK_DIM
        q_ref[0, h, :, :HEAD_DIM] = (q[:, base:base + HEAD_DIM] * ATT_SCALE).astype(BF16)
        pe = _rope(q[:, base + HEAD_DIM:base + QK_DIM], cos4, sin4)
        q_ref[0, h, :, HEAD_DIM:] = (pe * ATT_SCALE).astype(BF16)

    kvn = _rmsnorm(z[:, Q_RANK:Q_RANK + KV_RANK], kvn_ref[...]).astype(BF16)
    kv = _dot(kvn, wkv_ref[...])
    k_pe = _rope(z[:, Q_RANK + KV_RANK:Q_RANK + KV_RANK + LANES], cos4, sin4).astype(BF16)
    for h in range(N_HEADS):
        k_ref[0, h, :, :HEAD_DIM] = kv[:, h * HEAD_DIM:(h + 1) * HEAD_DIM].astype(BF16)
        k_ref[0, h, :, HEAD_DIM:] = k_pe
        v_ref[0, h] = kv[:, ATT_WIDTH + h * HEAD_DIM:ATT_WIDTH + (h + 1) * HEAD_DIM].astype(BF16)

    zp = z[:, D_MODEL - POOL_WIDTH:]

    @pl.when(seq_tile == 0)
    def _():
        ext_ref[0:POOL_HALO, :] = jnp.zeros((POOL_HALO, POOL_WIDTH), F32)

    ext_ref[POOL_HALO:, :] = zp
    t1 = (seq_tile * tm + 1 + lax.broadcasted_iota(jnp.int32, (tm, 1), 0)).astype(F32)
    diffs = []
    for g, w in enumerate(POOL_WINDOWS):
        cols = slice(g * POOL_CH, (g + 1) * POOL_CH)
        tot = zp[:, cols]
        for back in range(1, w):
            tot = tot + ext_ref[POOL_HALO - back:POOL_HALO - back + tm, cols]
        mean = tot / jnp.minimum(t1, float(w))
        diffs.append(mean - zp[:, cols])
    ext_ref[0:POOL_HALO, :] = ext_ref[tm:tm + POOL_HALO, :]

    y01 = _dot(jnp.concatenate(diffs[:2], axis=1).astype(BF16), poolw_ref[0])
    y23 = _dot(jnp.concatenate(diffs[2:], axis=1).astype(BF16), poolw_ref[1])
    p = jnp.concatenate([y01, y23], axis=1) * pscale_ref[...]
    h_ref[...] = h1 + _dot(p.astype(BF16), woutp_ref[...])


def _const_spec(shape):
    nd = len(shape)
    return pl.BlockSpec(shape, lambda *_: (0,) * nd, pipeline_mode=pl.Buffered(1))


def _pre(x2d, cos, sin, n1, wg, wu, wd, mixn, win, qn, wq, kvn, wkv, poolw, pscale, woutp, batch, seq):
    tm = TOKEN_TILE
    n_tok = x2d.shape[0]
    tiles_per_seq = seq // tm
    tok_spec = lambda width: pl.BlockSpec((tm, width), lambda i: (i, 0))
    head_spec = lambda width: pl.BlockSpec((1, N_HEADS, tm, width),
                                           lambda i: (i // tiles_per_seq, 0, i % tiles_per_seq, 0))
    weights = (n1, wg, wu, wd, mixn, win, qn, wq, kvn, wkv, poolw, pscale, woutp)
    return pl.pallas_call(
        functools.partial(_pre_body, tiles_per_seq),
        grid=(n_tok // tm,),
        in_specs=[tok_spec(D_MODEL), tok_spec(ROPE_HALF), tok_spec(ROPE_HALF)]
                 + [_const_spec(w.shape) for w in weights],
        out_specs=[tok_spec(D_MODEL), head_spec(QK_DIM), head_spec(QK_DIM), head_spec(HEAD_DIM)],
        out_shape=[jax.ShapeDtypeStruct((n_tok, D_MODEL), F32),
                   jax.ShapeDtypeStruct((batch, N_HEADS, seq, QK_DIM), BF16),
                   jax.ShapeDtypeStruct((batch, N_HEADS, seq, QK_DIM), BF16),
                   jax.ShapeDtypeStruct((batch, N_HEADS, seq, HEAD_DIM), BF16)],
        scratch_shapes=[pltpu.VMEM((tm + POOL_HALO, POOL_WIDTH), F32)],
        compiler_params=pltpu.CompilerParams(dimension_semantics=("arbitrary",),
                                             vmem_limit_bytes=VMEM_LIMIT_BYTES),
        name="pre",
    )(x2d, cos, sin, *weights)


def _attn_body(q_ref, k_ref, v_ref, o_ref):
    qi = pl.program_id(1)
    tq = q_ref.shape[2]
    tk = KV_TILE
    row = lax.broadcasted_iota(jnp.int32, (tq, tk), 0)
    col = lax.broadcasted_iota(jnp.int32, (tq, tk), 1)
    causal = row >= col

    for h in range(N_HEADS):
        q = q_ref[0, h]

        def step(c, carry, masked):
            m, l, acc = carry
            start = pl.multiple_of(c * tk, tk)
            k = k_ref[0, h, pl.ds(start, tk), :]
            v = v_ref[0, h, pl.ds(start, tk), :]
            s = _dot_nt(q, k)
            if masked:
                s = jnp.where(causal, s, -jnp.inf)
            m_new = jnp.maximum(m, jnp.max(s, axis=-1, keepdims=True))
            alpha = jnp.exp(m - m_new)
            p = jnp.exp(s - m_new)
            l_new = alpha * l + jnp.sum(p, axis=-1, keepdims=True)
            acc_new = alpha * acc + _dot(p.astype(BF16), v)
            return m_new, l_new, acc_new

        init = (jnp.full((tq, 1), -jnp.inf, F32), jnp.zeros((tq, 1), F32), jnp.zeros((tq, HEAD_DIM), F32))
        carry = lax.fori_loop(0, qi, lambda c, cr: step(c, cr, False), init)
        _, l, acc = step(qi, carry, True)
        o_ref[0, :, h * HEAD_DIM:(h + 1) * HEAD_DIM] = (acc / l).astype(BF16)


def _mla_attention(q, k, v):
    batch, _, seq, _ = q.shape
    assert Q_TILE == KV_TILE
    return pl.pallas_call(
        _attn_body,
        grid=(batch, seq // Q_TILE),
        in_specs=[pl.BlockSpec((1, N_HEADS, Q_TILE, QK_DIM), lambda b, i: (b, 0, i, 0)),
                  pl.BlockSpec((1, N_HEADS, seq, QK_DIM), lambda b, i: (b, 0, 0, 0)),
                  pl.BlockSpec((1, N_HEADS, seq, HEAD_DIM), lambda b, i: (b, 0, 0, 0))],
        out_specs=pl.BlockSpec((1, Q_TILE, ATT_WIDTH), lambda b, i: (b, i, 0)),
        out_shape=jax.ShapeDtypeStruct((batch, seq, ATT_WIDTH), BF16),
        compiler_params=pltpu.CompilerParams(dimension_semantics=("arbitrary", "arbitrary"),
                                             vmem_limit_bytes=VMEM_LIMIT_BYTES),
        name="mla_attn",
    )(q, k, v)


def _post_body(h_ref, a_ref, mkv_ref, wouta_ref, xn_ref, wmq_ref, wmo_ref,
               n2_ref, wg_ref, wu_ref, wd_ref, fn_ref, out_ref):
    h2 = h_ref[...] + _dot(a_ref[...], wouta_ref[...])

    hn = _rmsnorm(h2, xn_ref[...]).astype(BF16)
    q = (_dot(hn, wmq_ref[...]) * MEM_SCALE).astype(BF16)
    heads = []
    for h in range(MEM_HEADS):
        cols = slice(h * MEM_HEAD_DIM, (h + 1) * MEM_HEAD_DIM)
        k = mkv_ref[0, :, cols]
        v = mkv_ref[0, :, D_MODEL + h * MEM_HEAD_DIM:D_MODEL + (h + 1) * MEM_HEAD_DIM]
        s = _dot_nt(q[:, cols], k)
        e = jnp.exp(s - jnp.max(s, axis=-1, keepdims=True))
        p = e / jnp.sum(e, axis=-1, keepdims=True)
        heads.append(_dot(p.astype(BF16), v).astype(BF16))
    h3 = h2 + _dot(jnp.concatenate(heads, axis=1), wmo_ref[...])

    xn = _rmsnorm(h3, n2_ref[...]).astype(BF16)
    h4 = h3 + 0.5 * _swiglu(xn, wg_ref, wu_ref, wd_ref)
    out_ref[...] = _rmsnorm(h4, fn_ref[...])


def _post(h2d, a2d, mkv, wouta, xn, wmq, wmo, n2, wg, wu, wd, fn, seq):
    tm = TOKEN_TILE
    n_tok = h2d.shape[0]
    tiles_per_seq = seq // tm
    weights = (wouta, xn, wmq, wmo, n2, wg, wu, wd, fn)
    return pl.pallas_call(
        _post_body,
        grid=(n_tok // tm,),
        in_specs=[pl.BlockSpec((tm, D_MODEL), lambda i: (i, 0)),
                  pl.BlockSpec((tm, ATT_WIDTH), lambda i: (i, 0)),
                  pl.BlockSpec((1, N_MEM, 2 * D_MODEL), lambda i: (i // tiles_per_seq, 0, 0))]
                 + [_const_spec(w.shape) for w in weights],
        out_specs=pl.BlockSpec((tm, D_MODEL), lambda i: (i, 0)),
        out_shape=jax.ShapeDtypeStruct((n_tok, D_MODEL), F32),
        compiler_params=pltpu.CompilerParams(dimension_semantics=("arbitrary",),
                                             vmem_limit_bytes=VMEM_LIMIT_BYTES),
        name="post",
    )(h2d, a2d, mkv, *weights)


def _pad_rope_cols(w):
    zeros = jnp.zeros(w.shape[:-1] + (ROPE_HALF,), w.dtype)
    return jnp.concatenate([w[..., :ROPE_HALF], zeros, w[..., ROPE_HALF:], zeros], axis=-1)


def _block_diag2(a, b):
    za = jnp.zeros((a.shape[0], b.shape[1]), a.dtype)
    zb = jnp.zeros((b.shape[0], a.shape[1]), a.dtype)
    return jnp.concatenate([jnp.concatenate([a, za], axis=1), jnp.concatenate([zb, b], axis=1)], axis=0)


def _layer(h2d, mem, cos, sin, batch, seq, ffn1_norm, ffn1_w_gate, ffn1_w_up, ffn1_w_down, mix_norm, w_in,
           q_norm, w_q_up, kv_norm, w_kv_up, pool_w, pool_scale, w_out, xattn_norm, mem_norm,
           w_mq, w_mkv, w_mo, ffn2_norm, ffn2_w_gate, ffn2_w_up, ffn2_w_down, out_norm):
    row = lambda g: g.reshape(1, -1).astype(F32)
    rope_lo = Q_RANK + KV_RANK
    w_in_p = jnp.concatenate([w_in[:, :rope_lo], _pad_rope_cols(w_in[:, rope_lo:rope_lo + ROPE_DIM]),
                              w_in[:, rope_lo + ROPE_DIM:]], axis=1).astype(BF16)
    wq = w_q_up.reshape(Q_RANK, N_HEADS, HEAD_DIM + ROPE_DIM)
    wq_p = jnp.concatenate([wq[..., :HEAD_DIM], _pad_rope_cols(wq[..., HEAD_DIM:])], axis=-1)
    wq_p = wq_p.reshape(Q_RANK, N_HEADS * QK_DIM).astype(BF16)
    wkv = w_kv_up.reshape(KV_RANK, N_HEADS, 2 * HEAD_DIM)
    wkv_p = jnp.concatenate([wkv[..., :HEAD_DIM].reshape(KV_RANK, ATT_WIDTH),
                             wkv[..., HEAD_DIM:].reshape(KV_RANK, ATT_WIDTH)], axis=1).astype(BF16)
    poolw = jnp.stack([_block_diag2(pool_w[0], pool_w[1]), _block_diag2(pool_w[2], pool_w[3])]).astype(BF16)

    mkv = _mem_kv(mem, row(mem_norm), w_mkv.astype(BF16))
    h1p, q, k, v = _pre(h2d, cos, sin, row(ffn1_norm), ffn1_w_gate.astype(BF16), ffn1_w_up.astype(BF16),
                        ffn1_w_down.astype(BF16), row(mix_norm), w_in_p, row(q_norm), wq_p, row(kv_norm),
                        wkv_p, poolw, row(pool_scale), w_out[ATT_WIDTH:].astype(BF16), batch, seq)
    a = _mla_attention(q, k, v)
    return _post(h1p, a.reshape(batch * seq, ATT_WIDTH), mkv, w_out[:ATT_WIDTH].astype(BF16),
                 row(xattn_norm), w_mq.astype(BF16), w_mo.astype(BF16), row(ffn2_norm),
                 ffn2_w_gate.astype(BF16), ffn2_w_up.astype(BF16), ffn2_w_down.astype(BF16), row(out_norm), seq)


def kernel(x, mem, positions, ffn1_norm, ffn1_w_gate, ffn1_w_up, ffn1_w_down, mix_norm, w_in, q_norm, w_q_up,
           kv_norm, w_kv_up, pool_w, pool_scale, w_out, xattn_norm, mem_norm, w_mq, w_mkv, w_mo, ffn2_norm,
           ffn2_w_gate, ffn2_w_up, ffn2_w_down, final_norm):
    batch, seq, d_model = x.shape
    depth = ffn1_norm.shape[0]
    assert d_model == D_MODEL and depth == 1 and seq % TOKEN_TILE == 0 and seq % Q_TILE == 0
    cos, sin = _rope_tables(positions)
    layer = (ffn1_norm, ffn1_w_gate, ffn1_w_up, ffn1_w_down, mix_norm, w_in, q_norm, w_q_up, kv_norm, w_kv_up,
             pool_w, pool_scale, w_out, xattn_norm, mem_norm, w_mq, w_mkv, w_mo, ffn2_norm, ffn2_w_gate,
             ffn2_w_up, ffn2_w_down)
    out = _layer(x.reshape(batch * seq, d_model), mem, cos, sin, batch, seq, *[w[0] for w in layer], final_norm)
    return out.reshape(batch, seq, d_model)
```

```python
import functools

import jax
import jax.numpy as jnp
from jax import lax
from jax.experimental import pallas as pl
from jax.experimental.pallas import tpu as pltpu

F32 = jnp.float32
BF16 = jnp.bfloat16

D_MODEL = 1024
N_HEADS = 4
HEAD_DIM = 128
ROPE_DIM = 64
ROPE_HALF = ROPE_DIM // 2
Q_RANK = 256
KV_RANK = 128
ATT_WIDTH = N_HEADS * HEAD_DIM
POOL_WIDTH = D_MODEL - ATT_WIDTH
POOL_WINDOWS = (2, 4, 8, 16)
POOL_CH = POOL_WIDTH // len(POOL_WINDOWS)
POOL_HALO = 16
D_FF = 2816
N_MEM = 256
MEM_HEADS = 4
MEM_HEAD_DIM = D_MODEL // MEM_HEADS
ROPE_BASE = 10000.0
RMS_EPS = 1e-6
ATT_SCALE = (HEAD_DIM + ROPE_DIM) ** -0.5
LOG2_E = 1.4426950408889634
Q_PRESCALE = ATT_SCALE * LOG2_E
MEM_SCALE = MEM_HEAD_DIM ** -0.5

LANES = 128
MXU_DIM = 256
VMEM_LIMIT_BYTES = 56 * 1024 * 1024

TOKEN_TILE = 512
FF_CHUNK = MXU_DIM
Q_TILE = 512
KV_TILE = 512
QK_DIM = 2 * HEAD_DIM


def _rmsnorm(x, g):
    ms = jnp.mean(x * x, axis=-1, keepdims=True)
    return (x * lax.rsqrt(ms + RMS_EPS)) * g


def _dot(a, b):
    return jnp.dot(a, b, preferred_element_type=F32)


def _dot_nt(a, b):
    return lax.dot_general(a, b, (((1,), (1,)), ((), ())), preferred_element_type=F32)


def _swiglu(xn, wg_ref, wu_ref, wd_ref):
    acc = None
    for c in range(D_FF // FF_CHUNK):
        sl = slice(c * FF_CHUNK, (c + 1) * FF_CHUNK)
        g = _dot(xn, wg_ref[:, sl])
        u = _dot(xn, wu_ref[:, sl])
        a = (g / (1.0 + jnp.exp(-g))) * u
        d = _dot(a.astype(BF16), wd_ref[sl, :])
        acc = d if acc is None else acc + d
    return acc


def _rope(x, cos4, sin4):
    return x * cos4 + pltpu.roll(x, 2 * ROPE_HALF, 1) * sin4


def _rope_table_body(pos_ref, freq_ref, cos_ref, sin_ref):
    ang = freq_ref[...] * pos_ref[...].astype(F32)
    cos_ref[...] = jnp.cos(ang)
    sin_ref[...] = jnp.sin(ang)


def _rope_tables(positions):
    n_tok = positions.size
    freqs = 1.0 / (ROPE_BASE ** (jnp.arange(0, ROPE_DIM, 2, dtype=F32) / ROPE_DIM))
    blk = 2048
    return pl.pallas_call(
        _rope_table_body,
        grid=(n_tok // blk,),
        in_specs=[pl.BlockSpec((1, blk), lambda i: (0, i)),
                  pl.BlockSpec((ROPE_HALF, 1), lambda i: (0, 0))],
        out_specs=[pl.BlockSpec((ROPE_HALF, blk), lambda i: (0, i))] * 2,
        out_shape=[jax.ShapeDtypeStruct((ROPE_HALF, n_tok), F32)] * 2,
        name="rope_tables",
    )(positions.reshape(1, n_tok), freqs.reshape(ROPE_HALF, 1))


def _mem_kv_body(mem_ref, g_ref, w_ref, out_ref):
    mn = _rmsnorm(mem_ref[0], g_ref[...]).astype(BF16)
    out_ref[0] = _dot(mn, w_ref[...]).astype(BF16)


def _mem_kv(mem, mem_norm, w_mkv):
    batch = mem.shape[0]
    return pl.pallas_call(
        _mem_kv_body,
        grid=(batch,),
        in_specs=[pl.BlockSpec((1, N_MEM, D_MODEL), lambda b: (b, 0, 0)),
                  pl.BlockSpec((1, D_MODEL), lambda b: (0, 0)),
                  pl.BlockSpec((D_MODEL, 2 * D_MODEL), lambda b: (0, 0))],
        out_specs=pl.BlockSpec((1, N_MEM, 2 * D_MODEL), lambda b: (b, 0, 0)),
        out_shape=jax.ShapeDtypeStruct((batch, N_MEM, 2 * D_MODEL), BF16),
        name="mem_kv",
    )(mem, mem_norm, w_mkv)


def _pre_body(tiles_per_seq,
              x_ref, cos_ref, sin_ref, n1_ref, wg_ref, wu_ref, wd_ref, mixn_ref, win_ref,
              qn_ref, wq_ref, kvn_ref, wk_ref, wv_ref, poolw_ref, pscale_ref, woutp_ref,
              h_ref, q_ref, k_ref, v_ref, ext_ref):
    seq_tile = pl.program_id(0) % tiles_per_seq
    tm = x_ref.shape[0]

    x = x_ref[...]
    xn = _rmsnorm(x, n1_ref[...]).astype(BF16)
    h1 = x + 0.5 * _swiglu(xn, wg_ref, wu_ref, wd_ref)

    un = _rmsnorm(h1, mixn_ref[...]).astype(BF16)
    z = _dot(un, win_ref[...])

    cos = cos_ref[...]
    sin = sin_ref[...]
    cos4_t = jnp.concatenate([cos, cos, cos, cos], axis=0)
    sin4_t = jnp.concatenate([-sin, -sin, sin, sin], axis=0)

    qn = _rmsnorm(z[:, :Q_RANK], qn_ref[...]).astype(BF16)
    q_t = _dot_nt(wq_ref[...], qn)
    for h in range(N_HEADS):
        base = h * QK_DIM
        q_ref[0, h, :HEAD_DIM, :] = (q_t[base:base + HEAD_DIM] * Q_PRESCALE).astype(BF16)
        pe = q_t[base + HEAD_DIM:base + QK_DIM]
        pe = pe * cos4_t + pltpu.roll(pe, 2 * ROPE_HALF, 0) * sin4_t
        q_ref[0, h, HEAD_DIM:, :] = (pe * Q_PRESCALE).astype(BF16)

    kvn = _rmsnorm(z[:, Q_RANK:Q_RANK + KV_RANK], kvn_ref[...]).astype(BF16)
    k_nope = _dot(kvn, wk_ref[...])
    v_t = _dot_nt(wv_ref[...], kvn)
    k_pe = _rope(z[:, Q_RANK + KV_RANK:Q_RANK + KV_RANK + LANES], cos4_t.T, sin4_t.T).astype(BF16)
    for h in range(N_HEADS):
        k_ref[0, h, :, :HEAD_DIM] = k_nope[:, h * HEAD_DIM:(h + 1) * HEAD_DIM].astype(BF16)
        k_ref[0, h, :, HEAD_DIM:] = k_pe
        v_ref[0, h, 0] = v_t[h * HEAD_DIM:(h + 1) * HEAD_DIM].astype(BF16)

    zp = z[:, D_MODEL - POOL_WIDTH:]

    @pl.when(seq_tile == 0)
    def _():
        ext_ref[0:POOL_HALO, :] = jnp.zeros((POOL_HALO, POOL_WIDTH), F32)

    ext_ref[POOL_HALO:, :] = zp
    t1 = (seq_tile * tm + 1 + lax.broadcasted_iota(jnp.int32, (tm, 1), 0)).astype(F32)
    diffs = []
    for g, w in enumerate(POOL_WINDOWS):
        cols = slice(g * POOL_CH, (g + 1) * POOL_CH)
        tot = zp[:, cols]
        for back in range(1, w):
            tot = tot + ext_ref[POOL_HALO - back:POOL_HALO - back + tm, cols]
        mean = tot / jnp.minimum(t1, float(w))
        diffs.append(mean - zp[:, cols])
    ext_ref[0:POOL_HALO, :] = ext_ref[tm:tm + POOL_HALO, :]

    y01 = _dot(jnp.concatenate(diffs[:2], axis=1).astype(BF16), poolw_ref[0])
    y23 = _dot(jnp.concatenate(diffs[2:], axis=1).astype(BF16), poolw_ref[1])
    p = jnp.concatenate([y01, y23], axis=1) * pscale_ref[...]
    h_ref[...] = h1 + _dot(p.astype(BF16), woutp_ref[...])


def _const_spec(shape):
    nd = len(shape)
    return pl.BlockSpec(shape, lambda *_: (0,) * nd, pipeline_mode=pl.Buffered(1))


def _pre(x2d, cos, sin, n1, wg, wu, wd, mixn, win, qn, wq, kvn, wk, wv, poolw, pscale, woutp, batch, seq):
    tm = TOKEN_TILE
    assert tm == KV_TILE
    n_tok = x2d.shape[0]
    tiles_per_seq = seq // tm
    tok_spec = lambda width: pl.BlockSpec((tm, width), lambda i: (i, 0))
    rope_spec = pl.BlockSpec((ROPE_HALF, tm), lambda i: (0, i))
    bt = lambda i: (i // tiles_per_seq, i % tiles_per_seq)
    q_spec = pl.BlockSpec((1, N_HEADS, QK_DIM, tm), lambda i: (bt(i)[0], 0, 0, bt(i)[1]))
    k_spec = pl.BlockSpec((1, N_HEADS, tm, QK_DIM), lambda i: (bt(i)[0], 0, bt(i)[1], 0))
    v_spec = pl.BlockSpec((1, N_HEADS, 1, HEAD_DIM, tm), lambda i: (bt(i)[0], 0, bt(i)[1], 0, 0))
    weights = (n1, wg, wu, wd, mixn, win, qn, wq, kvn, wk, wv, poolw, pscale, woutp)
    return pl.pallas_call(
        functools.partial(_pre_body, tiles_per_seq),
        grid=(n_tok // tm,),
        in_specs=[tok_spec(D_MODEL), rope_spec, rope_spec] + [_const_spec(w.shape) for w in weights],
        out_specs=[tok_spec(D_MODEL), q_spec, k_spec, v_spec],
        out_shape=[jax.ShapeDtypeStruct((n_tok, D_MODEL), F32),
                   jax.ShapeDtypeStruct((batch, N_HEADS, QK_DIM, seq), BF16),
                   jax.ShapeDtypeStruct((batch, N_HEADS, seq, QK_DIM), BF16),
                   jax.ShapeDtypeStruct((batch, N_HEADS, tiles_per_seq, HEAD_DIM, tm), BF16)],
        scratch_shapes=[pltpu.VMEM((tm + POOL_HALO, POOL_WIDTH), F32)],
        compiler_params=pltpu.CompilerParams(dimension_semantics=("arbitrary",),
                                             vmem_limit_bytes=VMEM_LIMIT_BYTES),
        name="pre",
    )(x2d, cos, sin, *weights)


def _attn_body(q_ref, k_ref, v_ref, o_ref):
    qi = pl.program_id(1)
    tq = q_ref.shape[3]
    tk = KV_TILE
    key_idx = lax.broadcasted_iota(jnp.int32, (tk, tq), 0)
    qry_idx = lax.broadcasted_iota(jnp.int32, (tk, tq), 1)
    causal = key_idx <= qry_idx

    def step(c, carry, masked):
        kstart = pl.multiple_of(c * tk, tk)
        scores = [_dot(k_ref[0, h, pl.ds(kstart, tk), :], q_ref[0, h]) for h in range(N_HEADS)]
        probs = []
        for h in range(N_HEADS):
            m, l, _ = carry[h]
            s = jnp.where(causal, scores[h], -jnp.inf) if masked else scores[h]
            m_new = jnp.maximum(m, jnp.max(s, axis=0, keepdims=True))
            alpha = jnp.exp2(m - m_new)
            p = jnp.exp2(s - m_new)
            l_new = alpha * l + jnp.sum(p, axis=0, keepdims=True)
            probs.append((m_new, l_new, alpha, p.astype(BF16)))
        out = []
        for h in range(N_HEADS):
            m_new, l_new, alpha, p = probs[h]
            acc_new = alpha * carry[h][2] + _dot(v_ref[0, h, c], p)
            out.append((m_new, l_new, acc_new))
        return tuple(out)

    init = tuple((jnp.full((1, tq), -jnp.inf, F32), jnp.zeros((1, tq), F32), jnp.zeros((HEAD_DIM, tq), F32))
                 for _ in range(N_HEADS))
    carry = lax.fori_loop(0, qi, lambda c, cr: step(c, cr, False), init)
    carry = step(qi, carry, True)
    for h in range(N_HEADS):
        _, l, acc = carry[h]
        o_ref[0, :, h * HEAD_DIM:(h + 1) * HEAD_DIM] = (acc / l).T.astype(BF16)


def _mla_attention(q, k, v):
    batch, _, seq, _ = k.shape
    assert Q_TILE == KV_TILE
    return pl.pallas_call(
        _attn_body,
        grid=(batch, seq // Q_TILE),
        in_specs=[pl.BlockSpec((1, N_HEADS, QK_DIM, Q_TILE), lambda b, i: (b, 0, 0, i)),
                  pl.BlockSpec((1, N_HEADS, seq, QK_DIM), lambda b, i: (b, 0, 0, 0)),
                  pl.BlockSpec((1, N_HEADS, seq // KV_TILE, HEAD_DIM, KV_TILE), lambda b, i: (b, 0, 0, 0, 0))],
        out_specs=pl.BlockSpec((1, Q_TILE, ATT_WIDTH), lambda b, i: (b, i, 0)),
        out_shape=jax.ShapeDtypeStruct((batch, seq, ATT_WIDTH), BF16),
        compiler_params=pltpu.CompilerParams(dimension_semantics=("arbitrary", "arbitrary"),
                                             vmem_limit_bytes=VMEM_LIMIT_BYTES),
        name="mla_attn",
    )(q, k, v)


def _post_body(h_ref, a_ref, mkv_ref, wouta_ref, xn_ref, wmq_ref, wmo_ref,
               n2_ref, wg_ref, wu_ref, wd_ref, fn_ref, out_ref):
    h2 = h_ref[...] + _dot(a_ref[...], wouta_ref[...])

    hn = _rmsnorm(h2, xn_ref[...]).astype(BF16)
    q = (_dot(hn, wmq_ref[...]) * MEM_SCALE).astype(BF16)
    heads = []
    for h in range(MEM_HEADS):
        cols = slice(h * MEM_HEAD_DIM, (h + 1) * MEM_HEAD_DIM)
        k = mkv_ref[0, :, cols]
        v = mkv_ref[0, :, D_MODEL + h * MEM_HEAD_DIM:D_MODEL + (h + 1) * MEM_HEAD_DIM]
        s = _dot_nt(q[:, cols], k)
        e = jnp.exp(s - jnp.max(s, axis=-1, keepdims=True))
        p = e / jnp.sum(e, axis=-1, keepdims=True)
        heads.append(_dot(p.astype(BF16), v).astype(BF16))
    h3 = h2 + _dot(jnp.concatenate(heads, axis=1), wmo_ref[...])

    xn = _rmsnorm(h3, n2_ref[...]).astype(BF16)
    h4 = h3 + 0.5 * _swiglu(xn, wg_ref, wu_ref, wd_ref)
    out_ref[...] = _rmsnorm(h4, fn_ref[...])


def _post(h2d, a2d, mkv, wouta, xn, wmq, wmo, n2, wg, wu, wd, fn, seq):
    tm = TOKEN_TILE
    n_tok = h2d.shape[0]
    tiles_per_seq = seq // tm
    weights = (wouta, xn, wmq, wmo, n2, wg, wu, wd, fn)
    return pl.pallas_call(
        _post_body,
        grid=(n_tok // tm,),
        in_specs=[pl.BlockSpec((tm, D_MODEL), lambda i: (i, 0)),
                  pl.BlockSpec((tm, ATT_WIDTH), lambda i: (i, 0)),
                  pl.BlockSpec((1, N_MEM, 2 * D_MODEL), lambda i: (i // tiles_per_seq, 0, 0))]
                 + [_const_spec(w.shape) for w in weights],
        out_specs=pl.BlockSpec((tm, D_MODEL), lambda i: (i, 0)),
        out_shape=jax.ShapeDtypeStruct((n_tok, D_MODEL), F32),
        compiler_params=pltpu.CompilerParams(dimension_semantics=("arbitrary",),
                                             vmem_limit_bytes=VMEM_LIMIT_BYTES),
        name="post",
    )(h2d, a2d, mkv, *weights)


def _pad_rope_cols(w):
    zeros = jnp.zeros(w.shape[:-1] + (ROPE_HALF,), w.dtype)
    return jnp.concatenate([w[..., :ROPE_HALF], zeros, w[..., ROPE_HALF:], zeros], axis=-1)


def _block_diag2(a, b):
    za = jnp.zeros((a.shape[0], b.shape[1]), a.dtype)
    zb = jnp.zeros((b.shape[0], a.shape[1]), a.dtype)
    return jnp.concatenate([jnp.concatenate([a, za], axis=1), jnp.concatenate([zb, b], axis=1)], axis=0)


def _layer(h2d, mem, cos, sin, batch, seq, ffn1_norm, ffn1_w_gate, ffn1_w_up, ffn1_w_down, mix_norm, w_in,
           q_norm, w_q_up, kv_norm, w_kv_up, pool_w, pool_scale, w_out, xattn_norm, mem_norm,
           w_mq, w_mkv, w_mo, ffn2_norm, ffn2_w_gate, ffn2_w_up, ffn2_w_down, out_norm):
    row = lambda g: g.reshape(1, -1).astype(F32)
    rope_lo = Q_RANK + KV_RANK
    w_in_p = jnp.concatenate([w_in[:, :rope_lo], _pad_rope_cols(w_in[:, rope_lo:rope_lo + ROPE_DIM]),
                              w_in[:, rope_lo + ROPE_DIM:]], axis=1).astype(BF16)
    wq = w_q_up.reshape(Q_RANK, N_HEADS, HEAD_DIM + ROPE_DIM)
    wq_p = jnp.concatenate([wq[..., :HEAD_DIM], _pad_rope_cols(wq[..., HEAD_DIM:])], axis=-1)
    wq_t = wq_p.reshape(Q_RANK, N_HEADS * QK_DIM).T.astype(BF16)
    wkv = w_kv_up.reshape(KV_RANK, N_HEADS, 2 * HEAD_DIM)
    wk = wkv[..., :HEAD_DIM].reshape(KV_RANK, ATT_WIDTH).astype(BF16)
    wv_t = wkv[..., HEAD_DIM:].reshape(KV_RANK, ATT_WIDTH).T.astype(BF16)
    poolw = jnp.stack([_block_diag2(pool_w[0], pool_w[1]), _block_diag2(pool_w[2], pool_w[3])]).astype(BF16)

    mkv = _mem_kv(mem, row(mem_norm), w_mkv.astype(BF16))
    h1p, q, k, v = _pre(h2d, cos, sin, row(ffn1_norm), ffn1_w_gate.astype(BF16), ffn1_w_up.astype(BF16),
                        ffn1_w_down.astype(BF16), row(mix_norm), w_in_p, row(q_norm), wq_t, row(kv_norm),
                        wk, wv_t, poolw, row(pool_scale), w_out[ATT_WIDTH:].astype(BF16), batch, seq)
    a = _mla_attention(q, k, v)
    return _post(h1p, a.reshape(batch * seq, ATT_WIDTH), mkv, w_out[:ATT_WIDTH].astype(BF16),
                 row(xattn_norm), w_mq.astype(BF16), w_mo.astype(BF16), row(ffn2_norm),
                 ffn2_w_gate.astype(BF16), ffn2_w_up.astype(BF16), ffn2_w_down.astype(BF16), row(out_norm), seq)


def kernel(x, mem, positions, ffn1_norm, ffn1_w_gate, ffn1_w_up, ffn1_w_down, mix_norm, w_in, q_norm, w_q_up,
           kv_norm, w_kv_up, pool_w, pool_scale, w_out, xattn_norm, mem_norm, w_mq, w_mkv, w_mo, ffn2_norm,
           ffn2_w_gate, ffn2_w_up, ffn2_w_down, final_norm):
    batch, seq, d_model = x.shape
    depth = ffn1_norm.shape[0]
    assert d_model == D_MODEL and depth == 1 and seq % TOKEN_TILE == 0 and seq % Q_TILE == 0
    cos, sin = _rope_tables(positions)
    layer = (ffn1_norm, ffn1_w_gate, ffn1_w_up, ffn1_w_down, mix_norm, w_in, q_norm, w_q_up, kv_norm, w_kv_up,
             pool_w, pool_scale, w_out, xattn_norm, mem_norm, w_mq, w_mkv, w_mo, ffn2_norm, ffn2_w_gate,
             ffn2_w_up, ffn2_w_down)
    out = _layer(x.reshape(batch * seq, d_model), mem, cos, sin, batch, seq, *[w[0] for w in layer], final_norm)
    return out.reshape(batch, seq, d_model)
```

```python
import functools

import jax
import jax.numpy as jnp
from jax import lax
from jax.experimental import pallas as pl
from jax.experimental.pallas import tpu as pltpu

F32 = jnp.float32
BF16 = jnp.bfloat16

D_MODEL = 1024
N_HEADS = 4
HEAD_DIM = 128
ROPE_DIM = 64
ROPE_HALF = ROPE_DIM // 2
Q_RANK = 256
KV_RANK = 128
ATT_WIDTH = N_HEADS * HEAD_DIM
POOL_WIDTH = D_MODEL - ATT_WIDTH
POOL_WINDOWS = (2, 4, 8, 16)
POOL_CH = POOL_WIDTH // len(POOL_WINDOWS)
POOL_HALO = 16
D_FF = 2816
N_MEM = 256
MEM_HEADS = 4
MEM_HEAD_DIM = D_MODEL // MEM_HEADS
ROPE_BASE = 10000.0
RMS_EPS = 1e-6
ATT_SCALE = (HEAD_DIM + ROPE_DIM) ** -0.5
LOG2_E = 1.4426950408889634
Q_PRESCALE = ATT_SCALE * LOG2_E
MEM_SCALE = MEM_HEAD_DIM ** -0.5

LANES = 128
MXU_DIM = 256
VMEM_LIMIT_BYTES = 56 * 1024 * 1024

TOKEN_TILE = 512
FF_CHUNK = MXU_DIM
Q_TILE = 512
KV_TILE = 512
QK_DIM = 2 * HEAD_DIM


def _rmsnorm(x, g):
    ms = jnp.mean(x * x, axis=-1, keepdims=True)
    return (x * lax.rsqrt(ms + RMS_EPS)) * g


def _dot(a, b):
    return jnp.dot(a, b, preferred_element_type=F32)


def _dot_nt(a, b):
    return lax.dot_general(a, b, (((1,), (1,)), ((), ())), preferred_element_type=F32)


def _swiglu(xn, wg_ref, wu_ref, wd_ref):
    acc = None
    for c in range(D_FF // FF_CHUNK):
        sl = slice(c * FF_CHUNK, (c + 1) * FF_CHUNK)
        g = _dot(xn, wg_ref[:, sl])
        u = _dot(xn, wu_ref[:, sl])
        a = (g / (1.0 + jnp.exp(-g))) * u
        d = _dot(a.astype(BF16), wd_ref[sl, :])
        acc = d if acc is None else acc + d
    return acc


def _rope(x, cos4, sin4):
    return x * cos4 + pltpu.roll(x, 2 * ROPE_HALF, 1) * sin4


def _rope_table_body(pos_ref, freq_ref, cos_ref, sin_ref):
    ang = freq_ref[...] * pos_ref[...].astype(F32)
    cos_ref[...] = jnp.cos(ang)
    sin_ref[...] = jnp.sin(ang)


def _rope_tables(positions):
    n_tok = positions.size
    freqs = 1.0 / (ROPE_BASE ** (jnp.arange(0, ROPE_DIM, 2, dtype=F32) / ROPE_DIM))
    blk = 2048
    return pl.pallas_call(
        _rope_table_body,
        grid=(n_tok // blk,),
        in_specs=[pl.BlockSpec((1, blk), lambda i: (0, i)),
                  pl.BlockSpec((ROPE_HALF, 1), lambda i: (0, 0))],
        out_specs=[pl.BlockSpec((ROPE_HALF, blk), lambda i: (0, i))] * 2,
        out_shape=[jax.ShapeDtypeStruct((ROPE_HALF, n_tok), F32)] * 2,
        name="rope_tables",
    )(positions.reshape(1, n_tok), freqs.reshape(ROPE_HALF, 1))


def _mem_kv_body(mem_ref, g_ref, w_ref, out_ref):
    mn = _rmsnorm(mem_ref[0], g_ref[...]).astype(BF16)
    out_ref[0] = _dot(mn, w_ref[...]).astype(BF16)


def _mem_kv(mem, mem_norm, w_mkv):
    batch = mem.shape[0]
    return pl.pallas_call(
        _mem_kv_body,
        grid=(batch,),
        in_specs=[pl.BlockSpec((1, N_MEM, D_MODEL), lambda b: (b, 0, 0)),
                  pl.BlockSpec((1, D_MODEL), lambda b: (0, 0)),
                  pl.BlockSpec((D_MODEL, 2 * D_MODEL), lambda b: (0, 0))],
        out_specs=pl.BlockSpec((1, N_MEM, 2 * D_MODEL), lambda b: (b, 0, 0)),
        out_shape=jax.ShapeDtypeStruct((batch, N_MEM, 2 * D_MODEL), BF16),
        name="mem_kv",
    )(mem, mem_norm, w_mkv)


def _pre_body(tiles_per_seq,
              x_ref, cos_ref, sin_ref, n1_ref, wg_ref, wu_ref, wd_ref, mixn_ref, win_ref,
              qn_ref, wq_ref, kvn_ref, wk_ref, wv_ref, poolw_ref, pscale_ref, woutp_ref,
              h_ref, q_ref, k_ref, v_ref, ext_ref):
    seq_tile = pl.program_id(0) % tiles_per_seq
    tm = x_ref.shape[0]

    x = x_ref[...]
    xn = _rmsnorm(x, n1_ref[...]).astype(BF16)
    h1 = x + 0.5 * _swiglu(xn, wg_ref, wu_ref, wd_ref)

    un = _rmsnorm(h1, mixn_ref[...]).astype(BF16)
    z = _dot(un, win_ref[...])

    cos = cos_ref[...]
    sin = sin_ref[...]
    cos4_t = jnp.concatenate([cos, cos, cos, cos], axis=0)
    sin4_t = jnp.concatenate([-sin, -sin, sin, sin], axis=0)

    qn = _rmsnorm(z[:, :Q_RANK], qn_ref[...]).astype(BF16)
    q_t = _dot_nt(wq_ref[...], qn)
    for h in range(N_HEADS):
        base = h * QK_DIM
        q_ref[0, h, :HEAD_DIM, :] = (q_t[base:base + HEAD_DIM] * Q_PRESCALE).astype(BF16)
        pe = q_t[base + HEAD_DIM:base + QK_DIM]
        pe = pe * cos4_t + pltpu.roll(pe, 2 * ROPE_HALF, 0) * sin4_t
        q_ref[0, h, HEAD_DIM:, :] = (pe * Q_PRESCALE).astype(BF16)

    kvn = _rmsnorm(z[:, Q_RANK:Q_RANK + KV_RANK], kvn_ref[...]).astype(BF16)
    k_nope = _dot(kvn, wk_ref[...])
    v_t = _dot_nt(wv_ref[...], kvn)
    k_pe = _rope(z[:, Q_RANK + KV_RANK:Q_RANK + KV_RANK + LANES], cos4_t.T, sin4_t.T).astype(BF16)
    for h in range(N_HEADS):
        k_ref[0, h, :, :HEAD_DIM] = k_nope[:, h * HEAD_DIM:(h + 1) * HEAD_DIM].astype(BF16)
        k_ref[0, h, :, HEAD_DIM:] = k_pe
        v_ref[0, h, 0] = v_t[h * HEAD_DIM:(h + 1) * HEAD_DIM].astype(BF16)

    zp = z[:, D_MODEL - POOL_WIDTH:]

    @pl.when(seq_tile == 0)
    def _():
        ext_ref[0:POOL_HALO, :] = jnp.zeros((POOL_HALO, POOL_WIDTH), F32)

    ext_ref[POOL_HALO:, :] = zp
    t1 = (seq_tile * tm + 1 + lax.broadcasted_iota(jnp.int32, (tm, 1), 0)).astype(F32)
    diffs = []
    for g, w in enumerate(POOL_WINDOWS):
        cols = slice(g * POOL_CH, (g + 1) * POOL_CH)
        tot = zp[:, cols]
        for back in range(1, w):
            tot = tot + ext_ref[POOL_HALO - back:POOL_HALO - back + tm, cols]
        mean = tot / jnp.minimum(t1, float(w))
        diffs.append(mean - zp[:, cols])
    ext_ref[0:POOL_HALO, :] = ext_ref[tm:tm + POOL_HALO, :]

    y01 = _dot(jnp.concatenate(diffs[:2], axis=1).astype(BF16), poolw_ref[0])
    y23 = _dot(jnp.concatenate(diffs[2:], axis=1).astype(BF16), poolw_ref[1])
    p = jnp.concatenate([y01, y23], axis=1) * pscale_ref[...]
    h_ref[...] = h1 + _dot(p.astype(BF16), woutp_ref[...])


def _const_spec(shape):
    nd = len(shape)
    return pl.BlockSpec(shape, lambda *_: (0,) * nd, pipeline_mode=pl.Buffered(1))


def _pre(x2d, cos, sin, n1, wg, wu, wd, mixn, win, qn, wq, kvn, wk, wv, poolw, pscale, woutp, batch, seq):
    tm = TOKEN_TILE
    assert tm == KV_TILE
    n_tok = x2d.shape[0]
    tiles_per_seq = seq // tm
    tok_spec = lambda width: pl.BlockSpec((tm, width), lambda i: (i, 0))
    rope_spec = pl.BlockSpec((ROPE_HALF, tm), lambda i: (0, i))
    bt = lambda i: (i // tiles_per_seq, i % tiles_per_seq)
    q_spec = pl.BlockSpec((1, N_HEADS, QK_DIM, tm), lambda i: (bt(i)[0], 0, 0, bt(i)[1]))
    k_spec = pl.BlockSpec((1, N_HEADS, tm, QK_DIM), lambda i: (bt(i)[0], 0, bt(i)[1], 0))
    v_spec = pl.BlockSpec((1, N_HEADS, 1, HEAD_DIM, tm), lambda i: (bt(i)[0], 0, bt(i)[1], 0, 0))
    weights = (n1, wg, wu, wd, mixn, win, qn, wq, kvn, wk, wv, poolw, pscale, woutp)
    return pl.pallas_call(
        functools.partial(_pre_body, tiles_per_seq),
        grid=(n_tok // tm,),
        in_specs=[tok_spec(D_MODEL), rope_spec, rope_spec] + [_const_spec(w.shape) for w in weights],
        out_specs=[tok_spec(D_MODEL), q_spec, k_spec, v_spec],
        out_shape=[jax.ShapeDtypeStruct((n_tok, D_MODEL), F32),
                   jax.ShapeDtypeStruct((batch, N_HEADS, QK_DIM, seq), BF16),
                   jax.ShapeDtypeStruct((batch, N_HEADS, seq, QK_DIM), BF16),
                   jax.ShapeDtypeStruct((batch, N_HEADS, tiles_per_seq, HEAD_DIM, tm), BF16)],
        scratch_shapes=[pltpu.VMEM((tm + POOL_HALO, POOL_WIDTH), F32)],
        compiler_params=pltpu.CompilerParams(dimension_semantics=("arbitrary",),
                                             vmem_limit_bytes=VMEM_LIMIT_BYTES),
        name="pre",
    )(x2d, cos, sin, *weights)


def _attn_body(q_ref, k_ref, v_ref, o_ref, m_ref, l_ref, acc_ref):
    qi = pl.program_id(1)
    tq = q_ref.shape[3]
    tk = KV_TILE
    key_idx = lax.broadcasted_iota(jnp.int32, (tk, tq), 0)
    qry_idx = lax.broadcasted_iota(jnp.int32, (tk, tq), 1)
    causal = key_idx <= qry_idx

    m_ref[...] = jnp.full(m_ref.shape, -jnp.inf, F32)
    l_ref[...] = jnp.zeros(l_ref.shape, F32)
    acc_ref[...] = jnp.zeros(acc_ref.shape, F32)

    def update(blocks):
        scores = [[_dot(k_ref[0, h, pl.ds(pl.multiple_of(c * tk, tk), tk), :], q_ref[0, h])
                   for c, _ in blocks] for h in range(N_HEADS)]
        alphas, probs = [], []
        for h in range(N_HEADS):
            ss = [jnp.where(causal, s, -jnp.inf) if masked else s
                  for s, (_, masked) in zip(scores[h], blocks)]
            m_old = m_ref[h]
            m_new = m_old
            for s in ss:
                m_new = jnp.maximum(m_new, jnp.max(s, axis=0, keepdims=True))
            alpha = jnp.exp2(m_old - m_new)
            ps = [jnp.exp2(s - m_new) for s in ss]
            l_new = alpha * l_ref[h]
            for p in ps:
                l_new = l_new + jnp.sum(p, axis=0, keepdims=True)
            m_ref[h] = m_new
            l_ref[h] = l_new
            alphas.append(alpha)
            probs.append([p.astype(BF16) for p in ps])
        for h in range(N_HEADS):
            acc = alphas[h] * acc_ref[h]
            for (c, _), p in zip(blocks, probs[h]):
                acc = acc + _dot(v_ref[0, h, c], p)
            acc_ref[h] = acc

    def pair(j, _):
        update([(2 * j, False), (2 * j + 1, False)])
        return 0

    lax.fori_loop(0, lax.shift_right_logical(qi, 1), pair, 0)

    @pl.when(qi % 2 == 1)
    def _():
        update([(qi - 1, False), (qi, True)])

    @pl.when(qi % 2 == 0)
    def _():
        update([(qi, True)])

    for h in range(N_HEADS):
        o_ref[0, :, h * HEAD_DIM:(h + 1) * HEAD_DIM] = (acc_ref[h] / l_ref[h]).T.astype(BF16)


def _mla_attention(q, k, v):
    batch, _, seq, _ = k.shape
    assert Q_TILE == KV_TILE
    return pl.pallas_call(
        _attn_body,
        grid=(batch, seq // Q_TILE),
        in_specs=[pl.BlockSpec((1, N_HEADS, QK_DIM, Q_TILE), lambda b, i: (b, 0, 0, i)),
                  pl.BlockSpec((1, N_HEADS, seq, QK_DIM), lambda b, i: (b, 0, 0, 0)),
                  pl.BlockSpec((1, N_HEADS, seq // KV_TILE, HEAD_DIM, KV_TILE), lambda b, i: (b, 0, 0, 0, 0))],
        out_specs=pl.BlockSpec((1, Q_TILE, ATT_WIDTH), lambda b, i: (b, i, 0)),
        out_shape=jax.ShapeDtypeStruct((batch, seq, ATT_WIDTH), BF16),
        scratch_shapes=[pltpu.VMEM((N_HEADS, 1, Q_TILE), F32),
                        pltpu.VMEM((N_HEADS, 1, Q_TILE), F32),
                        pltpu.VMEM((N_HEADS, HEAD_DIM, Q_TILE), F32)],
        compiler_params=pltpu.CompilerParams(dimension_semantics=("arbitrary", "arbitrary"),
                                             vmem_limit_bytes=VMEM_LIMIT_BYTES),
        name="mla_attn",
    )(q, k, v)


def _post_body(h_ref, a_ref, mkv_ref, wouta_ref, xn_ref, wmq_ref, wmo_ref,
               n2_ref, wg_ref, wu_ref, wd_ref, fn_ref, out_ref):
    h2 = h_ref[...] + _dot(a_ref[...], wouta_ref[...])

    hn = _rmsnorm(h2, xn_ref[...]).astype(BF16)
    q = (_dot(hn, wmq_ref[...]) * MEM_SCALE).astype(BF16)
    heads = []
    for h in range(MEM_HEADS):
        cols = slice(h * MEM_HEAD_DIM, (h + 1) * MEM_HEAD_DIM)
        k = mkv_ref[0, :, cols]
        v = mkv_ref[0, :, D_MODEL + h * MEM_HEAD_DIM:D_MODEL + (h + 1) * MEM_HEAD_DIM]
        s = _dot_nt(q[:, cols], k)
        e = jnp.exp(s - jnp.max(s, axis=-1, keepdims=True))
        p = e / jnp.sum(e, axis=-1, keepdims=True)
        heads.append(_dot(p.astype(BF16), v).astype(BF16))
    h3 = h2 + _dot(jnp.concatenate(heads, axis=1), wmo_ref[...])

    xn = _rmsnorm(h3, n2_ref[...]).astype(BF16)
    h4 = h3 + 0.5 * _swiglu(xn, wg_ref, wu_ref, wd_ref)
    out_ref[...] = _rmsnorm(h4, fn_ref[...])


def _post(h2d, a2d, mkv, wouta, xn, wmq, wmo, n2, wg, wu, wd, fn, seq):
    tm = TOKEN_TILE
    n_tok = h2d.shape[0]
    tiles_per_seq = seq // tm
    weights = (wouta, xn, wmq, wmo, n2, wg, wu, wd, fn)
    return pl.pallas_call(
        _post_body,
        grid=(n_tok // tm,),
        in_specs=[pl.BlockSpec((tm, D_MODEL), lambda i: (i, 0)),
                  pl.BlockSpec((tm, ATT_WIDTH), lambda i: (i, 0)),
                  pl.BlockSpec((1, N_MEM, 2 * D_MODEL), lambda i: (i // tiles_per_seq, 0, 0))]
                 + [_const_spec(w.shape) for w in weights],
        out_specs=pl.BlockSpec((tm, D_MODEL), lambda i: (i, 0)),
        out_shape=jax.ShapeDtypeStruct((n_tok, D_MODEL), F32),
        compiler_params=pltpu.CompilerParams(dimension_semantics=("arbitrary",),
                                             vmem_limit_bytes=VMEM_LIMIT_BYTES),
        name="post",
    )(h2d, a2d, mkv, *weights)


def _pad_rope_cols(w):
    zeros = jnp.zeros(w.shape[:-1] + (ROPE_HALF,), w.dtype)
    return jnp.concatenate([w[..., :ROPE_HALF], zeros, w[..., ROPE_HALF:], zeros], axis=-1)


def _block_diag2(a, b):
    za = jnp.zeros((a.shape[0], b.shape[1]), a.dtype)
    zb = jnp.zeros((b.shape[0], a.shape[1]), a.dtype)
    return jnp.concatenate([jnp.concatenate([a, za], axis=1), jnp.concatenate([zb, b], axis=1)], axis=0)


def _layer(h2d, mem, cos, sin, batch, seq, ffn1_norm, ffn1_w_gate, ffn1_w_up, ffn1_w_down, mix_norm, w_in,
           q_norm, w_q_up, kv_norm, w_kv_up, pool_w, pool_scale, w_out, xattn_norm, mem_norm,
           w_mq, w_mkv, w_mo, ffn2_norm, ffn2_w_gate, ffn2_w_up, ffn2_w_down, out_norm):
    row = lambda g: g.reshape(1, -1).astype(F32)
    rope_lo = Q_RANK + KV_RANK
    w_in_p = jnp.concatenate([w_in[:, :rope_lo], _pad_rope_cols(w_in[:, rope_lo:rope_lo + ROPE_DIM]),
                              w_in[:, rope_lo + ROPE_DIM:]], axis=1).astype(BF16)
    wq = w_q_up.reshape(Q_RANK, N_HEADS, HEAD_DIM + ROPE_DIM)
    wq_p = jnp.concatenate([wq[..., :HEAD_DIM], _pad_rope_cols(wq[..., HEAD_DIM:])], axis=-1)
    wq_t = wq_p.reshape(Q_RANK, N_HEADS * QK_DIM).T.astype(BF16)
    wkv = w_kv_up.reshape(KV_RANK, N_HEADS, 2 * HEAD_DIM)
    wk = wkv[..., :HEAD_DIM].reshape(KV_RANK, ATT_WIDTH).astype(BF16)
    wv_t = wkv[..., HEAD_DIM:].reshape(KV_RANK, ATT_WIDTH).T.astype(BF16)
    poolw = jnp.stack([_block_diag2(pool_w[0], pool_w[1]), _block_diag2(pool_w[2], pool_w[3])]).astype(BF16)

    mkv = _mem_kv(mem, row(mem_norm), w_mkv.astype(BF16))
    h1p, q, k, v = _pre(h2d, cos, sin, row(ffn1_norm), ffn1_w_gate.astype(BF16), ffn1_w_up.astype(BF16),
                        ffn1_w_down.astype(BF16), row(mix_norm), w_in_p, row(q_norm), wq_t, row(kv_norm),
                        wk, wv_t, poolw, row(pool_scale), w_out[ATT_WIDTH:].astype(BF16), batch, seq)
    a = _mla_attention(q, k, v)
    return _post(h1p, a.reshape(batch * seq, ATT_WIDTH), mkv, w_out[:ATT_WIDTH].astype(BF16),
                 row(xattn_norm), w_mq.astype(BF16), w_mo.astype(BF16), row(ffn2_norm),
                 ffn2_w_gate.astype(BF16), ffn2_w_up.astype(BF16), ffn2_w_down.astype(BF16), row(out_norm), seq)


def kernel(x, mem, positions, ffn1_norm, ffn1_w_gate, ffn1_w_up, ffn1_w_down, mix_norm, w_in, q_norm, w_q_up,
           kv_norm, w_kv_up, pool_w, pool_scale, w_out, xattn_norm, mem_norm, w_mq, w_mkv, w_mo, ffn2_norm,
           ffn2_w_gate, ffn2_w_up, ffn2_w_down, final_norm):
    batch, seq, d_model = x.shape
    depth = ffn1_norm.shape[0]
    assert d_model == D_MODEL and depth == 1 and seq % TOKEN_TILE == 0 and seq % Q_TILE == 0
    cos, sin = _rope_tables(positions)
    layer = (ffn1_norm, ffn1_w_gate, ffn1_w_up, ffn1_w_down, mix_norm, w_in, q_norm, w_q_up, kv_norm, w_kv_up,
             pool_w, pool_scale, w_out, xattn_norm, mem_norm, w_mq, w_mkv, w_mo, ffn2_norm, ffn2_w_gate,
             ffn2_w_up, ffn2_w_down)
    out = _layer(x.reshape(batch * seq, d_model), mem, cos, sin, batch, seq, *[w[0] for w in layer], final_norm)
    return out.reshape(batch, seq, d_model)
```

```python
import functools

import jax
import jax.numpy as jnp
from jax import lax
from jax.experimental import pallas as pl
from jax.experimental.pallas import tpu as pltpu

F32 = jnp.float32
BF16 = jnp.bfloat16

D_MODEL = 1024
N_HEADS = 4
HEAD_DIM = 128
ROPE_DIM = 64
ROPE_HALF = ROPE_DIM // 2
Q_RANK = 256
KV_RANK = 128
ATT_WIDTH = N_HEADS * HEAD_DIM
POOL_WIDTH = D_MODEL - ATT_WIDTH
POOL_WINDOWS = (2, 4, 8, 16)
POOL_CH = POOL_WIDTH // len(POOL_WINDOWS)
POOL_HALO = 16
D_FF = 2816
N_MEM = 256
MEM_HEADS = 4
MEM_HEAD_DIM = D_MODEL // MEM_HEADS
ROPE_BASE = 10000.0
RMS_EPS = 1e-6
ATT_SCALE = (HEAD_DIM + ROPE_DIM) ** -0.5
LOG2_E = 1.4426950408889634
Q_PRESCALE = ATT_SCALE * LOG2_E
MEM_SCALE = MEM_HEAD_DIM ** -0.5

LANES = 128
MXU_DIM = 256
VMEM_LIMIT_BYTES = 56 * 1024 * 1024

TOKEN_TILE = 512
FF_CHUNK = MXU_DIM
Q_TILE = 512
KV_TILE = 512
QK_DIM = 2 * HEAD_DIM
STAT_ROWS = 8
MAX_SINGLE_PASS_JUMP = 32.0


def _rmsnorm(x, g):
    ms = jnp.mean(x * x, axis=-1, keepdims=True)
    return (x * lax.rsqrt(ms + RMS_EPS)) * g


def _dot(a, b):
    return jnp.dot(a, b, preferred_element_type=F32)


def _dot_nt(a, b):
    return lax.dot_general(a, b, (((1,), (1,)), ((), ())), preferred_element_type=F32)


def _swiglu(xn, wg_ref, wu_ref, wd_ref):
    n_chunks = D_FF // FF_CHUNK
    cols = lambda c: slice(c * FF_CHUNK, (c + 1) * FF_CHUNK)
    gate_up = lambda c: (_dot(xn, wg_ref[:, cols(c)]), _dot(xn, wu_ref[:, cols(c)]))
    acc = None
    g, u = gate_up(0)
    for c in range(n_chunks):
        nxt = gate_up(c + 1) if c + 1 < n_chunks else None
        a = (g / (1.0 + jnp.exp(-g))) * u
        d = _dot(a.astype(BF16), wd_ref[cols(c), :])
        acc = d if acc is None else acc + d
        if nxt is not None:
            g, u = nxt
    return acc


def _rope(x, cos4, sin4):
    return x * cos4 + pltpu.roll(x, 2 * ROPE_HALF, 1) * sin4


def _rope_table_body(pos_ref, freq_ref, cos_ref, sin_ref):
    ang = freq_ref[...] * pos_ref[...].astype(F32)
    cos_ref[...] = jnp.cos(ang)
    sin_ref[...] = jnp.sin(ang)


def _rope_tables(positions):
    n_tok = positions.size
    freqs = 1.0 / (ROPE_BASE ** (jnp.arange(0, ROPE_DIM, 2, dtype=F32) / ROPE_DIM))
    blk = 2048
    return pl.pallas_call(
        _rope_table_body,
        grid=(n_tok // blk,),
        in_specs=[pl.BlockSpec((1, blk), lambda i: (0, i)),
                  pl.BlockSpec((ROPE_HALF, 1), lambda i: (0, 0))],
        out_specs=[pl.BlockSpec((ROPE_HALF, blk), lambda i: (0, i))] * 2,
        out_shape=[jax.ShapeDtypeStruct((ROPE_HALF, n_tok), F32)] * 2,
        name="rope_tables",
    )(positions.reshape(1, n_tok), freqs.reshape(ROPE_HALF, 1))


def _mem_kv_body(mem_ref, g_ref, w_ref, out_ref):
    mn = _rmsnorm(mem_ref[0], g_ref[...]).astype(BF16)
    out_ref[0] = _dot(mn, w_ref[...]).astype(BF16)


def _mem_kv(mem, mem_norm, w_mkv):
    batch = mem.shape[0]
    return pl.pallas_call(
        _mem_kv_body,
        grid=(batch,),
        in_specs=[pl.BlockSpec((1, N_MEM, D_MODEL), lambda b: (b, 0, 0)),
                  pl.BlockSpec((1, D_MODEL), lambda b: (0, 0)),
                  pl.BlockSpec((D_MODEL, 2 * D_MODEL), lambda b: (0, 0))],
        out_specs=pl.BlockSpec((1, N_MEM, 2 * D_MODEL), lambda b: (b, 0, 0)),
        out_shape=jax.ShapeDtypeStruct((batch, N_MEM, 2 * D_MODEL), BF16),
        name="mem_kv",
    )(mem, mem_norm, w_mkv)


def _pre_body(tiles_per_seq,
              x_ref, cos_ref, sin_ref, n1_ref, wg_ref, wu_ref, wd_ref, mixn_ref, win_ref,
              qn_ref, wq_ref, kvn_ref, wk_ref, wv_ref, poolw_ref, pscale_ref, woutp_ref,
              h_ref, q_ref, k_ref, v_ref, ext_ref):
    seq_tile = pl.program_id(0) % tiles_per_seq
    tm = x_ref.shape[0]

    x = x_ref[...]
    xn = _rmsnorm(x, n1_ref[...]).astype(BF16)
    h1 = x + 0.5 * _swiglu(xn, wg_ref, wu_ref, wd_ref)

    un = _rmsnorm(h1, mixn_ref[...]).astype(BF16)
    z = _dot(un, win_ref[...])

    cos = cos_ref[...]
    sin = sin_ref[...]
    cos4_t = jnp.concatenate([cos, cos, cos, cos], axis=0)
    sin4_t = jnp.concatenate([-sin, -sin, sin, sin], axis=0)

    qn = _rmsnorm(z[:, :Q_RANK], qn_ref[...]).astype(BF16)
    q_t = _dot_nt(wq_ref[...], qn)
    for h in range(N_HEADS):
        base = h * QK_DIM
        q_ref[0, h, :HEAD_DIM, :] = (q_t[base:base + HEAD_DIM] * Q_PRESCALE).astype(BF16)
        pe = q_t[base + HEAD_DIM:base + QK_DIM]
        pe = pe * cos4_t + pltpu.roll(pe, 2 * ROPE_HALF, 0) * sin4_t
        q_ref[0, h, HEAD_DIM:, :] = (pe * Q_PRESCALE).astype(BF16)

    kvn = _rmsnorm(z[:, Q_RANK:Q_RANK + KV_RANK], kvn_ref[...]).astype(BF16)
    k_nope = _dot(kvn, wk_ref[...])
    v_t = _dot_nt(wv_ref[...], kvn)
    k_pe = _rope(z[:, Q_RANK + KV_RANK:Q_RANK + KV_RANK + LANES], cos4_t.T, sin4_t.T).astype(BF16)
    for h in range(N_HEADS):
        k_ref[0, h, :, :HEAD_DIM] = k_nope[:, h * HEAD_DIM:(h + 1) * HEAD_DIM].astype(BF16)
        k_ref[0, h, :, HEAD_DIM:] = k_pe
        v_ref[0, h, 0] = v_t[h * HEAD_DIM:(h + 1) * HEAD_DIM].astype(BF16)

    zp = z[:, D_MODEL - POOL_WIDTH:]

    @pl.when(seq_tile == 0)
    def _():
        ext_ref[0:POOL_HALO, :] = jnp.zeros((POOL_HALO, POOL_WIDTH), F32)

    ext_ref[POOL_HALO:, :] = zp
    t1 = (seq_tile * tm + 1 + lax.broadcasted_iota(jnp.int32, (tm, 1), 0)).astype(F32)
    diffs = []
    for g, w in enumerate(POOL_WINDOWS):
        cols = slice(g * POOL_CH, (g + 1) * POOL_CH)
        tot = zp[:, cols]
        for back in range(1, w):
            tot = tot + ext_ref[POOL_HALO - back:POOL_HALO - back + tm, cols]
        mean = tot / jnp.minimum(t1, float(w))
        diffs.append(mean - zp[:, cols])
    ext_ref[0:POOL_HALO, :] = ext_ref[tm:tm + POOL_HALO, :]

    y01 = _dot(jnp.concatenate(diffs[:2], axis=1).astype(BF16), poolw_ref[0])
    y23 = _dot(jnp.concatenate(diffs[2:], axis=1).astype(BF16), poolw_ref[1])
    p = jnp.concatenate([y01, y23], axis=1) * pscale_ref[...]
    h_ref[...] = h1 + _dot(p.astype(BF16), woutp_ref[...])


def _const_spec(shape):
    nd = len(shape)
    return pl.BlockSpec(shape, lambda *_: (0,) * nd, pipeline_mode=pl.Buffered(1))


def _pre(x2d, cos, sin, n1, wg, wu, wd, mixn, win, qn, wq, kvn, wk, wv, poolw, pscale, woutp, batch, seq):
    tm = TOKEN_TILE
    assert tm == KV_TILE
    n_tok = x2d.shape[0]
    tiles_per_seq = seq // tm
    tok_spec = lambda width: pl.BlockSpec((tm, width), lambda i: (i, 0))
    rope_spec = pl.BlockSpec((ROPE_HALF, tm), lambda i: (0, i))
    bt = lambda i: (i // tiles_per_seq, i % tiles_per_seq)
    q_spec = pl.BlockSpec((1, N_HEADS, QK_DIM, tm), lambda i: (bt(i)[0], 0, 0, bt(i)[1]))
    k_spec = pl.BlockSpec((1, N_HEADS, tm, QK_DIM), lambda i: (bt(i)[0], 0, bt(i)[1], 0))
    v_spec = pl.BlockSpec((1, N_HEADS, 1, HEAD_DIM, tm), lambda i: (bt(i)[0], 0, bt(i)[1], 0, 0))
    weights = (n1, wg, wu, wd, mixn, win, qn, wq, kvn, wk, wv, poolw, pscale, woutp)
    return pl.pallas_call(
        functools.partial(_pre_body, tiles_per_seq),
        grid=(n_tok // tm,),
        in_specs=[tok_spec(D_MODEL), rope_spec, rope_spec] + [_const_spec(w.shape) for w in weights],
        out_specs=[tok_spec(D_MODEL), q_spec, k_spec, v_spec],
        out_shape=[jax.ShapeDtypeStruct((n_tok, D_MODEL), F32),
                   jax.ShapeDtypeStruct((batch, N_HEADS, QK_DIM, seq), BF16),
                   jax.ShapeDtypeStruct((batch, N_HEADS, seq, QK_DIM), BF16),
                   jax.ShapeDtypeStruct((batch, N_HEADS, tiles_per_seq, HEAD_DIM, tm), BF16)],
        scratch_shapes=[pltpu.VMEM((tm + POOL_HALO, POOL_WIDTH), F32)],
        compiler_params=pltpu.CompilerParams(dimension_semantics=("arbitrary",),
                                             vmem_limit_bytes=VMEM_LIMIT_BYTES),
        name="pre",
    )(x2d, cos, sin, *weights)


def _attn_body(q_ref, k_ref, v_ref, o_ref, m_ref, l_ref, acc_ref, gap_ref):
    qi = pl.program_id(1)
    tq = q_ref.shape[3]
    tk = KV_TILE
    half = tk // 2

    def scores(h, c):
        return _dot(k_ref[0, h, pl.ds(pl.multiple_of(c * tk, tk), tk), :], q_ref[0, h])


    def init_with_diagonal():
        causal = (lax.broadcasted_iota(jnp.int32, (half, half), 0)
                  <= lax.broadcasted_iota(jnp.int32, (half, half), 1))
        start = pl.multiple_of(qi * tk, tk)
        s_lo = [_dot(k_ref[0, h, pl.ds(start, half), :], q_ref[0, h]) for h in range(N_HEADS)]
        s_hi = [_dot(k_ref[0, h, pl.ds(start + half, half), :], q_ref[0, h, :, half:]) for h in range(N_HEADS)]
        probs = []
        for h in range(N_HEADS):
            s_ll = jnp.where(causal, s_lo[h][:, :half], -jnp.inf)
            s_lu = s_lo[h][:, half:]
            s_uu = jnp.where(causal, s_hi[h], -jnp.inf)
            m_l = jnp.max(s_ll, axis=0, keepdims=True)
            m_u = jnp.maximum(jnp.max(s_lu, axis=0, keepdims=True), jnp.max(s_uu, axis=0, keepdims=True))
            p_ll = jnp.exp2(s_ll - m_l)
            p_lu = jnp.exp2(s_lu - m_u)
            p_uu = jnp.exp2(s_uu - m_u)
            l_l = jnp.sum(p_ll, axis=0, keepdims=True)
            l_u = jnp.sum(p_lu, axis=0, keepdims=True) + jnp.sum(p_uu, axis=0, keepdims=True)
            m_ref[h, :, :half] = jnp.broadcast_to(m_l, (STAT_ROWS, half))
            m_ref[h, :, half:] = jnp.broadcast_to(m_u, (STAT_ROWS, half))
            l_ref[h, :, :half] = jnp.broadcast_to(l_l, (STAT_ROWS, half))
            l_ref[h, :, half:] = jnp.broadcast_to(l_u, (STAT_ROWS, half))
            probs.append((jnp.concatenate([p_ll, p_lu], axis=1).astype(BF16), p_uu.astype(BF16)))
        for h in range(N_HEADS):
            v_t = v_ref[0, h, qi]
            acc_l = _dot(v_t[:, :half], probs[h][0])
            acc_u = _dot(v_t[:, half:], probs[h][1])
            acc_ref[h, :, :half] = acc_l[:, :half]
            acc_ref[h, :, half:] = acc_l[:, half:] + acc_u

    def update_single_pass(blocks):
        s_all = [[scores(h, c) for c in blocks] for h in range(N_HEADS)]
        betas, probs = [], []
        for h in range(N_HEADS):
            m_old = m_ref[h, 0:1, :]
            ps = [jnp.exp2(s - m_old) for s in s_all[h]]
            m_blk = jnp.max(s_all[h][0], axis=0, keepdims=True)
            for s in s_all[h][1:]:
                m_blk = jnp.maximum(m_blk, jnp.max(s, axis=0, keepdims=True))
            m_new = jnp.maximum(m_old, m_blk)
            beta = jnp.exp2(m_old - m_new)
            l_new = l_ref[h, 0:1, :]
            for p in ps:
                l_new = l_new + jnp.sum(p, axis=0, keepdims=True)
            m_ref[h] = jnp.broadcast_to(m_new, (STAT_ROWS, tq))
            l_ref[h] = jnp.broadcast_to(l_new * beta, (STAT_ROWS, tq))
            gap_ref[h] = jnp.maximum(gap_ref[h], jnp.broadcast_to(m_blk - m_old, (STAT_ROWS, tq)))
            betas.append(beta)
            probs.append([p.astype(BF16) for p in ps])
        for h in range(N_HEADS):
            acc = acc_ref[h]
            for c, p in zip(blocks, probs[h]):
                acc = acc + _dot(v_ref[0, h, c], p)
            acc_ref[h] = acc * betas[h]

    def update_two_pass(c):
        s_all = [scores(h, c) for h in range(N_HEADS)]
        alphas, probs = [], []
        for h in range(N_HEADS):
            m_old = m_ref[h, 0:1, :]
            m_new = jnp.maximum(m_old, jnp.max(s_all[h], axis=0, keepdims=True))
            alpha = jnp.exp2(m_old - m_new)
            p = jnp.exp2(s_all[h] - m_new)
            l_new = alpha * l_ref[h, 0:1, :] + jnp.sum(p, axis=0, keepdims=True)
            m_ref[h] = jnp.broadcast_to(m_new, (STAT_ROWS, tq))
            l_ref[h] = jnp.broadcast_to(l_new, (STAT_ROWS, tq))
            alphas.append(alpha)
            probs.append(p.astype(BF16))
        for h in range(N_HEADS):
            acc_ref[h] = alphas[h] * acc_ref[h] + _dot(v_ref[0, h, c], probs[h])

    init_with_diagonal()
    gap_ref[...] = jnp.zeros(gap_ref.shape, F32)

    def pair(j, _):
        update_single_pass([2 * j, 2 * j + 1])
        return 0

    lax.fori_loop(0, lax.shift_right_logical(qi, 1), pair, 0)

    @pl.when(qi % 2 == 1)
    def _():
        update_single_pass([qi - 1])

    worst = gap_ref[0]
    for h in range(1, N_HEADS):
        worst = jnp.maximum(worst, gap_ref[h])

    @pl.when(jnp.max(worst) > MAX_SINGLE_PASS_JUMP)
    def _():
        init_with_diagonal()

        def one(c, _):
            update_two_pass(c)
            return 0

        lax.fori_loop(0, qi, one, 0)

    for h in range(N_HEADS):
        o_ref[0, :, h * HEAD_DIM:(h + 1) * HEAD_DIM] = (acc_ref[h] / l_ref[h, 0:1, :]).T.astype(BF16)


def _mla_attention(q, k, v):
    batch, _, seq, _ = k.shape
    assert Q_TILE == KV_TILE
    return pl.pallas_call(
        _attn_body,
        grid=(batch, seq // Q_TILE),
        in_specs=[pl.BlockSpec((1, N_HEADS, QK_DIM, Q_TILE), lambda b, i: (b, 0, 0, i)),
                  pl.BlockSpec((1, N_HEADS, seq, QK_DIM), lambda b, i: (b, 0, 0, 0)),
                  pl.BlockSpec((1, N_HEADS, seq // KV_TILE, HEAD_DIM, KV_TILE), lambda b, i: (b, 0, 0, 0, 0))],
        out_specs=pl.BlockSpec((1, Q_TILE, ATT_WIDTH), lambda b, i: (b, i, 0)),
        out_shape=jax.ShapeDtypeStruct((batch, seq, ATT_WIDTH), BF16),
        scratch_shapes=[pltpu.VMEM((N_HEADS, STAT_ROWS, Q_TILE), F32),
                        pltpu.VMEM((N_HEADS, STAT_ROWS, Q_TILE), F32),
                        pltpu.VMEM((N_HEADS, HEAD_DIM, Q_TILE), F32),
                        pltpu.VMEM((N_HEADS, STAT_ROWS, Q_TILE), F32)],
        compiler_params=pltpu.CompilerParams(dimension_semantics=("arbitrary", "arbitrary"),
                                             vmem_limit_bytes=VMEM_LIMIT_BYTES),
        name="mla_attn",
    )(q, k, v)


def _post_body(h_ref, a_ref, mkv_ref, wouta_ref, xn_ref, wmq_ref, wmo_ref,
               n2_ref, wg_ref, wu_ref, wd_ref, fn_ref, out_ref):
    h2 = h_ref[...] + _dot(a_ref[...], wouta_ref[...])

    hn = _rmsnorm(h2, xn_ref[...]).astype(BF16)
    q = (_dot(hn, wmq_ref[...]) * MEM_SCALE).astype(BF16)
    cols = [slice(h * MEM_HEAD_DIM, (h + 1) * MEM_HEAD_DIM) for h in range(MEM_HEADS)]
    scores = [_dot_nt(q[:, c], mkv_ref[0, :, c]) for c in cols]
    probs = []
    for s in scores:
        e = jnp.exp(s - jnp.max(s, axis=-1, keepdims=True))
        probs.append((e / jnp.sum(e, axis=-1, keepdims=True)).astype(BF16))
    heads = [_dot(p, mkv_ref[0, :, D_MODEL + h * MEM_HEAD_DIM:D_MODEL + (h + 1) * MEM_HEAD_DIM]).astype(BF16)
             for h, p in enumerate(probs)]
    h3 = h2 + _dot(jnp.concatenate(heads, axis=1), wmo_ref[...])

    xn = _rmsnorm(h3, n2_ref[...]).astype(BF16)
    h4 = h3 + 0.5 * _swiglu(xn, wg_ref, wu_ref, wd_ref)
    out_ref[...] = _rmsnorm(h4, fn_ref[...])


def _post(h2d, a2d, mkv, wouta, xn, wmq, wmo, n2, wg, wu, wd, fn, seq):
    tm = TOKEN_TILE
    n_tok = h2d.shape[0]
    tiles_per_seq = seq // tm
    weights = (wouta, xn, wmq, wmo, n2, wg, wu, wd, fn)
    return pl.pallas_call(
        _post_body,
        grid=(n_tok // tm,),
        in_specs=[pl.BlockSpec((tm, D_MODEL), lambda i: (i, 0)),
                  pl.BlockSpec((tm, ATT_WIDTH), lambda i: (i, 0)),
                  pl.BlockSpec((1, N_MEM, 2 * D_MODEL), lambda i: (i // tiles_per_seq, 0, 0))]
                 + [_const_spec(w.shape) for w in weights],
        out_specs=pl.BlockSpec((tm, D_MODEL), lambda i: (i, 0)),
        out_shape=jax.ShapeDtypeStruct((n_tok, D_MODEL), F32),
        compiler_params=pltpu.CompilerParams(dimension_semantics=("arbitrary",),
                                             vmem_limit_bytes=VMEM_LIMIT_BYTES),
        name="post",
    )(h2d, a2d, mkv, *weights)


def _pad_rope_cols(w):
    zeros = jnp.zeros(w.shape[:-1] + (ROPE_HALF,), w.dtype)
    return jnp.concatenate([w[..., :ROPE_HALF], zeros, w[..., ROPE_HALF:], zeros], axis=-1)


def _block_diag2(a, b):
    za = jnp.zeros((a.shape[0], b.shape[1]), a.dtype)
    zb = jnp.zeros((b.shape[0], a.shape[1]), a.dtype)
    return jnp.concatenate([jnp.concatenate([a, za], axis=1), jnp.concatenate([zb, b], axis=1)], axis=0)


def _layer(h2d, mem, cos, sin, batch, seq, ffn1_norm, ffn1_w_gate, ffn1_w_up, ffn1_w_down, mix_norm, w_in,
           q_norm, w_q_up, kv_norm, w_kv_up, pool_w, pool_scale, w_out, xattn_norm, mem_norm,
           w_mq, w_mkv, w_mo, ffn2_norm, ffn2_w_gate, ffn2_w_up, ffn2_w_down, out_norm):
    row = lambda g: g.reshape(1, -1).astype(F32)
    rope_lo = Q_RANK + KV_RANK
    w_in_p = jnp.concatenate([w_in[:, :rope_lo], _pad_rope_cols(w_in[:, rope_lo:rope_lo + ROPE_DIM]),
                              w_in[:, rope_lo + ROPE_DIM:]], axis=1).astype(BF16)
    wq = w_q_up.reshape(Q_RANK, N_HEADS, HEAD_DIM + ROPE_DIM)
    wq_p = jnp.concatenate([wq[..., :HEAD_DIM], _pad_rope_cols(wq[..., HEAD_DIM:])], axis=-1)
    wq_t = wq_p.reshape(Q_RANK, N_HEADS * QK_DIM).T.astype(BF16)
    wkv = w_kv_up.reshape(KV_RANK, N_HEADS, 2 * HEAD_DIM)
    wk = wkv[..., :HEAD_DIM].reshape(KV_RANK, ATT_WIDTH).astype(BF16)
    wv_t = wkv[..., HEAD_DIM:].reshape(KV_RANK, ATT_WIDTH).T.astype(BF16)
    poolw = jnp.stack([_block_diag2(pool_w[0], pool_w[1]), _block_diag2(pool_w[2], pool_w[3])]).astype(BF16)

    mkv = _mem_kv(mem, row(mem_norm), w_mkv.astype(BF16))
    h1p, q, k, v = _pre(h2d, cos, sin, row(ffn1_norm), ffn1_w_gate.astype(BF16), ffn1_w_up.astype(BF16),
                        ffn1_w_down.astype(BF16), row(mix_norm), w_in_p, row(q_norm), wq_t, row(kv_norm),
                        wk, wv_t, poolw, row(pool_scale), w_out[ATT_WIDTH:].astype(BF16), batch, seq)
    a = _mla_attention(q, k, v)
    return _post(h1p, a.reshape(batch * seq, ATT_WIDTH), mkv, w_out[:ATT_WIDTH].astype(BF16),
                 row(xattn_norm), w_mq.astype(BF16), w_mo.astype(BF16), row(ffn2_norm),
                 ffn2_w_gate.astype(BF16), ffn2_w_up.astype(BF16), ffn2_w_down.astype(BF16), row(out_norm), seq)


def kernel(x, mem, positions, ffn1_norm, ffn1_w_gate, ffn1_w_up, ffn1_w_down, mix_norm, w_in, q_norm, w_q_up,
           kv_norm, w_kv_up, pool_w, pool_scale, w_out, xattn_norm, mem_norm, w_mq, w_mkv, w_mo, ffn2_norm,
           ffn2_w_gate, ffn2_w_up, ffn2_w_down, final_norm):
    batch, seq, d_model = x.shape
    depth = ffn1_norm.shape[0]
    assert d_model == D_MODEL and depth == 1 and seq % TOKEN_TILE == 0 and seq % Q_TILE == 0
    cos, sin = _rope_tables(positions)
    layer = (ffn1_norm, ffn1_w_gate, ffn1_w_up, ffn1_w_down, mix_norm, w_in, q_norm, w_q_up, kv_norm, w_kv_up,
             pool_w, pool_scale, w_out, xattn_norm, mem_norm, w_mq, w_mkv, w_mo, ffn2_norm, ffn2_w_gate,
             ffn2_w_up, ffn2_w_down)
    out = _layer(x.reshape(batch * seq, d_model), mem, cos, sin, batch, seq, *[w[0] for w in layer], final_norm)
    return out.reshape(batch, seq, d_model)
```

```python
import functools

import jax
import jax.numpy as jnp
from jax import lax
from jax.experimental import pallas as pl
from jax.experimental.pallas import tpu as pltpu

F32 = jnp.float32
BF16 = jnp.bfloat16

D_MODEL = 1024
N_HEADS = 4
HEAD_DIM = 128
ROPE_DIM = 64
ROPE_HALF = ROPE_DIM // 2
Q_RANK = 256
KV_RANK = 128
ATT_WIDTH = N_HEADS * HEAD_DIM
POOL_WIDTH = D_MODEL - ATT_WIDTH
POOL_WINDOWS = (2, 4, 8, 16)
POOL_CH = POOL_WIDTH // len(POOL_WINDOWS)
POOL_HALO = 16
D_FF = 2816
N_MEM = 256
MEM_HEADS = 4
MEM_HEAD_DIM = D_MODEL // MEM_HEADS
ROPE_BASE = 10000.0
RMS_EPS = 1e-6
ATT_SCALE = (HEAD_DIM + ROPE_DIM) ** -0.5
LOG2_E = 1.4426950408889634
Q_PRESCALE = ATT_SCALE * LOG2_E
MEM_SCALE = MEM_HEAD_DIM ** -0.5

LANES = 128
MXU_DIM = 256
VMEM_LIMIT_BYTES = 56 * 1024 * 1024

TOKEN_TILE = 512
FF_CHUNK = MXU_DIM
Q_TILE = 512
KV_TILE = 512
QK_DIM = 2 * HEAD_DIM
CAST_ROWS_WIDE = 128
CAST_ROWS = 256
STAT_ROWS = 8
MAX_SINGLE_PASS_JUMP = 32.0


def _rmsnorm(x, g):
    ms = jnp.mean(x * x, axis=-1, keepdims=True)
    return (x * lax.rsqrt(ms + RMS_EPS)) * g


def _dot(a, b):
    return jnp.dot(a, b, preferred_element_type=F32)


def _dot_nt(a, b):
    return lax.dot_general(a, b, (((1,), (1,)), ((), ())), preferred_element_type=F32)


def _swiglu(xn, wg_ref, wu_ref, wd_ref):
    n_chunks = D_FF // FF_CHUNK
    cols = lambda c: slice(c * FF_CHUNK, (c + 1) * FF_CHUNK)
    gate_up = lambda c: (_dot(xn, wg_ref[:, cols(c)]), _dot(xn, wu_ref[:, cols(c)]))
    acc = None
    g, u = gate_up(0)
    for c in range(n_chunks):
        nxt = gate_up(c + 1) if c + 1 < n_chunks else None
        a = (g / (1.0 + jnp.exp(-g))) * u
        d = _dot(a.astype(BF16), wd_ref[cols(c), :])
        acc = d if acc is None else acc + d
        if nxt is not None:
            g, u = nxt
    return acc


def _row_jobs(src, src_row0, n_rows, dst, chunk):
    assert n_rows % chunk == 0
    return [(src, src_row0 + r, dst, r) for r in range(0, n_rows, chunk)]


def _cast_weights_into_vmem(jobs, stage, sems):
    rows = stage.shape[1]

    def copy(j):
        src, r0, _, _ = jobs[j]
        return pltpu.make_async_copy(src.at[pl.ds(r0, rows), :], stage.at[j % 2], sems.at[j % 2])

    copy(0).start()
    for j in range(len(jobs)):
        if j + 1 < len(jobs):
            copy(j + 1).start()
        copy(j).wait()
        _, _, dst, d0 = jobs[j]
        dst[d0:d0 + rows, :] = stage[j % 2].astype(BF16)


def _cast_scratch(wide_rows, narrow_rows):
    return [pltpu.VMEM((2, wide_rows, D_FF), F32), pltpu.VMEM((2, narrow_rows, D_MODEL), F32),
            pltpu.SemaphoreType.DMA((2,)), pltpu.SemaphoreType.DMA((2,))]


def _rope(x, cos4, sin4):
    return x * cos4 + pltpu.roll(x, 2 * ROPE_HALF, 1) * sin4


def _rope_table_body(pos_ref, freq_ref, cos_ref, sin_ref):
    ang = freq_ref[...] * pos_ref[...].astype(F32)
    cos_ref[...] = jnp.cos(ang)
    sin_ref[...] = jnp.sin(ang)


def _rope_tables(positions):
    n_tok = positions.size
    freqs = 1.0 / (ROPE_BASE ** (jnp.arange(0, ROPE_DIM, 2, dtype=F32) / ROPE_DIM))
    blk = 2048
    return pl.pallas_call(
        _rope_table_body,
        grid=(n_tok // blk,),
        in_specs=[pl.BlockSpec((1, blk), lambda i: (0, i)),
                  pl.BlockSpec((ROPE_HALF, 1), lambda i: (0, 0))],
        out_specs=[pl.BlockSpec((ROPE_HALF, blk), lambda i: (0, i))] * 2,
        out_shape=[jax.ShapeDtypeStruct((ROPE_HALF, n_tok), F32)] * 2,
        name="rope_tables",
    )(positions.reshape(1, n_tok), freqs.reshape(ROPE_HALF, 1))


def _mem_kv_body(mem_ref, g_ref, w_ref, out_ref):
    mn = _rmsnorm(mem_ref[0], g_ref[...]).astype(BF16)
    out_ref[0] = _dot(mn, w_ref[...]).astype(BF16)


def _mem_kv(mem, mem_norm, w_mkv):
    batch = mem.shape[0]
    return pl.pallas_call(
        _mem_kv_body,
        grid=(batch,),
        in_specs=[pl.BlockSpec((1, N_MEM, D_MODEL), lambda b: (b, 0, 0)),
                  pl.BlockSpec((1, D_MODEL), lambda b: (0, 0)),
                  pl.BlockSpec((D_MODEL, 2 * D_MODEL), lambda b: (0, 0))],
        out_specs=pl.BlockSpec((1, N_MEM, 2 * D_MODEL), lambda b: (b, 0, 0)),
        out_shape=jax.ShapeDtypeStruct((batch, N_MEM, 2 * D_MODEL), BF16),
        name="mem_kv",
    )(mem, mem_norm, w_mkv)


def _pre_body(tiles_per_seq,
              x_ref, cos_ref, sin_ref, n1_ref, mixn_ref, win_ref, qn_ref, wq_ref, kvn_ref, wk_ref, wv_ref,
              poolw_ref, pscale_ref, wg_hbm, wu_hbm, wd_hbm, wout_hbm,
              h_ref, q_ref, k_ref, v_ref,
              ext_ref, wg_ref, wu_ref, wd_ref, woutp_ref, stage_wide, stage, sem_wide, sem):
    seq_tile = pl.program_id(0) % tiles_per_seq
    tm = x_ref.shape[0]

    @pl.when(pl.program_id(0) == 0)
    def _():
        _cast_weights_into_vmem(_row_jobs(wg_hbm, 0, D_MODEL, wg_ref, CAST_ROWS_WIDE)
                                + _row_jobs(wu_hbm, 0, D_MODEL, wu_ref, CAST_ROWS_WIDE), stage_wide, sem_wide)
        _cast_weights_into_vmem(_row_jobs(wd_hbm, 0, D_FF, wd_ref, CAST_ROWS)
                                + _row_jobs(wout_hbm, ATT_WIDTH, POOL_WIDTH, woutp_ref, CAST_ROWS), stage, sem)

    x = x_ref[...]
    xn = _rmsnorm(x, n1_ref[...]).astype(BF16)
    h1 = x + 0.5 * _swiglu(xn, wg_ref, wu_ref, wd_ref)

    un = _rmsnorm(h1, mixn_ref[...]).astype(BF16)
    z = _dot(un, win_ref[...])

    cos = cos_ref[...]
    sin = sin_ref[...]
    cos4_t = jnp.concatenate([cos, cos, cos, cos], axis=0)
    sin4_t = jnp.concatenate([-sin, -sin, sin, sin], axis=0)

    qn = _rmsnorm(z[:, :Q_RANK], qn_ref[...]).astype(BF16)
    q_t = _dot_nt(wq_ref[...], qn)
    for h in range(N_HEADS):
        base = h * QK_DIM
        q_ref[0, h, :HEAD_DIM, :] = (q_t[base:base + HEAD_DIM] * Q_PRESCALE).astype(BF16)
        pe = q_t[base + HEAD_DIM:base + QK_DIM]
        pe = pe * cos4_t + pltpu.roll(pe, 2 * ROPE_HALF, 0) * sin4_t
        q_ref[0, h, HEAD_DIM:, :] = (pe * Q_PRESCALE).astype(BF16)

    kvn = _rmsnorm(z[:, Q_RANK:Q_RANK + KV_RANK], kvn_ref[...]).astype(BF16)
    k_nope = _dot(kvn, wk_ref[...])
    v_t = _dot_nt(wv_ref[...], kvn)
    k_pe = _rope(z[:, Q_RANK + KV_RANK:Q_RANK + KV_RANK + LANES], cos4_t.T, sin4_t.T).astype(BF16)
    for h in range(N_HEADS):
        k_ref[0, h, :, :HEAD_DIM] = k_nope[:, h * HEAD_DIM:(h + 1) * HEAD_DIM].astype(BF16)
        k_ref[0, h, :, HEAD_DIM:] = k_pe
        v_ref[0, h, 0] = v_t[h * HEAD_DIM:(h + 1) * HEAD_DIM].astype(BF16)

    zp = z[:, D_MODEL - POOL_WIDTH:]

    @pl.when(seq_tile == 0)
    def _():
        ext_ref[0:POOL_HALO, :] = jnp.zeros((POOL_HALO, POOL_WIDTH), F32)

    ext_ref[POOL_HALO:, :] = zp
    t1 = (seq_tile * tm + 1 + lax.broadcasted_iota(jnp.int32, (tm, 1), 0)).astype(F32)
    diffs = []
    for g, w in enumerate(POOL_WINDOWS):
        cols = slice(g * POOL_CH, (g + 1) * POOL_CH)
        tot = zp[:, cols]
        for back in range(1, w):
            tot = tot + ext_ref[POOL_HALO - back:POOL_HALO - back + tm, cols]
        mean = tot / jnp.minimum(t1, float(w))
        diffs.append(mean - zp[:, cols])
    ext_ref[0:POOL_HALO, :] = ext_ref[tm:tm + POOL_HALO, :]

    y01 = _dot(jnp.concatenate(diffs[:2], axis=1).astype(BF16), poolw_ref[0])
    y23 = _dot(jnp.concatenate(diffs[2:], axis=1).astype(BF16), poolw_ref[1])
    p = jnp.concatenate([y01, y23], axis=1) * pscale_ref[...]
    h_ref[...] = h1 + _dot(p.astype(BF16), woutp_ref[...])


def _const_spec(shape):
    nd = len(shape)
    return pl.BlockSpec(shape, lambda *_: (0,) * nd, pipeline_mode=pl.Buffered(1))


def _pre(x2d, cos, sin, n1, mixn, win, qn, wq, kvn, wk, wv, poolw, pscale, wg, wu, wd, wout, batch, seq):
    tm = TOKEN_TILE
    assert tm == KV_TILE
    n_tok = x2d.shape[0]
    tiles_per_seq = seq // tm
    tok_spec = lambda width: pl.BlockSpec((tm, width), lambda i: (i, 0))
    rope_spec = pl.BlockSpec((ROPE_HALF, tm), lambda i: (0, i))
    bt = lambda i: (i // tiles_per_seq, i % tiles_per_seq)
    q_spec = pl.BlockSpec((1, N_HEADS, QK_DIM, tm), lambda i: (bt(i)[0], 0, 0, bt(i)[1]))
    k_spec = pl.BlockSpec((1, N_HEADS, tm, QK_DIM), lambda i: (bt(i)[0], 0, bt(i)[1], 0))
    v_spec = pl.BlockSpec((1, N_HEADS, 1, HEAD_DIM, tm), lambda i: (bt(i)[0], 0, bt(i)[1], 0, 0))
    weights = (n1, mixn, win, qn, wq, kvn, wk, wv, poolw, pscale)
    hbm_weights = (wg, wu, wd, wout)
    return pl.pallas_call(
        functools.partial(_pre_body, tiles_per_seq),
        grid=(n_tok // tm,),
        in_specs=[tok_spec(D_MODEL), rope_spec, rope_spec] + [_const_spec(w.shape) for w in weights]
                 + [pl.BlockSpec(memory_space=pl.ANY)] * len(hbm_weights),
        out_specs=[tok_spec(D_MODEL), q_spec, k_spec, v_spec],
        out_shape=[jax.ShapeDtypeStruct((n_tok, D_MODEL), F32),
                   jax.ShapeDtypeStruct((batch, N_HEADS, QK_DIM, seq), BF16),
                   jax.ShapeDtypeStruct((batch, N_HEADS, seq, QK_DIM), BF16),
                   jax.ShapeDtypeStruct((batch, N_HEADS, tiles_per_seq, HEAD_DIM, tm), BF16)],
        scratch_shapes=[pltpu.VMEM((tm + POOL_HALO, POOL_WIDTH), F32),
                        pltpu.VMEM((D_MODEL, D_FF), BF16), pltpu.VMEM((D_MODEL, D_FF), BF16),
                        pltpu.VMEM((D_FF, D_MODEL), BF16), pltpu.VMEM((POOL_WIDTH, D_MODEL), BF16)]
                       + _cast_scratch(CAST_ROWS_WIDE, CAST_ROWS),
        compiler_params=pltpu.CompilerParams(dimension_semantics=("arbitrary",),
                                             vmem_limit_bytes=VMEM_LIMIT_BYTES),
        name="pre",
    )(x2d, cos, sin, *weights, *hbm_weights)


def _attn_body(q_ref, k_ref, v_ref, o_ref, m_ref, l_ref, acc_ref, gap_ref):
    qi = pl.program_id(1)
    tq = q_ref.shape[3]
    tk = KV_TILE
    half = tk // 2

    def scores(h, c):
        return _dot(k_ref[0, h, pl.ds(pl.multiple_of(c * tk, tk), tk), :], q_ref[0, h])


    def init_with_diagonal():
        causal = (lax.broadcasted_iota(jnp.int32, (half, half), 0)
                  <= lax.broadcasted_iota(jnp.int32, (half, half), 1))
        start = pl.multiple_of(qi * tk, tk)
        s_lo = [_dot(k_ref[0, h, pl.ds(start, half), :], q_ref[0, h]) for h in range(N_HEADS)]
        s_hi = [_dot(k_ref[0, h, pl.ds(start + half, half), :], q_ref[0, h, :, half:]) for h in range(N_HEADS)]
        probs = []
        for h in range(N_HEADS):
            s_ll = jnp.where(causal, s_lo[h][:, :half], -jnp.inf)
            s_lu = s_lo[h][:, half:]
            s_uu = jnp.where(causal, s_hi[h], -jnp.inf)
            m_l = jnp.max(s_ll, axis=0, keepdims=True)
            m_u = jnp.maximum(jnp.max(s_lu, axis=0, keepdims=True), jnp.max(s_uu, axis=0, keepdims=True))
            p_ll = jnp.exp2(s_ll - m_l)
            p_lu = jnp.exp2(s_lu - m_u)
            p_uu = jnp.exp2(s_uu - m_u)
            l_l = jnp.sum(p_ll, axis=0, keepdims=True)
            l_u = jnp.sum(p_lu, axis=0, keepdims=True) + jnp.sum(p_uu, axis=0, keepdims=True)
            m_ref[h, :, :half] = jnp.broadcast_to(m_l, (STAT_ROWS, half))
            m_ref[h, :, half:] = jnp.broadcast_to(m_u, (STAT_ROWS, half))
            l_ref[h, :, :half] = jnp.broadcast_to(l_l, (STAT_ROWS, half))
            l_ref[h, :, half:] = jnp.broadcast_to(l_u, (STAT_ROWS, half))
            probs.append((jnp.concatenate([p_ll, p_lu], axis=1).astype(BF16), p_uu.astype(BF16)))
        for h in range(N_HEADS):
            v_t = v_ref[0, h, qi]
            acc_l = _dot(v_t[:, :half], probs[h][0])
            acc_u = _dot(v_t[:, half:], probs[h][1])
            acc_ref[h, :, :half] = acc_l[:, :half]
            acc_ref[h, :, half:] = acc_l[:, half:] + acc_u

    def update_single_pass(blocks):
        s_all = [[scores(h, c) for c in blocks] for h in range(N_HEADS)]
        betas, probs = [], []
        for h in range(N_HEADS):
            m_old = m_ref[h, 0:1, :]
            ps = [jnp.exp2(s - m_old) for s in s_all[h]]
            m_blk = jnp.max(s_all[h][0], axis=0, keepdims=True)
            for s in s_all[h][1:]:
                m_blk = jnp.maximum(m_blk, jnp.max(s, axis=0, keepdims=True))
            m_new = jnp.maximum(m_old, m_blk)
            beta = jnp.exp2(m_old - m_new)
            l_new = l_ref[h, 0:1, :]
            for p in ps:
                l_new = l_new + jnp.sum(p, axis=0, keepdims=True)
            m_ref[h] = jnp.broadcast_to(m_new, (STAT_ROWS, tq))
            l_ref[h] = jnp.broadcast_to(l_new * beta, (STAT_ROWS, tq))
            gap_ref[h] = jnp.maximum(gap_ref[h], jnp.broadcast_to(m_blk - m_old, (STAT_ROWS, tq)))
            betas.append(beta)
            probs.append([p.astype(BF16) for p in ps])
        for h in range(N_HEADS):
            acc = acc_ref[h]
            for c, p in zip(blocks, probs[h]):
                acc = acc + _dot(v_ref[0, h, c], p)
            acc_ref[h] = acc * betas[h]

    def update_two_pass(c):
        s_all = [scores(h, c) for h in range(N_HEADS)]
        alphas, probs = [], []
        for h in range(N_HEADS):
            m_old = m_ref[h, 0:1, :]
            m_new = jnp.maximum(m_old, jnp.max(s_all[h], axis=0, keepdims=True))
            alpha = jnp.exp2(m_old - m_new)
            p = jnp.exp2(s_all[h] - m_new)
            l_new = alpha * l_ref[h, 0:1, :] + jnp.sum(p, axis=0, keepdims=True)
            m_ref[h] = jnp.broadcast_to(m_new, (STAT_ROWS, tq))
            l_ref[h] = jnp.broadcast_to(l_new, (STAT_ROWS, tq))
            alphas.append(alpha)
            probs.append(p.astype(BF16))
        for h in range(N_HEADS):
            acc_ref[h] = alphas[h] * acc_ref[h] + _dot(v_ref[0, h, c], probs[h])

    init_with_diagonal()
    gap_ref[...] = jnp.zeros(gap_ref.shape, F32)

    def pair(j, _):
        update_single_pass([2 * j, 2 * j + 1])
        return 0

    lax.fori_loop(0, lax.shift_right_logical(qi, 1), pair, 0)

    @pl.when(qi % 2 == 1)
    def _():
        update_single_pass([qi - 1])

    worst = gap_ref[0]
    for h in range(1, N_HEADS):
        worst = jnp.maximum(worst, gap_ref[h])

    @pl.when(jnp.max(worst) > MAX_SINGLE_PASS_JUMP)
    def _():
        init_with_diagonal()

        def one(c, _):
            update_two_pass(c)
            return 0

        lax.fori_loop(0, qi, one, 0)

    for h in range(N_HEADS):
        o_ref[0, :, h * HEAD_DIM:(h + 1) * HEAD_DIM] = (acc_ref[h] / l_ref[h, 0:1, :]).T.astype(BF16)


def _mla_attention(q, k, v):
    batch, _, seq, _ = k.shape
    assert Q_TILE == KV_TILE
    return pl.pallas_call(
        _attn_body,
        grid=(batch, seq // Q_TILE),
        in_specs=[pl.BlockSpec((1, N_HEADS, QK_DIM, Q_TILE), lambda b, i: (b, 0, 0, i)),
                  pl.BlockSpec((1, N_HEADS, seq, QK_DIM), lambda b, i: (b, 0, 0, 0)),
                  pl.BlockSpec((1, N_HEADS, seq // KV_TILE, HEAD_DIM, KV_TILE), lambda b, i: (b, 0, 0, 0, 0))],
        out_specs=pl.BlockSpec((1, Q_TILE, ATT_WIDTH), lambda b, i: (b, i, 0)),
        out_shape=jax.ShapeDtypeStruct((batch, seq, ATT_WIDTH), BF16),
        scratch_shapes=[pltpu.VMEM((N_HEADS, STAT_ROWS, Q_TILE), F32),
                        pltpu.VMEM((N_HEADS, STAT_ROWS, Q_TILE), F32),
                        pltpu.VMEM((N_HEADS, HEAD_DIM, Q_TILE), F32),
                        pltpu.VMEM((N_HEADS, STAT_ROWS, Q_TILE), F32)],
        compiler_params=pltpu.CompilerParams(dimension_semantics=("arbitrary", "arbitrary"),
                                             vmem_limit_bytes=VMEM_LIMIT_BYTES),
        name="mla_attn",
    )(q, k, v)


def _post_body(h_ref, a_ref, mkv_ref, xn_ref, n2_ref, fn_ref, wout_hbm, wmq_hbm, wmo_hbm, wg_hbm, wu_hbm, wd_hbm,
               out_ref,
               wouta_ref, wmq_ref, wmo_ref, wg_ref, wu_ref, wd_ref, stage_wide, stage, sem_wide, sem):
    @pl.when(pl.program_id(0) == 0)
    def _():
        _cast_weights_into_vmem(_row_jobs(wout_hbm, 0, ATT_WIDTH, wouta_ref, CAST_ROWS)
                                + _row_jobs(wmq_hbm, 0, D_MODEL, wmq_ref, CAST_ROWS)
                                + _row_jobs(wmo_hbm, 0, D_MODEL, wmo_ref, CAST_ROWS)
                                + _row_jobs(wd_hbm, 0, D_FF, wd_ref, CAST_ROWS), stage, sem)
        _cast_weights_into_vmem(_row_jobs(wg_hbm, 0, D_MODEL, wg_ref, CAST_ROWS_WIDE)
                                + _row_jobs(wu_hbm, 0, D_MODEL, wu_ref, CAST_ROWS_WIDE), stage_wide, sem_wide)

    h2 = h_ref[...] + _dot(a_ref[...], wouta_ref[...])

    hn = _rmsnorm(h2, xn_ref[...]).astype(BF16)
    q = (_dot(hn, wmq_ref[...]) * MEM_SCALE).astype(BF16)
    cols = [slice(h * MEM_HEAD_DIM, (h + 1) * MEM_HEAD_DIM) for h in range(MEM_HEADS)]
    scores = [_dot_nt(q[:, c], mkv_ref[0, :, c]) for c in cols]
    probs = []
    for s in scores:
        e = jnp.exp(s - jnp.max(s, axis=-1, keepdims=True))
        probs.append((e / jnp.sum(e, axis=-1, keepdims=True)).astype(BF16))
    heads = [_dot(p, mkv_ref[0, :, D_MODEL + h * MEM_HEAD_DIM:D_MODEL + (h + 1) * MEM_HEAD_DIM]).astype(BF16)
             for h, p in enumerate(probs)]
    h3 = h2 + _dot(jnp.concatenate(heads, axis=1), wmo_ref[...])

    xn = _rmsnorm(h3, n2_ref[...]).astype(BF16)
    h4 = h3 + 0.5 * _swiglu(xn, wg_ref, wu_ref, wd_ref)
    out_ref[...] = _rmsnorm(h4, fn_ref[...])


def _post(h2d, a2d, mkv, xn, n2, fn, wout, wmq, wmo, wg, wu, wd, seq):
    tm = TOKEN_TILE
    n_tok = h2d.shape[0]
    tiles_per_seq = seq // tm
    norms = (xn, n2, fn)
    hbm_weights = (wout, wmq, wmo, wg, wu, wd)
    return pl.pallas_call(
        _post_body,
        grid=(n_tok // tm,),
        in_specs=[pl.BlockSpec((tm, D_MODEL), lambda i: (i, 0)),
                  pl.BlockSpec((tm, ATT_WIDTH), lambda i: (i, 0)),
                  pl.BlockSpec((1, N_MEM, 2 * D_MODEL), lambda i: (i // tiles_per_seq, 0, 0))]
                 + [_const_spec(w.shape) for w in norms]
                 + [pl.BlockSpec(memory_space=pl.ANY)] * len(hbm_weights),
        out_specs=pl.BlockSpec((tm, D_MODEL), lambda i: (i, 0)),
        out_shape=jax.ShapeDtypeStruct((n_tok, D_MODEL), F32),
        scratch_shapes=[pltpu.VMEM((ATT_WIDTH, D_MODEL), BF16), pltpu.VMEM((D_MODEL, D_MODEL), BF16),
                        pltpu.VMEM((D_MODEL, D_MODEL), BF16), pltpu.VMEM((D_MODEL, D_FF), BF16),
                        pltpu.VMEM((D_MODEL, D_FF), BF16), pltpu.VMEM((D_FF, D_MODEL), BF16)]
                       + _cast_scratch(CAST_ROWS_WIDE, CAST_ROWS),
        compiler_params=pltpu.CompilerParams(dimension_semantics=("arbitrary",),
                                             vmem_limit_bytes=VMEM_LIMIT_BYTES),
        name="post",
    )(h2d, a2d, mkv, *norms, *hbm_weights)


def _pad_rope_cols(w):
    zeros = jnp.zeros(w.shape[:-1] + (ROPE_HALF,), w.dtype)
    return jnp.concatenate([w[..., :ROPE_HALF], zeros, w[..., ROPE_HALF:], zeros], axis=-1)


def _block_diag2(a, b):
    za = jnp.zeros((a.shape[0], b.shape[1]), a.dtype)
    zb = jnp.zeros((b.shape[0], a.shape[1]), a.dtype)
    return jnp.concatenate([jnp.concatenate([a, za], axis=1), jnp.concatenate([zb, b], axis=1)], axis=0)


def _layer(h2d, mem, cos, sin, batch, seq, ffn1_norm, ffn1_w_gate, ffn1_w_up, ffn1_w_down, mix_norm, w_in,
           q_norm, w_q_up, kv_norm, w_kv_up, pool_w, pool_scale, w_out, xattn_norm, mem_norm,
           w_mq, w_mkv, w_mo, ffn2_norm, ffn2_w_gate, ffn2_w_up, ffn2_w_down, out_norm):
    row = lambda g: g.reshape(1, -1).astype(F32)
    rope_lo = Q_RANK + KV_RANK
    w_in_p = jnp.concatenate([w_in[:, :rope_lo], _pad_rope_cols(w_in[:, rope_lo:rope_lo + ROPE_DIM]),
                              w_in[:, rope_lo + ROPE_DIM:]], axis=1).astype(BF16)
    wq = w_q_up.reshape(Q_RANK, N_HEADS, HEAD_DIM + ROPE_DIM)
    wq_p = jnp.concatenate([wq[..., :HEAD_DIM], _pad_rope_cols(wq[..., HEAD_DIM:])], axis=-1)
    wq_t = wq_p.reshape(Q_RANK, N_HEADS * QK_DIM).T.astype(BF16)
    wkv = w_kv_up.reshape(KV_RANK, N_HEADS, 2 * HEAD_DIM)
    wk = wkv[..., :HEAD_DIM].reshape(KV_RANK, ATT_WIDTH).astype(BF16)
    wv_t = wkv[..., HEAD_DIM:].reshape(KV_RANK, ATT_WIDTH).T.astype(BF16)
    poolw = jnp.stack([_block_diag2(pool_w[0], pool_w[1]), _block_diag2(pool_w[2], pool_w[3])]).astype(BF16)

    mkv = _mem_kv(mem, row(mem_norm), w_mkv.astype(BF16))
    h1p, q, k, v = _pre(h2d, cos, sin, row(ffn1_norm), row(mix_norm), w_in_p, row(q_norm), wq_t, row(kv_norm),
                        wk, wv_t, poolw, row(pool_scale), ffn1_w_gate, ffn1_w_up, ffn1_w_down, w_out, batch, seq)
    a = _mla_attention(q, k, v)
    return _post(h1p, a.reshape(batch * seq, ATT_WIDTH), mkv, row(xattn_norm), row(ffn2_norm), row(out_norm),
                 w_out, w_mq, w_mo, ffn2_w_gate, ffn2_w_up, ffn2_w_down, seq)


def kernel(x, mem, positions, ffn1_norm, ffn1_w_gate, ffn1_w_up, ffn1_w_down, mix_norm, w_in, q_norm, w_q_up,
           kv_norm, w_kv_up, pool_w, pool_scale, w_out, xattn_norm, mem_norm, w_mq, w_mkv, w_mo, ffn2_norm,
           ffn2_w_gate, ffn2_w_up, ffn2_w_down, final_norm):
    batch, seq, d_model = x.shape
    depth = ffn1_norm.shape[0]
    assert d_model == D_MODEL and depth == 1 and seq % TOKEN_TILE == 0 and seq % Q_TILE == 0
    cos, sin = _rope_tables(positions)
    layer = (ffn1_norm, ffn1_w_gate, ffn1_w_up, ffn1_w_down, mix_norm, w_in, q_norm, w_q_up, kv_norm, w_kv_up,
             pool_w, pool_scale, w_out, xattn_norm, mem_norm, w_mq, w_mkv, w_mo, ffn2_norm, ffn2_w_gate,
             ffn2_w_up, ffn2_w_down)
    out = _layer(x.reshape(batch * seq, d_model), mem, cos, sin, batch, seq, *[w[0] for w in layer], final_norm)
    return out.reshape(batch, seq, d_model)
```

```python
import functools

import jax
import jax.numpy as jnp
from jax import lax
from jax.experimental import pallas as pl
from jax.experimental.pallas import tpu as pltpu

F32 = jnp.float32
BF16 = jnp.bfloat16

D_MODEL = 1024
N_HEADS = 4
HEAD_DIM = 128
ROPE_DIM = 64
ROPE_HALF = ROPE_DIM // 2
Q_RANK = 256
KV_RANK = 128
ATT_WIDTH = N_HEADS * HEAD_DIM
POOL_WIDTH = D_MODEL - ATT_WIDTH
POOL_WINDOWS = (2, 4, 8, 16)
POOL_CH = POOL_WIDTH // len(POOL_WINDOWS)
POOL_HALO = 16
D_FF = 2816
N_MEM = 256
MEM_HEADS = 4
MEM_HEAD_DIM = D_MODEL // MEM_HEADS
ROPE_BASE = 10000.0
RMS_EPS = 1e-6
ATT_SCALE = (HEAD_DIM + ROPE_DIM) ** -0.5
LOG2_E = 1.4426950408889634
Q_PRESCALE = ATT_SCALE * LOG2_E
MEM_SCALE = MEM_HEAD_DIM ** -0.5

LANES = 128
BF16_SUBLANES = 16
MXU_DIM = 256
VMEM_LIMIT_BYTES = 56 * 1024 * 1024

TOKEN_TILE = 512
FF_CHUNK = MXU_DIM
Q_TILE = 512
KV_TILE = 512
QK_DIM = 2 * HEAD_DIM
CAST_ROWS_WIDE = 128
CAST_ROWS = 256
CAST_SLOTS = 4
STAT_ROWS = 8
MAX_SINGLE_PASS_JUMP = 32.0


def _rmsnorm(x, g):
    ms = jnp.mean(x * x, axis=-1, keepdims=True)
    return (x * lax.rsqrt(ms + RMS_EPS)) * g


def _dot(a, b):
    return jnp.dot(a, b, preferred_element_type=F32)


def _dot_nt(a, b):
    return lax.dot_general(a, b, (((1,), (1,)), ((), ())), preferred_element_type=F32)


def _swiglu(xn, wg_ref, wu_ref, wd_ref):
    n_chunks = D_FF // FF_CHUNK
    cols = lambda c: slice(c * FF_CHUNK, (c + 1) * FF_CHUNK)
    gate_up = lambda c: (_dot(xn, wg_ref[:, cols(c)]), _dot(xn, wu_ref[:, cols(c)]))
    acc = None
    g, u = gate_up(0)
    for c in range(n_chunks):
        nxt = gate_up(c + 1) if c + 1 < n_chunks else None
        a = (g / (1.0 + jnp.exp(-g))) * u
        d = _dot(a.astype(BF16), wd_ref[cols(c), :])
        acc = d if acc is None else acc + d
        if nxt is not None:
            g, u = nxt
    return acc


def _row_jobs(src, src_row0, n_rows, dst, chunk):
    assert n_rows % chunk == 0
    return [(src, src_row0 + r, dst, r) for r in range(0, n_rows, chunk)]


def _cast_weights_into_vmem(jobs, stage, sems):
    slots, rows = stage.shape[0], stage.shape[1]
    ahead = slots - 1

    def copy(j):
        src, r0, _, _ = jobs[j]
        return pltpu.make_async_copy(src.at[pl.ds(r0, rows), :], stage.at[j % slots], sems.at[j % slots])

    for j in range(min(ahead, len(jobs))):
        copy(j).start()
    for j in range(len(jobs)):
        if j + ahead < len(jobs):
            copy(j + ahead).start()
        copy(j).wait()
        _, _, dst, d0 = jobs[j]
        dst[d0:d0 + rows, :] = stage[j % slots].astype(BF16)


def _cast_scratch(wide_rows, narrow_rows):
    return [pltpu.VMEM((CAST_SLOTS, wide_rows, D_FF), F32), pltpu.VMEM((CAST_SLOTS, narrow_rows, D_MODEL), F32),
            pltpu.SemaphoreType.DMA((CAST_SLOTS,)), pltpu.SemaphoreType.DMA((CAST_SLOTS,))]


def _rope(x, cos4, sin4):
    return x * cos4 + pltpu.roll(x, 2 * ROPE_HALF, 1) * sin4


def _rope_table_body(pos_ref, freq_ref, cos_ref, sin_ref):
    ang = freq_ref[...] * pos_ref[...].astype(F32)
    cos_ref[...] = jnp.cos(ang)
    sin_ref[...] = jnp.sin(ang)


def _rope_tables(positions):
    n_tok = positions.size
    freqs = 1.0 / (ROPE_BASE ** (jnp.arange(0, ROPE_DIM, 2, dtype=F32) / ROPE_DIM))
    blk = 2048
    return pl.pallas_call(
        _rope_table_body,
        grid=(n_tok // blk,),
        in_specs=[pl.BlockSpec((1, blk), lambda i: (0, i)),
                  pl.BlockSpec((ROPE_HALF, 1), lambda i: (0, 0))],
        out_specs=[pl.BlockSpec((ROPE_HALF, blk), lambda i: (0, i))] * 2,
        out_shape=[jax.ShapeDtypeStruct((ROPE_HALF, n_tok), F32)] * 2,
        name="rope_tables",
    )(positions.reshape(1, n_tok), freqs.reshape(ROPE_HALF, 1))


def _mem_kv_body(mem_ref, g_ref, w_ref, out_ref):
    mn = _rmsnorm(mem_ref[0], g_ref[...]).astype(BF16)
    out_ref[0] = _dot(mn, w_ref[...]).astype(BF16)


def _mem_kv(mem, mem_norm, w_mkv):
    batch = mem.shape[0]
    return pl.pallas_call(
        _mem_kv_body,
        grid=(batch,),
        in_specs=[pl.BlockSpec((1, N_MEM, D_MODEL), lambda b: (b, 0, 0)),
                  pl.BlockSpec((1, D_MODEL), lambda b: (0, 0)),
                  pl.BlockSpec((D_MODEL, 2 * D_MODEL), lambda b: (0, 0))],
        out_specs=pl.BlockSpec((1, N_MEM, 2 * D_MODEL), lambda b: (b, 0, 0)),
        out_shape=jax.ShapeDtypeStruct((batch, N_MEM, 2 * D_MODEL), BF16),
        name="mem_kv",
    )(mem, mem_norm, w_mkv)


def _pre_body(tiles_per_seq,
              x_ref, cos_ref, sin_ref, n1_ref, mixn_ref, win_ref, qn_ref, wq_ref, kvn_ref, wk_ref, wv_ref,
              poolw_ref, pscale_ref, wg_hbm, wu_hbm, wd_hbm, wout_hbm,
              h_ref, q_ref, k_ref, v_ref,
              ext_ref, wg_ref, wu_ref, wd_ref, woutp_ref, stage_wide, stage, sem_wide, sem):
    seq_tile = pl.program_id(0) % tiles_per_seq
    tm = x_ref.shape[0]

    @pl.when(pl.program_id(0) == 0)
    def _():
        _cast_weights_into_vmem(_row_jobs(wg_hbm, 0, D_MODEL, wg_ref, CAST_ROWS_WIDE)
                                + _row_jobs(wu_hbm, 0, D_MODEL, wu_ref, CAST_ROWS_WIDE), stage_wide, sem_wide)
        _cast_weights_into_vmem(_row_jobs(wd_hbm, 0, D_FF, wd_ref, CAST_ROWS)
                                + _row_jobs(wout_hbm, ATT_WIDTH, POOL_WIDTH, woutp_ref, CAST_ROWS), stage, sem)

    x = x_ref[...]
    xn = _rmsnorm(x, n1_ref[...]).astype(BF16)
    h1 = x + 0.5 * _swiglu(xn, wg_ref, wu_ref, wd_ref)

    un = _rmsnorm(h1, mixn_ref[...]).astype(BF16)
    z = _dot(un, win_ref[...])

    cos = cos_ref[...]
    sin = sin_ref[...]
    cos4_t = jnp.concatenate([cos, cos, cos, cos], axis=0)
    sin4_t = jnp.concatenate([-sin, -sin, sin, sin], axis=0)

    qn = _rmsnorm(z[:, :Q_RANK], qn_ref[...]).astype(BF16)
    q_t = _dot_nt(wq_ref[...], qn)
    for h in range(N_HEADS):
        base = h * QK_DIM
        q_ref[0, h, :HEAD_DIM, :] = (q_t[base:base + HEAD_DIM] * Q_PRESCALE).astype(BF16)
        pe = q_t[base + HEAD_DIM:base + QK_DIM]
        pe = pe * cos4_t + pltpu.roll(pe, 2 * ROPE_HALF, 0) * sin4_t
        q_ref[0, h, HEAD_DIM:, :] = (pe * Q_PRESCALE).astype(BF16)

    kvn = _rmsnorm(z[:, Q_RANK:Q_RANK + KV_RANK], kvn_ref[...]).astype(BF16)
    k_nope = _dot(kvn, wk_ref[...])
    v_t = _dot_nt(wv_ref[...], kvn)
    k_pe = _rope(z[:, Q_RANK + KV_RANK:Q_RANK + KV_RANK + LANES], cos4_t.T, sin4_t.T).astype(BF16)
    for h in range(N_HEADS):
        k_ref[0, h, :, :HEAD_DIM] = k_nope[:, h * HEAD_DIM:(h + 1) * HEAD_DIM].astype(BF16)
        k_ref[0, h, :, HEAD_DIM:] = k_pe
        v_ref[0, h, 0] = v_t[h * HEAD_DIM:(h + 1) * HEAD_DIM].astype(BF16)

    zp = z[:, D_MODEL - POOL_WIDTH:]

    @pl.when(seq_tile == 0)
    def _():
        ext_ref[0:POOL_HALO, :] = jnp.zeros((POOL_HALO, POOL_WIDTH), F32)

    ext_ref[POOL_HALO:, :] = zp
    t1 = (seq_tile * tm + 1 + lax.broadcasted_iota(jnp.int32, (tm, 1), 0)).astype(F32)
    diffs = []
    for g, w in enumerate(POOL_WINDOWS):
        cols = slice(g * POOL_CH, (g + 1) * POOL_CH)
        tot = zp[:, cols]
        for back in range(1, w):
            tot = tot + ext_ref[POOL_HALO - back:POOL_HALO - back + tm, cols]
        mean = tot / jnp.minimum(t1, float(w))
        diffs.append(mean - zp[:, cols])
    ext_ref[0:POOL_HALO, :] = ext_ref[tm:tm + POOL_HALO, :]

    y01 = _dot(jnp.concatenate(diffs[:2], axis=1).astype(BF16), poolw_ref[0])
    y23 = _dot(jnp.concatenate(diffs[2:], axis=1).astype(BF16), poolw_ref[1])
    p = jnp.concatenate([y01, y23], axis=1) * pscale_ref[...]
    h_ref[...] = h1 + _dot(p.astype(BF16), woutp_ref[...])


def _const_spec(shape):
    nd = len(shape)
    return pl.BlockSpec(shape, lambda *_: (0,) * nd, pipeline_mode=pl.Buffered(1))


def _pre(x2d, cos, sin, n1, mixn, win, qn, wq, kvn, wk, wv, poolw, pscale, wg, wu, wd, wout, batch, seq):
    tm = TOKEN_TILE
    assert tm == KV_TILE
    n_tok = x2d.shape[0]
    tiles_per_seq = seq // tm
    tok_spec = lambda width: pl.BlockSpec((tm, width), lambda i: (i, 0))
    rope_spec = pl.BlockSpec((ROPE_HALF, tm), lambda i: (0, i))
    bt = lambda i: (i // tiles_per_seq, i % tiles_per_seq)
    q_spec = pl.BlockSpec((1, N_HEADS, QK_DIM, tm), lambda i: (bt(i)[0], 0, 0, bt(i)[1]))
    k_spec = pl.BlockSpec((1, N_HEADS, tm, QK_DIM), lambda i: (bt(i)[0], 0, bt(i)[1], 0))
    v_spec = pl.BlockSpec((1, N_HEADS, 1, HEAD_DIM, tm), lambda i: (bt(i)[0], 0, bt(i)[1], 0, 0))
    weights = (n1, mixn, win, qn, wq, kvn, wk, wv, poolw, pscale)
    hbm_weights = (wg, wu, wd, wout)
    return pl.pallas_call(
        functools.partial(_pre_body, tiles_per_seq),
        grid=(n_tok // tm,),
        in_specs=[tok_spec(D_MODEL), rope_spec, rope_spec] + [_const_spec(w.shape) for w in weights]
                 + [pl.BlockSpec(memory_space=pl.ANY)] * len(hbm_weights),
        out_specs=[tok_spec(D_MODEL), q_spec, k_spec, v_spec],
        out_shape=[jax.ShapeDtypeStruct((n_tok, D_MODEL), F32),
                   jax.ShapeDtypeStruct((batch, N_HEADS, QK_DIM, seq), BF16),
                   jax.ShapeDtypeStruct((batch, N_HEADS, seq, QK_DIM), BF16),
                   jax.ShapeDtypeStruct((batch, N_HEADS, tiles_per_seq, HEAD_DIM, tm), BF16)],
        scratch_shapes=[pltpu.VMEM((tm + POOL_HALO, POOL_WIDTH), F32),
                        pltpu.VMEM((D_MODEL, D_FF), BF16), pltpu.VMEM((D_MODEL, D_FF), BF16),
                        pltpu.VMEM((D_FF, D_MODEL), BF16), pltpu.VMEM((POOL_WIDTH, D_MODEL), BF16)]
                       + _cast_scratch(CAST_ROWS_WIDE, CAST_ROWS),
        compiler_params=pltpu.CompilerParams(dimension_semantics=("arbitrary",),
                                             vmem_limit_bytes=VMEM_LIMIT_BYTES),
        name="pre",
    )(x2d, cos, sin, *weights, *hbm_weights)


def _attn_body(n_cast, q_ref, k_ref, v_ref, *refs):
    cast_in, o_ref, cast_out = refs[:n_cast], refs[n_cast], refs[n_cast + 1:2 * n_cast + 1]
    m_ref, l_ref, acc_ref, gap_ref = refs[2 * n_cast + 1:]
    for src, dst in zip(cast_in, cast_out):
        dst[...] = src[...].astype(BF16)

    qi = pl.program_id(1)
    tq = q_ref.shape[3]
    tk = KV_TILE
    half = tk // 2

    def scores(h, c):
        return _dot(k_ref[0, h, pl.ds(pl.multiple_of(c * tk, tk), tk), :], q_ref[0, h])


    def init_with_diagonal():
        causal = (lax.broadcasted_iota(jnp.int32, (half, half), 0)
                  <= lax.broadcasted_iota(jnp.int32, (half, half), 1))
        start = pl.multiple_of(qi * tk, tk)
        s_lo = [_dot(k_ref[0, h, pl.ds(start, half), :], q_ref[0, h]) for h in range(N_HEADS)]
        s_hi = [_dot(k_ref[0, h, pl.ds(start + half, half), :], q_ref[0, h, :, half:]) for h in range(N_HEADS)]
        probs = []
        for h in range(N_HEADS):
            s_ll = jnp.where(causal, s_lo[h][:, :half], -jnp.inf)
            s_lu = s_lo[h][:, half:]
            s_uu = jnp.where(causal, s_hi[h], -jnp.inf)
            m_l = jnp.max(s_ll, axis=0, keepdims=True)
            m_u = jnp.maximum(jnp.max(s_lu, axis=0, keepdims=True), jnp.max(s_uu, axis=0, keepdims=True))
            p_ll = jnp.exp2(s_ll - m_l)
            p_lu = jnp.exp2(s_lu - m_u)
            p_uu = jnp.exp2(s_uu - m_u)
            l_l = jnp.sum(p_ll, axis=0, keepdims=True)
            l_u = jnp.sum(p_lu, axis=0, keepdims=True) + jnp.sum(p_uu, axis=0, keepdims=True)
            m_ref[h, :, :half] = jnp.broadcast_to(m_l, (STAT_ROWS, half))
            m_ref[h, :, half:] = jnp.broadcast_to(m_u, (STAT_ROWS, half))
            l_ref[h, :, :half] = jnp.broadcast_to(l_l, (STAT_ROWS, half))
            l_ref[h, :, half:] = jnp.broadcast_to(l_u, (STAT_ROWS, half))
            probs.append((jnp.concatenate([p_ll, p_lu], axis=1).astype(BF16), p_uu.astype(BF16)))
        for h in range(N_HEADS):
            v_t = v_ref[0, h, qi]
            acc_l = _dot(v_t[:, :half], probs[h][0])
            acc_u = _dot(v_t[:, half:], probs[h][1])
            acc_ref[h, :, :half] = acc_l[:, :half]
            acc_ref[h, :, half:] = acc_l[:, half:] + acc_u

    def update_single_pass(blocks):
        s_all = [[scores(h, c) for c in blocks] for h in range(N_HEADS)]
        betas, probs = [], []
        for h in range(N_HEADS):
            m_old = m_ref[h, 0:1, :]
            ps = [jnp.exp2(s - m_old) for s in s_all[h]]
            m_blk = jnp.max(s_all[h][0], axis=0, keepdims=True)
            for s in s_all[h][1:]:
                m_blk = jnp.maximum(m_blk, jnp.max(s, axis=0, keepdims=True))
            m_new = jnp.maximum(m_old, m_blk)
            beta = jnp.exp2(m_old - m_new)
            l_new = l_ref[h, 0:1, :]
            for p in ps:
                l_new = l_new + jnp.sum(p, axis=0, keepdims=True)
            m_ref[h] = jnp.broadcast_to(m_new, (STAT_ROWS, tq))
            l_ref[h] = jnp.broadcast_to(l_new * beta, (STAT_ROWS, tq))
            gap_ref[h] = jnp.maximum(gap_ref[h], jnp.broadcast_to(m_blk - m_old, (STAT_ROWS, tq)))
            betas.append(beta)
            probs.append([p.astype(BF16) for p in ps])
        for h in range(N_HEADS):
            acc = acc_ref[h]
            for c, p in zip(blocks, probs[h]):
                acc = acc + _dot(v_ref[0, h, c], p)
            acc_ref[h] = acc * betas[h]

    def update_two_pass(c):
        s_all = [scores(h, c) for h in range(N_HEADS)]
        alphas, probs = [], []
        for h in range(N_HEADS):
            m_old = m_ref[h, 0:1, :]
            m_new = jnp.maximum(m_old, jnp.max(s_all[h], axis=0, keepdims=True))
            alpha = jnp.exp2(m_old - m_new)
            p = jnp.exp2(s_all[h] - m_new)
            l_new = alpha * l_ref[h, 0:1, :] + jnp.sum(p, axis=0, keepdims=True)
            m_ref[h] = jnp.broadcast_to(m_new, (STAT_ROWS, tq))
            l_ref[h] = jnp.broadcast_to(l_new, (STAT_ROWS, tq))
            alphas.append(alpha)
            probs.append(p.astype(BF16))
        for h in range(N_HEADS):
            acc_ref[h] = alphas[h] * acc_ref[h] + _dot(v_ref[0, h, c], probs[h])

    init_with_diagonal()
    gap_ref[...] = jnp.zeros(gap_ref.shape, F32)

    def pair(j, _):
        update_single_pass([2 * j, 2 * j + 1])
        return 0

    lax.fori_loop(0, lax.shift_right_logical(qi, 1), pair, 0)

    @pl.when(qi % 2 == 1)
    def _():
        update_single_pass([qi - 1])

    worst = gap_ref[0]
    for h in range(1, N_HEADS):
        worst = jnp.maximum(worst, gap_ref[h])

    @pl.when(jnp.max(worst) > MAX_SINGLE_PASS_JUMP)
    def _():
        init_with_diagonal()

        def one(c, _):
            update_two_pass(c)
            return 0

        lax.fori_loop(0, qi, one, 0)

    for h in range(N_HEADS):
        o_ref[0, :, h * HEAD_DIM:(h + 1) * HEAD_DIM] = (acc_ref[h] / l_ref[h, 0:1, :]).T.astype(BF16)


def _cast_block_specs(w, n_rows, steps, n_q):
    cols = w.shape[1]
    per_step = n_rows // steps
    blk = per_step if n_rows % steps == 0 and per_step % BF16_SUBLANES == 0 else LANES
    n_blk = n_rows // blk
    assert n_rows % blk == 0 and n_blk <= steps
    index = lambda b, i: (jnp.minimum(b * n_q + i, n_blk - 1), 0)
    return (pl.BlockSpec((blk, cols), index), pl.BlockSpec((blk, cols), index),
            jax.ShapeDtypeStruct((n_rows, cols), BF16))


def _mla_attention(q, k, v, cast_jobs):
    batch, _, seq, _ = k.shape
    assert Q_TILE == KV_TILE
    n_q = seq // Q_TILE
    cast_specs = [_cast_block_specs(w, rows, batch * n_q, n_q) for w, rows in cast_jobs]
    outs = pl.pallas_call(
        functools.partial(_attn_body, len(cast_jobs)),
        grid=(batch, n_q),
        in_specs=[pl.BlockSpec((1, N_HEADS, QK_DIM, Q_TILE), lambda b, i: (b, 0, 0, i)),
                  pl.BlockSpec((1, N_HEADS, seq, QK_DIM), lambda b, i: (b, 0, 0, 0)),
                  pl.BlockSpec((1, N_HEADS, seq // KV_TILE, HEAD_DIM, KV_TILE), lambda b, i: (b, 0, 0, 0, 0))]
                 + [c[0] for c in cast_specs],
        out_specs=[pl.BlockSpec((1, Q_TILE, ATT_WIDTH), lambda b, i: (b, i, 0))] + [c[1] for c in cast_specs],
        out_shape=[jax.ShapeDtypeStruct((batch, seq, ATT_WIDTH), BF16)] + [c[2] for c in cast_specs],
        scratch_shapes=[pltpu.VMEM((N_HEADS, STAT_ROWS, Q_TILE), F32),
                        pltpu.VMEM((N_HEADS, STAT_ROWS, Q_TILE), F32),
                        pltpu.VMEM((N_HEADS, HEAD_DIM, Q_TILE), F32),
                        pltpu.VMEM((N_HEADS, STAT_ROWS, Q_TILE), F32)],
        compiler_params=pltpu.CompilerParams(dimension_semantics=("arbitrary", "arbitrary"),
                                             vmem_limit_bytes=VMEM_LIMIT_BYTES),
        name="mla_attn",
    )(q, k, v, *[w for w, _ in cast_jobs])
    return outs[0], outs[1:]


def _post_body(h_ref, a_ref, mkv_ref, xn_ref, n2_ref, fn_ref, wouta_ref, wmq_ref, wmo_ref, wg_ref, wu_ref, wd_ref,
               out_ref):
    h2 = h_ref[...] + _dot(a_ref[...], wouta_ref[...])

    hn = _rmsnorm(h2, xn_ref[...]).astype(BF16)
    q = (_dot(hn, wmq_ref[...]) * MEM_SCALE).astype(BF16)
    cols = [slice(h * MEM_HEAD_DIM, (h + 1) * MEM_HEAD_DIM) for h in range(MEM_HEADS)]
    scores = [_dot_nt(q[:, c], mkv_ref[0, :, c]) for c in cols]
    probs = []
    for s in scores:
        e = jnp.exp(s - jnp.max(s, axis=-1, keepdims=True))
        probs.append((e / jnp.sum(e, axis=-1, keepdims=True)).astype(BF16))
    heads = [_dot(p, mkv_ref[0, :, D_MODEL + h * MEM_HEAD_DIM:D_MODEL + (h + 1) * MEM_HEAD_DIM]).astype(BF16)
             for h, p in enumerate(probs)]
    h3 = h2 + _dot(jnp.concatenate(heads, axis=1), wmo_ref[...])

    xn = _rmsnorm(h3, n2_ref[...]).astype(BF16)
    h4 = h3 + 0.5 * _swiglu(xn, wg_ref, wu_ref, wd_ref)
    out_ref[...] = _rmsnorm(h4, fn_ref[...])


def _post(h2d, a2d, mkv, xn, n2, fn, wouta, wmq, wmo, wg, wu, wd, seq):
    tm = TOKEN_TILE
    n_tok = h2d.shape[0]
    tiles_per_seq = seq // tm
    weights = (xn, n2, fn, wouta, wmq, wmo, wg, wu, wd)
    return pl.pallas_call(
        _post_body,
        grid=(n_tok // tm,),
        in_specs=[pl.BlockSpec((tm, D_MODEL), lambda i: (i, 0)),
                  pl.BlockSpec((tm, ATT_WIDTH), lambda i: (i, 0)),
                  pl.BlockSpec((1, N_MEM, 2 * D_MODEL), lambda i: (i // tiles_per_seq, 0, 0))]
                 + [_const_spec(w.shape) for w in weights],
        out_specs=pl.BlockSpec((tm, D_MODEL), lambda i: (i, 0)),
        out_shape=jax.ShapeDtypeStruct((n_tok, D_MODEL), F32),
        compiler_params=pltpu.CompilerParams(dimension_semantics=("arbitrary",),
                                             vmem_limit_bytes=VMEM_LIMIT_BYTES),
        name="post",
    )(h2d, a2d, mkv, *weights)


def _pad_rope_cols(w):
    zeros = jnp.zeros(w.shape[:-1] + (ROPE_HALF,), w.dtype)
    return jnp.concatenate([w[..., :ROPE_HALF], zeros, w[..., ROPE_HALF:], zeros], axis=-1)


def _block_diag2(a, b):
    za = jnp.zeros((a.shape[0], b.shape[1]), a.dtype)
    zb = jnp.zeros((b.shape[0], a.shape[1]), a.dtype)
    return jnp.concatenate([jnp.concatenate([a, za], axis=1), jnp.concatenate([zb, b], axis=1)], axis=0)


def _layer(h2d, mem, cos, sin, batch, seq, ffn1_norm, ffn1_w_gate, ffn1_w_up, ffn1_w_down, mix_norm, w_in,
           q_norm, w_q_up, kv_norm, w_kv_up, pool_w, pool_scale, w_out, xattn_norm, mem_norm,
           w_mq, w_mkv, w_mo, ffn2_norm, ffn2_w_gate, ffn2_w_up, ffn2_w_down, out_norm):
    row = lambda g: g.reshape(1, -1).astype(F32)
    rope_lo = Q_RANK + KV_RANK
    w_in_p = jnp.concatenate([w_in[:, :rope_lo], _pad_rope_cols(w_in[:, rope_lo:rope_lo + ROPE_DIM]),
                              w_in[:, rope_lo + ROPE_DIM:]], axis=1).astype(BF16)
    wq = w_q_up.reshape(Q_RANK, N_HEADS, HEAD_DIM + ROPE_DIM)
    wq_p = jnp.concatenate([wq[..., :HEAD_DIM], _pad_rope_cols(wq[..., HEAD_DIM:])], axis=-1)
    wq_t = wq_p.reshape(Q_RANK, N_HEADS * QK_DIM).T.astype(BF16)
    wkv = w_kv_up.reshape(KV_RANK, N_HEADS, 2 * HEAD_DIM)
    wk = wkv[..., :HEAD_DIM].reshape(KV_RANK, ATT_WIDTH).astype(BF16)
    wv_t = wkv[..., HEAD_DIM:].reshape(KV_RANK, ATT_WIDTH).T.astype(BF16)
    poolw = jnp.stack([_block_diag2(pool_w[0], pool_w[1]), _block_diag2(pool_w[2], pool_w[3])]).astype(BF16)

    mkv = _mem_kv(mem, row(mem_norm), w_mkv.astype(BF16))
    h1p, q, k, v = _pre(h2d, cos, sin, row(ffn1_norm), row(mix_norm), w_in_p, row(q_norm), wq_t, row(kv_norm),
                        wk, wv_t, poolw, row(pool_scale), ffn1_w_gate, ffn1_w_up, ffn1_w_down, w_out, batch, seq)
    a, post_weights = _mla_attention(q, k, v, [(w_out, ATT_WIDTH), (w_mq, D_MODEL), (w_mo, D_MODEL),
                                               (ffn2_w_gate, D_MODEL), (ffn2_w_up, D_MODEL), (ffn2_w_down, D_FF)])
    return _post(h1p, a.reshape(batch * seq, ATT_WIDTH), mkv, row(xattn_norm), row(ffn2_norm), row(out_norm),
                 *post_weights, seq)


def kernel(x, mem, positions, ffn1_norm, ffn1_w_gate, ffn1_w_up, ffn1_w_down, mix_norm, w_in, q_norm, w_q_up,
           kv_norm, w_kv_up, pool_w, pool_scale, w_out, xattn_norm, mem_norm, w_mq, w_mkv, w_mo, ffn2_norm,
           ffn2_w_gate, ffn2_w_up, ffn2_w_down, final_norm):
    batch, seq, d_model = x.shape
    depth = ffn1_norm.shape[0]
    assert d_model == D_MODEL and depth == 1 and seq % TOKEN_TILE == 0 and seq % Q_TILE == 0
    cos, sin = _rope_tables(positions)
    layer = (ffn1_norm, ffn1_w_gate, ffn1_w_up, ffn1_w_down, mix_norm, w_in, q_norm, w_q_up, kv_norm, w_kv_up,
             pool_w, pool_scale, w_out, xattn_norm, mem_norm, w_mq, w_mkv, w_mo, ffn2_norm, ffn2_w_gate,
             ffn2_w_up, ffn2_w_down)
    out = _layer(x.reshape(batch * seq, d_model), mem, cos, sin, batch, seq, *[w[0] for w in layer], final_norm)
    return out.reshape(batch, seq, d_model)
```

```python
import functools

import jax
import jax.numpy as jnp
from jax import lax
from jax.experimental import pallas as pl
from jax.experimental.pallas import tpu as pltpu

F32 = jnp.float32
BF16 = jnp.bfloat16

D_MODEL = 1024
N_HEADS = 4
HEAD_DIM = 128
ROPE_DIM = 64
ROPE_HALF = ROPE_DIM // 2
Q_RANK = 256
KV_RANK = 128
ATT_WIDTH = N_HEADS * HEAD_DIM
POOL_WIDTH = D_MODEL - ATT_WIDTH
POOL_WINDOWS = (2, 4, 8, 16)
POOL_CH = POOL_WIDTH // len(POOL_WINDOWS)
POOL_HALO = 16
D_FF = 2816
N_MEM = 256
MEM_HEADS = 4
MEM_HEAD_DIM = D_MODEL // MEM_HEADS
ROPE_BASE = 10000.0
RMS_EPS = 1e-6
ATT_SCALE = (HEAD_DIM + ROPE_DIM) ** -0.5
LOG2_E = 1.4426950408889634
Q_PRESCALE = ATT_SCALE * LOG2_E
MEM_SCALE = MEM_HEAD_DIM ** -0.5

LANES = 128
BF16_SUBLANES = 16
MXU_DIM = 256
VMEM_LIMIT_BYTES = 56 * 1024 * 1024

TOKEN_TILE = 512
POST_TILE = 1024
FF_CHUNK = MXU_DIM
Q_TILE = 512
KV_TILE = 512
QK_DIM = 2 * HEAD_DIM
CAST_ROWS_WIDE = 128
CAST_ROWS = 256
CAST_SLOTS = 4
STAT_ROWS = 8
MAX_SINGLE_PASS_JUMP = 32.0


def _rmsnorm(x, g):
    ms = jnp.mean(x * x, axis=-1, keepdims=True)
    return (x * lax.rsqrt(ms + RMS_EPS)) * g


def _dot(a, b):
    return jnp.dot(a, b, preferred_element_type=F32)


def _dot_nt(a, b):
    return lax.dot_general(a, b, (((1,), (1,)), ((), ())), preferred_element_type=F32)


def _swiglu(xn, wg_ref, wu_ref, wd_ref):
    n_chunks = D_FF // FF_CHUNK
    cols = lambda c: slice(c * FF_CHUNK, (c + 1) * FF_CHUNK)
    gate_up = lambda c: (_dot(xn, wg_ref[:, cols(c)]), _dot(xn, wu_ref[:, cols(c)]))
    acc = None
    g, u = gate_up(0)
    for c in range(n_chunks):
        nxt = gate_up(c + 1) if c + 1 < n_chunks else None
        a = (g / (1.0 + jnp.exp(-g))) * u
        d = _dot(a.astype(BF16), wd_ref[cols(c), :])
        acc = d if acc is None else acc + d
        if nxt is not None:
            g, u = nxt
    return acc


def _row_jobs(src, src_row0, n_rows, dst, chunk):
    assert n_rows % chunk == 0
    return [(src, src_row0 + r, dst, r) for r in range(0, n_rows, chunk)]


def _cast_weights_into_vmem(jobs, stage, sems):
    slots, rows = stage.shape[0], stage.shape[1]
    ahead = slots - 1

    def copy(j):
        src, r0, _, _ = jobs[j]
        return pltpu.make_async_copy(src.at[pl.ds(r0, rows), :], stage.at[j % slots], sems.at[j % slots])

    for j in range(min(ahead, len(jobs))):
        copy(j).start()
    for j in range(len(jobs)):
        if j + ahead < len(jobs):
            copy(j + ahead).start()
        copy(j).wait()
        _, _, dst, d0 = jobs[j]
        dst[d0:d0 + rows, :] = stage[j % slots].astype(BF16)


def _cast_scratch(wide_rows, narrow_rows):
    return [pltpu.VMEM((CAST_SLOTS, wide_rows, D_FF), F32), pltpu.VMEM((CAST_SLOTS, narrow_rows, D_MODEL), F32),
            pltpu.SemaphoreType.DMA((CAST_SLOTS,)), pltpu.SemaphoreType.DMA((CAST_SLOTS,))]


def _rope(x, cos4, sin4):
    return x * cos4 + pltpu.roll(x, 2 * ROPE_HALF, 1) * sin4


def _rope_table_body(pos_ref, freq_ref, cos_ref, sin_ref):
    ang = freq_ref[...] * pos_ref[...].astype(F32)
    cos_ref[...] = jnp.cos(ang)
    sin_ref[...] = jnp.sin(ang)


def _rope_tables(positions):
    n_tok = positions.size
    freqs = 1.0 / (ROPE_BASE ** (jnp.arange(0, ROPE_DIM, 2, dtype=F32) / ROPE_DIM))
    blk = 2048
    return pl.pallas_call(
        _rope_table_body,
        grid=(n_tok // blk,),
        in_specs=[pl.BlockSpec((1, blk), lambda i: (0, i)),
                  pl.BlockSpec((ROPE_HALF, 1), lambda i: (0, 0))],
        out_specs=[pl.BlockSpec((ROPE_HALF, blk), lambda i: (0, i))] * 2,
        out_shape=[jax.ShapeDtypeStruct((ROPE_HALF, n_tok), F32)] * 2,
        name="rope_tables",
    )(positions.reshape(1, n_tok), freqs.reshape(ROPE_HALF, 1))


def _mem_kv_body(mem_ref, g_ref, w_ref, out_ref):
    mn = _rmsnorm(mem_ref[0], g_ref[...]).astype(BF16)
    out_ref[0] = _dot(mn, w_ref[...]).astype(BF16)


def _mem_kv(mem, mem_norm, w_mkv):
    batch = mem.shape[0]
    return pl.pallas_call(
        _mem_kv_body,
        grid=(batch,),
        in_specs=[pl.BlockSpec((1, N_MEM, D_MODEL), lambda b: (b, 0, 0)),
                  pl.BlockSpec((1, D_MODEL), lambda b: (0, 0)),
                  pl.BlockSpec((D_MODEL, 2 * D_MODEL), lambda b: (0, 0))],
        out_specs=pl.BlockSpec((1, N_MEM, 2 * D_MODEL), lambda b: (b, 0, 0)),
        out_shape=jax.ShapeDtypeStruct((batch, N_MEM, 2 * D_MODEL), BF16),
        name="mem_kv",
    )(mem, mem_norm, w_mkv)


def _pre_body(tiles_per_seq, n_tiles,
              x_ref, cos_ref, sin_ref, n1_ref, mixn_ref, win_ref, qn_ref, wq_ref, kvn_ref, wk_ref, wv_ref,
              poolw_ref, pscale_ref, wg_hbm, wu_hbm, wd_hbm, wout_hbm,
              h_ref, q_ref, k_ref, v_ref,
              ext_ref, h1_ref, z_ref, wg_ref, wu_ref, wd_ref, woutp_ref, stage_wide, stage, sem_wide, sem):
    step = pl.program_id(0)
    tm = x_ref.shape[0]

    @pl.when(step == 0)
    def _():
        _cast_weights_into_vmem(_row_jobs(wg_hbm, 0, D_MODEL, wg_ref, CAST_ROWS_WIDE)
                                + _row_jobs(wu_hbm, 0, D_MODEL, wu_ref, CAST_ROWS_WIDE), stage_wide, sem_wide)
        _cast_weights_into_vmem(_row_jobs(wd_hbm, 0, D_FF, wd_ref, CAST_ROWS)
                                + _row_jobs(wout_hbm, ATT_WIDTH, POOL_WIDTH, woutp_ref, CAST_ROWS), stage, sem)
        h1_ref[...] = jnp.zeros(h1_ref.shape, F32)
        z_ref[...] = jnp.zeros(z_ref.shape, F32)
        ext_ref[...] = jnp.zeros(ext_ref.shape, F32)

    def ffn_and_project():
        x = x_ref[...]
        xn = _rmsnorm(x, n1_ref[...]).astype(BF16)
        h1 = x + 0.5 * _swiglu(xn, wg_ref, wu_ref, wd_ref)
        un = _rmsnorm(h1, mixn_ref[...]).astype(BF16)
        return h1, _dot(un, win_ref[...])

    def mixer_inputs():
        seq_tile = jnp.maximum(step - 1, 0) % tiles_per_seq

        cos = cos_ref[...]
        sin = sin_ref[...]
        cos4_t = jnp.concatenate([cos, cos, cos, cos], axis=0)
        sin4_t = jnp.concatenate([-sin, -sin, sin, sin], axis=0)

        qn = _rmsnorm(z_ref[:, :Q_RANK], qn_ref[...]).astype(BF16)
        q_t = _dot_nt(wq_ref[...], qn)
        for h in range(N_HEADS):
            base = h * QK_DIM
            q_ref[0, h, :HEAD_DIM, :] = (q_t[base:base + HEAD_DIM] * Q_PRESCALE).astype(BF16)
            pe = q_t[base + HEAD_DIM:base + QK_DIM]
            pe = pe * cos4_t + pltpu.roll(pe, 2 * ROPE_HALF, 0) * sin4_t
            q_ref[0, h, HEAD_DIM:, :] = (pe * Q_PRESCALE).astype(BF16)

        kvn = _rmsnorm(z_ref[:, Q_RANK:Q_RANK + KV_RANK], kvn_ref[...]).astype(BF16)
        k_nope = _dot(kvn, wk_ref[...])
        v_t = _dot_nt(wv_ref[...], kvn)
        k_pe = _rope(z_ref[:, Q_RANK + KV_RANK:Q_RANK + KV_RANK + LANES], cos4_t.T, sin4_t.T).astype(BF16)
        for h in range(N_HEADS):
            k_ref[0, h, :, :HEAD_DIM] = k_nope[:, h * HEAD_DIM:(h + 1) * HEAD_DIM].astype(BF16)
            k_ref[0, h, :, HEAD_DIM:] = k_pe
            v_ref[0, h, 0] = v_t[h * HEAD_DIM:(h + 1) * HEAD_DIM].astype(BF16)

        zp = z_ref[:, D_MODEL - POOL_WIDTH:]
        ext_ref[0:POOL_HALO, :] = jnp.where(seq_tile == 0, 0.0, ext_ref[0:POOL_HALO, :])
        ext_ref[POOL_HALO:, :] = zp
        t1 = (seq_tile * tm + 1 + lax.broadcasted_iota(jnp.int32, (tm, 1), 0)).astype(F32)
        diffs = []
        for g, w in enumerate(POOL_WINDOWS):
            cols = slice(g * POOL_CH, (g + 1) * POOL_CH)
            tot = zp[:, cols]
            for back in range(1, w):
                tot = tot + ext_ref[POOL_HALO - back:POOL_HALO - back + tm, cols]
            mean = tot / jnp.minimum(t1, float(w))
            diffs.append(mean - zp[:, cols])
        ext_ref[0:POOL_HALO, :] = ext_ref[tm:tm + POOL_HALO, :]

        y01 = _dot(jnp.concatenate(diffs[:2], axis=1).astype(BF16), poolw_ref[0])
        y23 = _dot(jnp.concatenate(diffs[2:], axis=1).astype(BF16), poolw_ref[1])
        p = jnp.concatenate([y01, y23], axis=1) * pscale_ref[...]
        h_ref[...] = h1_ref[...] + _dot(p.astype(BF16), woutp_ref[...])

    @pl.when(step < n_tiles)
    def _():
        mixer_inputs()
        h1, z = ffn_and_project()
        h1_ref[...] = h1
        z_ref[...] = z

    @pl.when(step == n_tiles)
    def _():
        mixer_inputs()


def _const_spec(shape):
    nd = len(shape)
    return pl.BlockSpec(shape, lambda *_: (0,) * nd, pipeline_mode=pl.Buffered(1))


def _pre(x2d, cos, sin, n1, mixn, win, qn, wq, kvn, wk, wv, poolw, pscale, wg, wu, wd, wout, batch, seq):
    tm = TOKEN_TILE
    assert tm == KV_TILE
    n_tok = x2d.shape[0]
    tiles_per_seq = seq // tm
    n_tiles = n_tok // tm
    lag = lambda i: jnp.maximum(i - 1, 0)
    bt = lambda i: (lag(i) // tiles_per_seq, lag(i) % tiles_per_seq)
    x_spec = pl.BlockSpec((tm, D_MODEL), lambda i: (jnp.minimum(i, n_tiles - 1), 0))
    h_spec = pl.BlockSpec((tm, D_MODEL), lambda i: (lag(i), 0))
    rope_spec = pl.BlockSpec((ROPE_HALF, tm), lambda i: (0, lag(i)))
    q_spec = pl.BlockSpec((1, N_HEADS, QK_DIM, tm), lambda i: (bt(i)[0], 0, 0, bt(i)[1]))
    k_spec = pl.BlockSpec((1, N_HEADS, tm, QK_DIM), lambda i: (bt(i)[0], 0, bt(i)[1], 0))
    v_spec = pl.BlockSpec((1, N_HEADS, 1, HEAD_DIM, tm), lambda i: (bt(i)[0], 0, bt(i)[1], 0, 0))
    weights = (n1, mixn, win, qn, wq, kvn, wk, wv, poolw, pscale)
    hbm_weights = (wg, wu, wd, wout)
    return pl.pallas_call(
        functools.partial(_pre_body, tiles_per_seq, n_tiles),
        grid=(n_tiles + 1,),
        in_specs=[x_spec, rope_spec, rope_spec] + [_const_spec(w.shape) for w in weights]
                 + [pl.BlockSpec(memory_space=pl.ANY)] * len(hbm_weights),
        out_specs=[h_spec, q_spec, k_spec, v_spec],
        out_shape=[jax.ShapeDtypeStruct((n_tok, D_MODEL), F32),
                   jax.ShapeDtypeStruct((batch, N_HEADS, QK_DIM, seq), BF16),
                   jax.ShapeDtypeStruct((batch, N_HEADS, seq, QK_DIM), BF16),
                   jax.ShapeDtypeStruct((batch, N_HEADS, tiles_per_seq, HEAD_DIM, tm), BF16)],
        scratch_shapes=[pltpu.VMEM((tm + POOL_HALO, POOL_WIDTH), F32),
                        pltpu.VMEM((tm, D_MODEL), F32), pltpu.VMEM((tm, D_MODEL), F32),
                        pltpu.VMEM((D_MODEL, D_FF), BF16), pltpu.VMEM((D_MODEL, D_FF), BF16),
                        pltpu.VMEM((D_FF, D_MODEL), BF16), pltpu.VMEM((POOL_WIDTH, D_MODEL), BF16)]
                       + _cast_scratch(CAST_ROWS_WIDE, CAST_ROWS),
        compiler_params=pltpu.CompilerParams(dimension_semantics=("arbitrary",),
                                             vmem_limit_bytes=VMEM_LIMIT_BYTES),
        name="pre",
    )(x2d, cos, sin, *weights, *hbm_weights)


def _attn_body(n_cast, q_ref, k_ref, v_ref, *refs):
    cast_in, o_ref, cast_out = refs[:n_cast], refs[n_cast], refs[n_cast + 1:2 * n_cast + 1]
    m_ref, l_ref, acc_ref, gap_ref = refs[2 * n_cast + 1:]
    for src, dst in zip(cast_in, cast_out):
        dst[...] = src[...].astype(BF16)

    qi = pl.program_id(1)
    tq = q_ref.shape[3]
    tk = KV_TILE
    half = tk // 2

    def scores(h, c):
        return _dot(k_ref[0, h, pl.ds(pl.multiple_of(c * tk, tk), tk), :], q_ref[0, h])


    def init_with_diagonal():
        causal = (lax.broadcasted_iota(jnp.int32, (half, half), 0)
                  <= lax.broadcasted_iota(jnp.int32, (half, half), 1))
        start = pl.multiple_of(qi * tk, tk)
        s_lo = [_dot(k_ref[0, h, pl.ds(start, half), :], q_ref[0, h]) for h in range(N_HEADS)]
        s_hi = [_dot(k_ref[0, h, pl.ds(start + half, half), :], q_ref[0, h, :, half:]) for h in range(N_HEADS)]
        probs = []
        for h in range(N_HEADS):
            s_ll = jnp.where(causal, s_lo[h][:, :half], -jnp.inf)
            s_lu = s_lo[h][:, half:]
            s_uu = jnp.where(causal, s_hi[h], -jnp.inf)
            m_l = jnp.max(s_ll, axis=0, keepdims=True)
            m_u = jnp.maximum(jnp.max(s_lu, axis=0, keepdims=True), jnp.max(s_uu, axis=0, keepdims=True))
            p_ll = jnp.exp2(s_ll - m_l)
            p_lu = jnp.exp2(s_lu - m_u)
            p_uu = jnp.exp2(s_uu - m_u)
            l_l = jnp.sum(p_ll, axis=0, keepdims=True)
            l_u = jnp.sum(p_lu, axis=0, keepdims=True) + jnp.sum(p_uu, axis=0, keepdims=True)
            m_ref[h, :, :half] = jnp.broadcast_to(m_l, (STAT_ROWS, half))
            m_ref[h, :, half:] = jnp.broadcast_to(m_u, (STAT_ROWS, half))
            l_ref[h, :, :half] = jnp.broadcast_to(l_l, (STAT_ROWS, half))
            l_ref[h, :, half:] = jnp.broadcast_to(l_u, (STAT_ROWS, half))
            probs.append((jnp.concatenate([p_ll, p_lu], axis=1).astype(BF16), p_uu.astype(BF16)))
        for h in range(N_HEADS):
            v_t = v_ref[0, h, qi]
            acc_l = _dot(v_t[:, :half], probs[h][0])
            acc_u = _dot(v_t[:, half:], probs[h][1])
            acc_ref[h, :, :half] = acc_l[:, :half]
            acc_ref[h, :, half:] = acc_l[:, half:] + acc_u

    def update_single_pass(blocks):
        s_all = [[scores(h, c) for c in blocks] for h in range(N_HEADS)]
        betas, probs = [], []
        for h in range(N_HEADS):
            m_old = m_ref[h, 0:1, :]
            ps = [jnp.exp2(s - m_old) for s in s_all[h]]
            m_blk = jnp.max(s_all[h][0], axis=0, keepdims=True)
            for s in s_all[h][1:]:
                m_blk = jnp.maximum(m_blk, jnp.max(s, axis=0, keepdims=True))
            m_new = jnp.maximum(m_old, m_blk)
            beta = jnp.exp2(m_old - m_new)
            l_new = l_ref[h, 0:1, :]
            for p in ps:
                l_new = l_new + jnp.sum(p, axis=0, keepdims=True)
            m_ref[h] = jnp.broadcast_to(m_new, (STAT_ROWS, tq))
            l_ref[h] = jnp.broadcast_to(l_new * beta, (STAT_ROWS, tq))
            gap_ref[h] = jnp.maximum(gap_ref[h], jnp.broadcast_to(m_blk - m_old, (STAT_ROWS, tq)))
            betas.append(beta)
            probs.append([p.astype(BF16) for p in ps])
        for h in range(N_HEADS):
            acc = acc_ref[h]
            for c, p in zip(blocks, probs[h]):
                acc = acc + _dot(v_ref[0, h, c], p)
            acc_ref[h] = acc * betas[h]

    def update_two_pass(c):
        s_all = [scores(h, c) for h in range(N_HEADS)]
        alphas, probs = [], []
        for h in range(N_HEADS):
            m_old = m_ref[h, 0:1, :]
            m_new = jnp.maximum(m_old, jnp.max(s_all[h], axis=0, keepdims=True))
            alpha = jnp.exp2(m_old - m_new)
            p = jnp.exp2(s_all[h] - m_new)
            l_new = alpha * l_ref[h, 0:1, :] + jnp.sum(p, axis=0, keepdims=True)
            m_ref[h] = jnp.broadcast_to(m_new, (STAT_ROWS, tq))
            l_ref[h] = jnp.broadcast_to(l_new, (STAT_ROWS, tq))
            alphas.append(alpha)
            probs.append(p.astype(BF16))
        for h in range(N_HEADS):
            acc_ref[h] = alphas[h] * acc_ref[h] + _dot(v_ref[0, h, c], probs[h])

    init_with_diagonal()
    gap_ref[...] = jnp.zeros(gap_ref.shape, F32)

    def pair(j, _):
        update_single_pass([2 * j, 2 * j + 1])
        return 0

    lax.fori_loop(0, lax.shift_right_logical(qi, 1), pair, 0)

    @pl.when(qi % 2 == 1)
    def _():
        update_single_pass([qi - 1])

    worst = gap_ref[0]
    for h in range(1, N_HEADS):
        worst = jnp.maximum(worst, gap_ref[h])

    @pl.when(jnp.max(worst) > MAX_SINGLE_PASS_JUMP)
    def _():
        init_with_diagonal()

        def one(c, _):
            update_two_pass(c)
            return 0

        lax.fori_loop(0, qi, one, 0)

    for h in range(N_HEADS):
        o_ref[0, :, h * HEAD_DIM:(h + 1) * HEAD_DIM] = (acc_ref[h] / l_ref[h, 0:1, :]).T.astype(BF16)


def _cast_block_specs(w, n_rows, steps, n_q):
    cols = w.shape[1]
    per_step = n_rows // steps
    blk = per_step if n_rows % steps == 0 and per_step % BF16_SUBLANES == 0 else LANES
    n_blk = n_rows // blk
    assert n_rows % blk == 0 and n_blk <= steps
    index = lambda b, i: (jnp.minimum(b * n_q + i, n_blk - 1), 0)
    return (pl.BlockSpec((blk, cols), index), pl.BlockSpec((blk, cols), index),
            jax.ShapeDtypeStruct((n_rows, cols), BF16))


def _mla_attention(q, k, v, cast_jobs):
    batch, _, seq, _ = k.shape
    assert Q_TILE == KV_TILE
    n_q = seq // Q_TILE
    cast_specs = [_cast_block_specs(w, rows, batch * n_q, n_q) for w, rows in cast_jobs]
    outs = pl.pallas_call(
        functools.partial(_attn_body, len(cast_jobs)),
        grid=(batch, n_q),
        in_specs=[pl.BlockSpec((1, N_HEADS, QK_DIM, Q_TILE), lambda b, i: (b, 0, 0, i)),
                  pl.BlockSpec((1, N_HEADS, seq, QK_DIM), lambda b, i: (b, 0, 0, 0)),
                  pl.BlockSpec((1, N_HEADS, seq // KV_TILE, HEAD_DIM, KV_TILE), lambda b, i: (b, 0, 0, 0, 0))]
                 + [c[0] for c in cast_specs],
        out_specs=[pl.BlockSpec((1, Q_TILE, ATT_WIDTH), lambda b, i: (b, i, 0))] + [c[1] for c in cast_specs],
        out_shape=[jax.ShapeDtypeStruct((batch, seq, ATT_WIDTH), BF16)] + [c[2] for c in cast_specs],
        scratch_shapes=[pltpu.VMEM((N_HEADS, STAT_ROWS, Q_TILE), F32),
                        pltpu.VMEM((N_HEADS, STAT_ROWS, Q_TILE), F32),
                        pltpu.VMEM((N_HEADS, HEAD_DIM, Q_TILE), F32),
                        pltpu.VMEM((N_HEADS, STAT_ROWS, Q_TILE), F32)],
        compiler_params=pltpu.CompilerParams(dimension_semantics=("arbitrary", "arbitrary"),
                                             vmem_limit_bytes=VMEM_LIMIT_BYTES),
        name="mla_attn",
    )(q, k, v, *[w for w, _ in cast_jobs])
    return outs[0], outs[1:]


def _post_body(h_ref, a_ref, mkv_ref, xn_ref, n2_ref, fn_ref, wouta_ref, wmq_ref, wmo_ref, wg_ref, wu_ref, wd_ref,
               out_ref):
    h2 = h_ref[...] + _dot(a_ref[...], wouta_ref[...])

    hn = _rmsnorm(h2, xn_ref[...]).astype(BF16)
    q = (_dot(hn, wmq_ref[...]) * MEM_SCALE).astype(BF16)
    cols = [slice(h * MEM_HEAD_DIM, (h + 1) * MEM_HEAD_DIM) for h in range(MEM_HEADS)]
    scores = [_dot_nt(q[:, c], mkv_ref[0, :, c]) for c in cols]
    probs = []
    for s in scores:
        e = jnp.exp(s - jnp.max(s, axis=-1, keepdims=True))
        probs.append((e / jnp.sum(e, axis=-1, keepdims=True)).astype(BF16))
    heads = [_dot(p, mkv_ref[0, :, D_MODEL + h * MEM_HEAD_DIM:D_MODEL + (h + 1) * MEM_HEAD_DIM]).astype(BF16)
             for h, p in enumerate(probs)]
    h3 = h2 + _dot(jnp.concatenate(heads, axis=1), wmo_ref[...])

    xn = _rmsnorm(h3, n2_ref[...]).astype(BF16)
    h4 = h3 + 0.5 * _swiglu(xn, wg_ref, wu_ref, wd_ref)
    out_ref[...] = _rmsnorm(h4, fn_ref[...])


def _post(h2d, a2d, mkv, xn, n2, fn, wouta, wmq, wmo, wg, wu, wd, seq):
    tm = POST_TILE
    n_tok = h2d.shape[0]
    tiles_per_seq = seq // tm
    weights = (xn, n2, fn, wouta, wmq, wmo, wg, wu, wd)
    return pl.pallas_call(
        _post_body,
        grid=(n_tok // tm,),
        in_specs=[pl.BlockSpec((tm, D_MODEL), lambda i: (i, 0)),
                  pl.BlockSpec((tm, ATT_WIDTH), lambda i: (i, 0)),
                  pl.BlockSpec((1, N_MEM, 2 * D_MODEL), lambda i: (i // tiles_per_seq, 0, 0))]
                 + [_const_spec(w.shape) for w in weights],
        out_specs=pl.BlockSpec((tm, D_MODEL), lambda i: (i, 0)),
        out_shape=jax.ShapeDtypeStruct((n_tok, D_MODEL), F32),
        compiler_params=pltpu.CompilerParams(dimension_semantics=("arbitrary",),
                                             vmem_limit_bytes=VMEM_LIMIT_BYTES),
        name="post",
    )(h2d, a2d, mkv, *weights)


def _pad_rope_cols(w):
    zeros = jnp.zeros(w.shape[:-1] + (ROPE_HALF,), w.dtype)
    return jnp.concatenate([w[..., :ROPE_HALF], zeros, w[..., ROPE_HALF:], zeros], axis=-1)


def _block_diag2(a, b):
    za = jnp.zeros((a.shape[0], b.shape[1]), a.dtype)
    zb = jnp.zeros((b.shape[0], a.shape[1]), a.dtype)
    return jnp.concatenate([jnp.concatenate([a, za], axis=1), jnp.concatenate([zb, b], axis=1)], axis=0)


def _layer(h2d, mem, cos, sin, batch, seq, ffn1_norm, ffn1_w_gate, ffn1_w_up, ffn1_w_down, mix_norm, w_in,
           q_norm, w_q_up, kv_norm, w_kv_up, pool_w, pool_scale, w_out, xattn_norm, mem_norm,
           w_mq, w_mkv, w_mo, ffn2_norm, ffn2_w_gate, ffn2_w_up, ffn2_w_down, out_norm):
    row = lambda g: g.reshape(1, -1).astype(F32)
    rope_lo = Q_RANK + KV_RANK
    w_in_p = jnp.concatenate([w_in[:, :rope_lo], _pad_rope_cols(w_in[:, rope_lo:rope_lo + ROPE_DIM]),
                              w_in[:, rope_lo + ROPE_DIM:]], axis=1).astype(BF16)
    wq = w_q_up.reshape(Q_RANK, N_HEADS, HEAD_DIM + ROPE_DIM)
    wq_p = jnp.concatenate([wq[..., :HEAD_DIM], _pad_rope_cols(wq[..., HEAD_DIM:])], axis=-1)
    wq_t = wq_p.reshape(Q_RANK, N_HEADS * QK_DIM).T.astype(BF16)
    wkv = w_kv_up.reshape(KV_RANK, N_HEADS, 2 * HEAD_DIM)
    wk = wkv[..., :HEAD_DIM].reshape(KV_RANK, ATT_WIDTH).astype(BF16)
    wv_t = wkv[..., HEAD_DIM:].reshape(KV_RANK, ATT_WIDTH).T.astype(BF16)
    poolw = jnp.stack([_block_diag2(pool_w[0], pool_w[1]), _block_diag2(pool_w[2], pool_w[3])]).astype(BF16)

    mkv = _mem_kv(mem, row(mem_norm), w_mkv.astype(BF16))
    h1p, q, k, v = _pre(h2d, cos, sin, row(ffn1_norm), row(mix_norm), w_in_p, row(q_norm), wq_t, row(kv_norm),
                        wk, wv_t, poolw, row(pool_scale), ffn1_w_gate, ffn1_w_up, ffn1_w_down, w_out, batch, seq)
    a, post_weights = _mla_attention(q, k, v, [(w_out, ATT_WIDTH), (w_mq, D_MODEL), (w_mo, D_MODEL),
                                               (ffn2_w_gate, D_MODEL), (ffn2_w_up, D_MODEL), (ffn2_w_down, D_FF)])
    return _post(h1p, a.reshape(batch * seq, ATT_WIDTH), mkv, row(xattn_norm), row(ffn2_norm), row(out_norm),
                 *post_weights, seq)


def kernel(x, mem, positions, ffn1_norm, ffn1_w_gate, ffn1_w_up, ffn1_w_down, mix_norm, w_in, q_norm, w_q_up,
           kv_norm, w_kv_up, pool_w, pool_scale, w_out, xattn_norm, mem_norm, w_mq, w_mkv, w_mo, ffn2_norm,
           ffn2_w_gate, ffn2_w_up, ffn2_w_down, final_norm):
    batch, seq, d_model = x.shape
    depth = ffn1_norm.shape[0]
    assert d_model == D_MODEL and depth == 1
    assert seq % TOKEN_TILE == 0 and seq % POST_TILE == 0 and seq % Q_TILE == 0
    cos, sin = _rope_tables(positions)
    layer = (ffn1_norm, ffn1_w_gate, ffn1_w_up, ffn1_w_down, mix_norm, w_in, q_norm, w_q_up, kv_norm, w_kv_up,
             pool_w, pool_scale, w_out, xattn_norm, mem_norm, w_mq, w_mkv, w_mo, ffn2_norm, ffn2_w_gate,
             ffn2_w_up, ffn2_w_down)
    out = _layer(x.reshape(batch * seq, d_model), mem, cos, sin, batch, seq, *[w[0] for w in layer], final_norm)
    return out.reshape(batch, seq, d_model)
```

```python
import functools

import jax
import jax.numpy as jnp
from jax import lax
from jax.experimental import pallas as pl
from jax.experimental.pallas import tpu as pltpu

F32 = jnp.float32
BF16 = jnp.bfloat16

D_MODEL = 1024
N_HEADS = 4
HEAD_DIM = 128
ROPE_DIM = 64
ROPE_HALF = ROPE_DIM // 2
Q_RANK = 256
KV_RANK = 128
ATT_WIDTH = N_HEADS * HEAD_DIM
POOL_WIDTH = D_MODEL - ATT_WIDTH
POOL_WINDOWS = (2, 4, 8, 16)
POOL_CH = POOL_WIDTH // len(POOL_WINDOWS)
POOL_HALO = 16
D_FF = 2816
N_MEM = 256
MEM_HEADS = 4
MEM_HEAD_DIM = D_MODEL // MEM_HEADS
ROPE_BASE = 10000.0
RMS_EPS = 1e-6
ATT_SCALE = (HEAD_DIM + ROPE_DIM) ** -0.5
LOG2_E = 1.4426950408889634
Q_PRESCALE = ATT_SCALE * LOG2_E
MEM_SCALE = MEM_HEAD_DIM ** -0.5

LANES = 128
BF16_SUBLANES = 16
MXU_DIM = 256
VMEM_LIMIT_BYTES = 56 * 1024 * 1024

TOKEN_TILE = 512
POST_TILE = 1024
FF_CHUNK = MXU_DIM
Q_TILE = 512
KV_TILE = 512
QK_DIM = 2 * HEAD_DIM
CAST_ROWS_WIDE = 128
CAST_ROWS = 256
CAST_SLOTS = 4
STAT_ROWS = 8
MAX_SINGLE_PASS_JUMP = 32.0


def _rmsnorm(x, g):
    ms = jnp.mean(x * x, axis=-1, keepdims=True)
    return (x * lax.rsqrt(ms + RMS_EPS)) * g


def _dot(a, b):
    return jnp.dot(a, b, preferred_element_type=F32)


def _dot_nt(a, b):
    return lax.dot_general(a, b, (((1,), (1,)), ((), ())), preferred_element_type=F32)


def _swiglu(xn, wg_ref, wu_ref, wd_ref):
    n_chunks = D_FF // FF_CHUNK
    cols = lambda c: slice(c * FF_CHUNK, (c + 1) * FF_CHUNK)
    gate_up = lambda c: (_dot(xn, wg_ref[:, cols(c)]), _dot(xn, wu_ref[:, cols(c)]))
    acc = None
    g, u = gate_up(0)
    for c in range(n_chunks):
        nxt = gate_up(c + 1) if c + 1 < n_chunks else None
        a = (g / (1.0 + jnp.exp(-g))) * u
        d = _dot(a.astype(BF16), wd_ref[cols(c), :])
        acc = d if acc is None else acc + d
        if nxt is not None:
            g, u = nxt
    return acc


def _row_jobs(src, src_row0, n_rows, dst, chunk):
    assert n_rows % chunk == 0
    return [(src, src_row0 + r, dst, r) for r in range(0, n_rows, chunk)]


def _cast_weights_into_vmem(jobs, stage, sems):
    slots, rows = stage.shape[0], stage.shape[1]
    ahead = slots - 1

    def copy(j):
        src, r0, _, _ = jobs[j]
        return pltpu.make_async_copy(src.at[pl.ds(r0, rows), :], stage.at[j % slots], sems.at[j % slots])

    for j in range(min(ahead, len(jobs))):
        copy(j).start()
    for j in range(len(jobs)):
        if j + ahead < len(jobs):
            copy(j + ahead).start()
        copy(j).wait()
        _, _, dst, d0 = jobs[j]
        dst[d0:d0 + rows, :] = stage[j % slots].astype(BF16)


def _cast_scratch(wide_rows, narrow_rows):
    return [pltpu.VMEM((CAST_SLOTS, wide_rows, D_FF), F32), pltpu.VMEM((CAST_SLOTS, narrow_rows, D_MODEL), F32),
            pltpu.SemaphoreType.DMA((CAST_SLOTS,)), pltpu.SemaphoreType.DMA((CAST_SLOTS,))]


def _rope(x, cos4, sin4):
    return x * cos4 + pltpu.roll(x, 2 * ROPE_HALF, 1) * sin4


def _mem_kv_body(mem_ref, g_ref, w_ref, out_ref):
    mn = _rmsnorm(mem_ref[0], g_ref[...]).astype(BF16)
    out_ref[0] = _dot(mn, w_ref[...]).astype(BF16)


def _mem_kv(mem, mem_norm, w_mkv):
    batch = mem.shape[0]
    return pl.pallas_call(
        _mem_kv_body,
        grid=(batch,),
        in_specs=[pl.BlockSpec((1, N_MEM, D_MODEL), lambda b: (b, 0, 0)),
                  pl.BlockSpec((1, D_MODEL), lambda b: (0, 0)),
                  pl.BlockSpec((D_MODEL, 2 * D_MODEL), lambda b: (0, 0))],
        out_specs=pl.BlockSpec((1, N_MEM, 2 * D_MODEL), lambda b: (b, 0, 0)),
        out_shape=jax.ShapeDtypeStruct((batch, N_MEM, 2 * D_MODEL), BF16),
        name="mem_kv",
    )(mem, mem_norm, w_mkv)


def _pre_body(tiles_per_seq, n_tiles,
              x_ref, pos_ref, freq_ref, n1_ref, mixn_ref, win_ref, qn_ref, wq_ref, kvn_ref, wk_ref, wv_ref,
              poolw_ref, pscale_ref, wg_hbm, wu_hbm, wd_hbm, wout_hbm,
              h_ref, q_ref, k_ref, v_ref,
              ext_ref, h1_ref, z_ref, cos_ref, sin_ref, wg_ref, wu_ref, wd_ref, woutp_ref,
              stage_wide, stage, sem_wide, sem):
    step = pl.program_id(0)
    tm = x_ref.shape[0]

    @pl.when(step == 0)
    def _():
        _cast_weights_into_vmem(_row_jobs(wg_hbm, 0, D_MODEL, wg_ref, CAST_ROWS_WIDE)
                                + _row_jobs(wu_hbm, 0, D_MODEL, wu_ref, CAST_ROWS_WIDE), stage_wide, sem_wide)
        _cast_weights_into_vmem(_row_jobs(wd_hbm, 0, D_FF, wd_ref, CAST_ROWS)
                                + _row_jobs(wout_hbm, ATT_WIDTH, POOL_WIDTH, woutp_ref, CAST_ROWS), stage, sem)
        h1_ref[...] = jnp.zeros(h1_ref.shape, F32)
        z_ref[...] = jnp.zeros(z_ref.shape, F32)
        ext_ref[...] = jnp.zeros(ext_ref.shape, F32)
        cos_ref[...] = jnp.zeros(cos_ref.shape, F32)
        sin_ref[...] = jnp.zeros(sin_ref.shape, F32)

    def ffn_and_project():
        x = x_ref[...]
        xn = _rmsnorm(x, n1_ref[...]).astype(BF16)
        h1 = x + 0.5 * _swiglu(xn, wg_ref, wu_ref, wd_ref)
        un = _rmsnorm(h1, mixn_ref[...]).astype(BF16)
        z = _dot(un, win_ref[...])
        ang = freq_ref[...] * pos_ref[...].astype(F32)
        return h1, z, jnp.cos(ang), jnp.sin(ang)

    def mixer_inputs():
        seq_tile = jnp.maximum(step - 1, 0) % tiles_per_seq

        cos = cos_ref[...]
        sin = sin_ref[...]
        cos4_t = jnp.concatenate([cos, cos, cos, cos], axis=0)
        sin4_t = jnp.concatenate([-sin, -sin, sin, sin], axis=0)

        qn = _rmsnorm(z_ref[:, :Q_RANK], qn_ref[...]).astype(BF16)
        q_t = _dot_nt(wq_ref[...], qn)
        for h in range(N_HEADS):
            base = h * QK_DIM
            q_ref[0, h, :HEAD_DIM, :] = (q_t[base:base + HEAD_DIM] * Q_PRESCALE).astype(BF16)
            pe = q_t[base + HEAD_DIM:base + QK_DIM]
            pe = pe * cos4_t + pltpu.roll(pe, 2 * ROPE_HALF, 0) * sin4_t
            q_ref[0, h, HEAD_DIM:, :] = (pe * Q_PRESCALE).astype(BF16)

        kvn = _rmsnorm(z_ref[:, Q_RANK:Q_RANK + KV_RANK], kvn_ref[...]).astype(BF16)
        k_nope = _dot(kvn, wk_ref[...])
        v_t = _dot_nt(wv_ref[...], kvn)
        k_pe = _rope(z_ref[:, Q_RANK + KV_RANK:Q_RANK + KV_RANK + LANES], cos4_t.T, sin4_t.T).astype(BF16)
        for h in range(N_HEADS):
            k_ref[0, h, :, :HEAD_DIM] = k_nope[:, h * HEAD_DIM:(h + 1) * HEAD_DIM].astype(BF16)
            k_ref[0, h, :, HEAD_DIM:] = k_pe
            v_ref[0, h, 0] = v_t[h * HEAD_DIM:(h + 1) * HEAD_DIM].astype(BF16)

        zp = z_ref[:, D_MODEL - POOL_WIDTH:]
        ext_ref[0:POOL_HALO, :] = jnp.where(seq_tile == 0, 0.0, ext_ref[0:POOL_HALO, :])
        ext_ref[POOL_HALO:, :] = zp
        t1 = (seq_tile * tm + 1 + lax.broadcasted_iota(jnp.int32, (tm, 1), 0)).astype(F32)
        diffs = []
        for g, w in enumerate(POOL_WINDOWS):
            cols = slice(g * POOL_CH, (g + 1) * POOL_CH)
            tot = zp[:, cols]
            for back in range(1, w):
                tot = tot + ext_ref[POOL_HALO - back:POOL_HALO - back + tm, cols]
            mean = tot / jnp.minimum(t1, float(w))
            diffs.append(mean - zp[:, cols])
        ext_ref[0:POOL_HALO, :] = ext_ref[tm:tm + POOL_HALO, :]

        y01 = _dot(jnp.concatenate(diffs[:2], axis=1).astype(BF16), poolw_ref[0])
        y23 = _dot(jnp.concatenate(diffs[2:], axis=1).astype(BF16), poolw_ref[1])
        p = jnp.concatenate([y01, y23], axis=1) * pscale_ref[...]
        h_ref[...] = h1_ref[...] + _dot(p.astype(BF16), woutp_ref[...])

    @pl.when(step < n_tiles)
    def _():
        mixer_inputs()
        h1, z, cos, sin = ffn_and_project()
        h1_ref[...] = h1
        z_ref[...] = z
        cos_ref[...] = cos
        sin_ref[...] = sin

    @pl.when(step == n_tiles)
    def _():
        mixer_inputs()


def _const_spec(shape):
    nd = len(shape)
    return pl.BlockSpec(shape, lambda *_: (0,) * nd, pipeline_mode=pl.Buffered(1))


def _pre(x2d, pos, freqs, n1, mixn, win, qn, wq, kvn, wk, wv, poolw, pscale, wg, wu, wd, wout, batch, seq):
    tm = TOKEN_TILE
    assert tm == KV_TILE
    n_tok = x2d.shape[0]
    tiles_per_seq = seq // tm
    n_tiles = n_tok // tm
    lag = lambda i: jnp.maximum(i - 1, 0)
    bt = lambda i: (lag(i) // tiles_per_seq, lag(i) % tiles_per_seq)
    x_spec = pl.BlockSpec((tm, D_MODEL), lambda i: (jnp.minimum(i, n_tiles - 1), 0))
    h_spec = pl.BlockSpec((tm, D_MODEL), lambda i: (lag(i), 0))
    pos_spec = pl.BlockSpec((1, tm), lambda i: (0, jnp.minimum(i, n_tiles - 1)))
    q_spec = pl.BlockSpec((1, N_HEADS, QK_DIM, tm), lambda i: (bt(i)[0], 0, 0, bt(i)[1]))
    k_spec = pl.BlockSpec((1, N_HEADS, tm, QK_DIM), lambda i: (bt(i)[0], 0, bt(i)[1], 0))
    v_spec = pl.BlockSpec((1, N_HEADS, 1, HEAD_DIM, tm), lambda i: (bt(i)[0], 0, bt(i)[1], 0, 0))
    weights = (n1, mixn, win, qn, wq, kvn, wk, wv, poolw, pscale)
    hbm_weights = (wg, wu, wd, wout)
    return pl.pallas_call(
        functools.partial(_pre_body, tiles_per_seq, n_tiles),
        grid=(n_tiles + 1,),
        in_specs=[x_spec, pos_spec, _const_spec(freqs.shape)] + [_const_spec(w.shape) for w in weights]
                 + [pl.BlockSpec(memory_space=pl.ANY)] * len(hbm_weights),
        out_specs=[h_spec, q_spec, k_spec, v_spec],
        out_shape=[jax.ShapeDtypeStruct((n_tok, D_MODEL), F32),
                   jax.ShapeDtypeStruct((batch, N_HEADS, QK_DIM, seq), BF16),
                   jax.ShapeDtypeStruct((batch, N_HEADS, seq, QK_DIM), BF16),
                   jax.ShapeDtypeStruct((batch, N_HEADS, tiles_per_seq, HEAD_DIM, tm), BF16)],
        scratch_shapes=[pltpu.VMEM((tm + POOL_HALO, POOL_WIDTH), F32),
                        pltpu.VMEM((tm, D_MODEL), F32), pltpu.VMEM((tm, D_MODEL), F32),
                        pltpu.VMEM((ROPE_HALF, tm), F32), pltpu.VMEM((ROPE_HALF, tm), F32),
                        pltpu.VMEM((D_MODEL, D_FF), BF16), pltpu.VMEM((D_MODEL, D_FF), BF16),
                        pltpu.VMEM((D_FF, D_MODEL), BF16), pltpu.VMEM((POOL_WIDTH, D_MODEL), BF16)]
                       + _cast_scratch(CAST_ROWS_WIDE, CAST_ROWS),
        compiler_params=pltpu.CompilerParams(dimension_semantics=("arbitrary",),
                                             vmem_limit_bytes=VMEM_LIMIT_BYTES),
        name="pre",
    )(x2d, pos, freqs, *weights, *hbm_weights)


def _attn_body(n_cast, q_ref, k_ref, v_ref, *refs):
    cast_in, o_ref, cast_out = refs[:n_cast], refs[n_cast], refs[n_cast + 1:2 * n_cast + 1]
    m_ref, l_ref, acc_ref, gap_ref = refs[2 * n_cast + 1:]
    for src, dst in zip(cast_in, cast_out):
        dst[...] = src[...].astype(BF16)

    qi = pl.program_id(1)
    tq = q_ref.shape[3]
    tk = KV_TILE
    half = tk // 2

    def scores(h, c):
        return _dot(k_ref[0, h, pl.ds(pl.multiple_of(c * tk, tk), tk), :], q_ref[0, h])


    def init_with_diagonal():
        causal = (lax.broadcasted_iota(jnp.int32, (half, half), 0)
                  <= lax.broadcasted_iota(jnp.int32, (half, half), 1))
        start = pl.multiple_of(qi * tk, tk)
        s_lo = [_dot(k_ref[0, h, pl.ds(start, half), :], q_ref[0, h]) for h in range(N_HEADS)]
        s_hi = [_dot(k_ref[0, h, pl.ds(start + half, half), :], q_ref[0, h, :, half:]) for h in range(N_HEADS)]
        probs = []
        for h in range(N_HEADS):
            s_ll = jnp.where(causal, s_lo[h][:, :half], -jnp.inf)
            s_lu = s_lo[h][:, half:]
            s_uu = jnp.where(causal, s_hi[h], -jnp.inf)
            m_l = jnp.max(s_ll, axis=0, keepdims=True)
            m_u = jnp.maximum(jnp.max(s_lu, axis=0, keepdims=True), jnp.max(s_uu, axis=0, keepdims=True))
            p_ll = jnp.exp2(s_ll - m_l)
            p_lu = jnp.exp2(s_lu - m_u)
            p_uu = jnp.exp2(s_uu - m_u)
            l_l = jnp.sum(p_ll, axis=0, keepdims=True)
            l_u = jnp.sum(p_lu, axis=0, keepdims=True) + jnp.sum(p_uu, axis=0, keepdims=True)
            m_ref[h, :, :half] = jnp.broadcast_to(m_l, (STAT_ROWS, half))
            m_ref[h, :, half:] = jnp.broadcast_to(m_u, (STAT_ROWS, half))
            l_ref[h, :, :half] = jnp.broadcast_to(l_l, (STAT_ROWS, half))
            l_ref[h, :, half:] = jnp.broadcast_to(l_u, (STAT_ROWS, half))
            probs.append((jnp.concatenate([p_ll, p_lu], axis=1).astype(BF16), p_uu.astype(BF16)))
        for h in range(N_HEADS):
            v_t = v_ref[0, h, qi]
            acc_l = _dot(v_t[:, :half], probs[h][0])
            acc_u = _dot(v_t[:, half:], probs[h][1])
            acc_ref[h, :, :half] = acc_l[:, :half]
            acc_ref[h, :, half:] = acc_l[:, half:] + acc_u

    def update_single_pass(blocks):
        s_all = [[scores(h, c) for c in blocks] for h in range(N_HEADS)]
        betas, probs = [], []
        for h in range(N_HEADS):
            m_old = m_ref[h, 0:1, :]
            ps = [jnp.exp2(s - m_old) for s in s_all[h]]
            m_blk = jnp.max(s_all[h][0], axis=0, keepdims=True)
            for s in s_all[h][1:]:
                m_blk = jnp.maximum(m_blk, jnp.max(s, axis=0, keepdims=True))
            m_new = jnp.maximum(m_old, m_blk)
            beta = jnp.exp2(m_old - m_new)
            l_new = l_ref[h, 0:1, :]
            for p in ps:
                l_new = l_new + jnp.sum(p, axis=0, keepdims=True)
            m_ref[h] = jnp.broadcast_to(m_new, (STAT_ROWS, tq))
            l_ref[h] = jnp.broadcast_to(l_new * beta, (STAT_ROWS, tq))
            gap_ref[h] = jnp.maximum(gap_ref[h], jnp.broadcast_to(m_blk - m_old, (STAT_ROWS, tq)))
            betas.append(beta)
            probs.append([p.astype(BF16) for p in ps])
        for h in range(N_HEADS):
            acc = acc_ref[h]
            for c, p in zip(blocks, probs[h]):
                acc = acc + _dot(v_ref[0, h, c], p)
            acc_ref[h] = acc * betas[h]

    def update_two_pass(c):
        s_all = [scores(h, c) for h in range(N_HEADS)]
        alphas, probs = [], []
        for h in range(N_HEADS):
            m_old = m_ref[h, 0:1, :]
            m_new = jnp.maximum(m_old, jnp.max(s_all[h], axis=0, keepdims=True))
            alpha = jnp.exp2(m_old - m_new)
            p = jnp.exp2(s_all[h] - m_new)
            l_new = alpha * l_ref[h, 0:1, :] + jnp.sum(p, axis=0, keepdims=True)
            m_ref[h] = jnp.broadcast_to(m_new, (STAT_ROWS, tq))
            l_ref[h] = jnp.broadcast_to(l_new, (STAT_ROWS, tq))
            alphas.append(alpha)
            probs.append(p.astype(BF16))
        for h in range(N_HEADS):
            acc_ref[h] = alphas[h] * acc_ref[h] + _dot(v_ref[0, h, c], probs[h])

    @pl.when(qi % 2 == 0)
    def _():
        init_with_diagonal()
        gap_ref[...] = jnp.zeros(gap_ref.shape, F32)

    @pl.when(qi % 2 == 1)
    def _():
        init_with_diagonal()
        gap_ref[...] = jnp.zeros(gap_ref.shape, F32)
        update_single_pass([qi - 1])

    def pair(j, _):
        update_single_pass([2 * j, 2 * j + 1])
        return 0

    lax.fori_loop(0, lax.shift_right_logical(qi, 1), pair, 0)

    worst = gap_ref[0]
    for h in range(1, N_HEADS):
        worst = jnp.maximum(worst, gap_ref[h])

    @pl.when(jnp.max(worst) > MAX_SINGLE_PASS_JUMP)
    def _():
        init_with_diagonal()

        def one(c, _):
            update_two_pass(c)
            return 0

        lax.fori_loop(0, qi, one, 0)

    for h in range(N_HEADS):
        o_ref[0, :, h * HEAD_DIM:(h + 1) * HEAD_DIM] = (acc_ref[h] / l_ref[h, 0:1, :]).T.astype(BF16)


def _cast_block_specs(w, n_rows, steps, n_q):
    cols = w.shape[1]
    per_step = n_rows // steps
    blk = per_step if n_rows % steps == 0 and per_step % BF16_SUBLANES == 0 else LANES
    n_blk = n_rows // blk
    assert n_rows % blk == 0 and n_blk <= steps
    index = lambda b, i: (jnp.minimum(b * n_q + i, n_blk - 1), 0)
    return (pl.BlockSpec((blk, cols), index), pl.BlockSpec((blk, cols), index),
            jax.ShapeDtypeStruct((n_rows, cols), BF16))


def _mla_attention(q, k, v, cast_jobs):
    batch, _, seq, _ = k.shape
    assert Q_TILE == KV_TILE
    n_q = seq // Q_TILE
    cast_specs = [_cast_block_specs(w, rows, batch * n_q, n_q) for w, rows in cast_jobs]
    outs = pl.pallas_call(
        functools.partial(_attn_body, len(cast_jobs)),
        grid=(batch, n_q),
        in_specs=[pl.BlockSpec((1, N_HEADS, QK_DIM, Q_TILE), lambda b, i: (b, 0, 0, i)),
                  pl.BlockSpec((1, N_HEADS, seq, QK_DIM), lambda b, i: (b, 0, 0, 0)),
                  pl.BlockSpec((1, N_HEADS, seq // KV_TILE, HEAD_DIM, KV_TILE), lambda b, i: (b, 0, 0, 0, 0))]
                 + [c[0] for c in cast_specs],
        out_specs=[pl.BlockSpec((1, Q_TILE, ATT_WIDTH), lambda b, i: (b, i, 0))] + [c[1] for c in cast_specs],
        out_shape=[jax.ShapeDtypeStruct((batch, seq, ATT_WIDTH), BF16)] + [c[2] for c in cast_specs],
        scratch_shapes=[pltpu.VMEM((N_HEADS, STAT_ROWS, Q_TILE), F32),
                        pltpu.VMEM((N_HEADS, STAT_ROWS, Q_TILE), F32),
                        pltpu.VMEM((N_HEADS, HEAD_DIM, Q_TILE), F32),
                        pltpu.VMEM((N_HEADS, STAT_ROWS, Q_TILE), F32)],
        compiler_params=pltpu.CompilerParams(dimension_semantics=("arbitrary", "arbitrary"),
                                             vmem_limit_bytes=VMEM_LIMIT_BYTES),
        name="mla_attn",
    )(q, k, v, *[w for w, _ in cast_jobs])
    return outs[0], outs[1:]


def _post_body(h_ref, a_ref, mkv_ref, xn_ref, n2_ref, fn_ref, wouta_ref, wmq_ref, wmo_ref, wg_ref, wu_ref, wd_ref,
               out_ref):
    h2 = h_ref[...] + _dot(a_ref[...], wouta_ref[...])

    hn = _rmsnorm(h2, xn_ref[...]).astype(BF16)
    q = (_dot(hn, wmq_ref[...]) * MEM_SCALE).astype(BF16)
    cols = [slice(h * MEM_HEAD_DIM, (h + 1) * MEM_HEAD_DIM) for h in range(MEM_HEADS)]
    scores = [_dot_nt(q[:, c], mkv_ref[0, :, c]) for c in cols]
    probs = []
    for s in scores:
        e = jnp.exp(s - jnp.max(s, axis=-1, keepdims=True))
        probs.append((e / jnp.sum(e, axis=-1, keepdims=True)).astype(BF16))
    heads = [_dot(p, mkv_ref[0, :, D_MODEL + h * MEM_HEAD_DIM:D_MODEL + (h + 1) * MEM_HEAD_DIM]).astype(BF16)
             for h, p in enumerate(probs)]
    h3 = h2 + _dot(jnp.concatenate(heads, axis=1), wmo_ref[...])

    xn = _rmsnorm(h3, n2_ref[...]).astype(BF16)
    h4 = h3 + 0.5 * _swiglu(xn, wg_ref, wu_ref, wd_ref)
    out_ref[...] = _rmsnorm(h4, fn_ref[...])


def _post(h2d, a2d, mkv, xn, n2, fn, wouta, wmq, wmo, wg, wu, wd, seq):
    tm = POST_TILE
    n_tok = h2d.shape[0]
    tiles_per_seq = seq // tm
    weights = (xn, n2, fn, wouta, wmq, wmo, wg, wu, wd)
    return pl.pallas_call(
        _post_body,
        grid=(n_tok // tm,),
        in_specs=[pl.BlockSpec((tm, D_MODEL), lambda i: (i, 0)),
                  pl.BlockSpec((tm, ATT_WIDTH), lambda i: (i, 0)),
                  pl.BlockSpec((1, N_MEM, 2 * D_MODEL), lambda i: (i // tiles_per_seq, 0, 0))]
                 + [_const_spec(w.shape) for w in weights],
        out_specs=pl.BlockSpec((tm, D_MODEL), lambda i: (i, 0)),
        out_shape=jax.ShapeDtypeStruct((n_tok, D_MODEL), F32),
        compiler_params=pltpu.CompilerParams(dimension_semantics=("arbitrary",),
                                             vmem_limit_bytes=VMEM_LIMIT_BYTES),
        name="post",
    )(h2d, a2d, mkv, *weights)


def _pad_rope_cols(w):
    zeros = jnp.zeros(w.shape[:-1] + (ROPE_HALF,), w.dtype)
    return jnp.concatenate([w[..., :ROPE_HALF], zeros, w[..., ROPE_HALF:], zeros], axis=-1)


def _block_diag2(a, b):
    za = jnp.zeros((a.shape[0], b.shape[1]), a.dtype)
    zb = jnp.zeros((b.shape[0], a.shape[1]), a.dtype)
    return jnp.concatenate([jnp.concatenate([a, za], axis=1), jnp.concatenate([zb, b], axis=1)], axis=0)


def _layer(h2d, mem, pos, freqs, batch, seq, ffn1_norm, ffn1_w_gate, ffn1_w_up, ffn1_w_down, mix_norm, w_in,
           q_norm, w_q_up, kv_norm, w_kv_up, pool_w, pool_scale, w_out, xattn_norm, mem_norm,
           w_mq, w_mkv, w_mo, ffn2_norm, ffn2_w_gate, ffn2_w_up, ffn2_w_down, out_norm):
    row = lambda g: g.reshape(1, -1).astype(F32)
    rope_lo = Q_RANK + KV_RANK
    w_in_p = jnp.concatenate([w_in[:, :rope_lo], _pad_rope_cols(w_in[:, rope_lo:rope_lo + ROPE_DIM]),
                              w_in[:, rope_lo + ROPE_DIM:]], axis=1).astype(BF16)
    wq = w_q_up.reshape(Q_RANK, N_HEADS, HEAD_DIM + ROPE_DIM)
    wq_p = jnp.concatenate([wq[..., :HEAD_DIM], _pad_rope_cols(wq[..., HEAD_DIM:])], axis=-1)
    wq_t = wq_p.reshape(Q_RANK, N_HEADS * QK_DIM).T.astype(BF16)
    wkv = w_kv_up.reshape(KV_RANK, N_HEADS, 2 * HEAD_DIM)
    wk = wkv[..., :HEAD_DIM].reshape(KV_RANK, ATT_WIDTH).astype(BF16)
    wv_t = wkv[..., HEAD_DIM:].reshape(KV_RANK, ATT_WIDTH).T.astype(BF16)
    poolw = jnp.stack([_block_diag2(pool_w[0], pool_w[1]), _block_diag2(pool_w[2], pool_w[3])]).astype(BF16)

    mkv = _mem_kv(mem, row(mem_norm), w_mkv.astype(BF16))
    h1p, q, k, v = _pre(h2d, pos, freqs, row(ffn1_norm), row(mix_norm), w_in_p, row(q_norm), wq_t, row(kv_norm),
                        wk, wv_t, poolw, row(pool_scale), ffn1_w_gate, ffn1_w_up, ffn1_w_down, w_out, batch, seq)
    a, post_weights = _mla_attention(q, k, v, [(w_out, ATT_WIDTH), (w_mq, D_MODEL), (w_mo, D_MODEL),
                                               (ffn2_w_gate, D_MODEL), (ffn2_w_up, D_MODEL), (ffn2_w_down, D_FF)])
    return _post(h1p, a.reshape(batch * seq, ATT_WIDTH), mkv, row(xattn_norm), row(ffn2_norm), row(out_norm),
                 *post_weights, seq)


def kernel(x, mem, positions, ffn1_norm, ffn1_w_gate, ffn1_w_up, ffn1_w_down, mix_norm, w_in, q_norm, w_q_up,
           kv_norm, w_kv_up, pool_w, pool_scale, w_out, xattn_norm, mem_norm, w_mq, w_mkv, w_mo, ffn2_norm,
           ffn2_w_gate, ffn2_w_up, ffn2_w_down, final_norm):
    batch, seq, d_model = x.shape
    depth = ffn1_norm.shape[0]
    assert d_model == D_MODEL and depth == 1
    assert seq % TOKEN_TILE == 0 and seq % POST_TILE == 0 and seq % Q_TILE == 0
    freqs = 1.0 / (ROPE_BASE ** (jnp.arange(0, ROPE_DIM, 2, dtype=F32) / ROPE_DIM))
    layer = (ffn1_norm, ffn1_w_gate, ffn1_w_up, ffn1_w_down, mix_norm, w_in, q_norm, w_q_up, kv_norm, w_kv_up,
             pool_w, pool_scale, w_out, xattn_norm, mem_norm, w_mq, w_mkv, w_mo, ffn2_norm, ffn2_w_gate,
             ffn2_w_up, ffn2_w_down)
    out = _layer(x.reshape(batch * seq, d_model), mem, positions.reshape(1, batch * seq),
                 freqs.reshape(ROPE_HALF, 1), batch, seq, *[w[0] for w in layer], final_norm)
    return out.reshape(batch, seq, d_model)
```

```python
import functools

import jax
import jax.numpy as jnp
from jax import lax
from jax.experimental import pallas as pl
from jax.experimental.pallas import tpu as pltpu

F32 = jnp.float32
BF16 = jnp.bfloat16

D_MODEL = 1024
N_HEADS = 4
HEAD_DIM = 128
ROPE_DIM = 64
ROPE_HALF = ROPE_DIM // 2
Q_RANK = 256
KV_RANK = 128
ATT_WIDTH = N_HEADS * HEAD_DIM
POOL_WIDTH = D_MODEL - ATT_WIDTH
POOL_WINDOWS = (2, 4, 8, 16)
POOL_CH = POOL_WIDTH // len(POOL_WINDOWS)
POOL_HALO = 16
D_FF = 2816
N_MEM = 256
MEM_HEADS = 4
MEM_HEAD_DIM = D_MODEL // MEM_HEADS
ROPE_BASE = 10000.0
RMS_EPS = 1e-6
ATT_SCALE = (HEAD_DIM + ROPE_DIM) ** -0.5
LOG2_E = 1.4426950408889634
Q_PRESCALE = ATT_SCALE * LOG2_E
MEM_SCALE = MEM_HEAD_DIM ** -0.5

LANES = 128
BF16_SUBLANES = 16
MXU_DIM = 256
VMEM_LIMIT_BYTES = 56 * 1024 * 1024

TOKEN_TILE = 512
POST_TILE = 1024
FF_CHUNK = MXU_DIM
Q_TILE = 512
KV_TILE = 512
QK_DIM = 2 * HEAD_DIM
CAST_ROWS_WIDE = 128
CAST_ROWS = 256
CAST_SLOTS = 4
STAT_ROWS = 8
MAX_SINGLE_PASS_JUMP = 32.0


def _rmsnorm(x, g):
    ms = jnp.mean(x * x, axis=-1, keepdims=True)
    return (x * lax.rsqrt(ms + RMS_EPS)) * g


def _dot(a, b):
    return jnp.dot(a, b, preferred_element_type=F32)


def _dot_nt(a, b):
    return lax.dot_general(a, b, (((1,), (1,)), ((), ())), preferred_element_type=F32)


def _swiglu(xn, wg_ref, wu_ref, wd_ref):
    n_chunks = D_FF // FF_CHUNK
    cols = lambda c: slice(c * FF_CHUNK, (c + 1) * FF_CHUNK)
    gate_up = lambda c: (_dot(xn, wg_ref[:, cols(c)]), _dot(xn, wu_ref[:, cols(c)]))
    acc = None
    g, u = gate_up(0)
    for c in range(n_chunks):
        nxt = gate_up(c + 1) if c + 1 < n_chunks else None
        a = (g / (1.0 + jnp.exp(-g))) * u
        d = _dot(a.astype(BF16), wd_ref[cols(c), :])
        acc = d if acc is None else acc + d
        if nxt is not None:
            g, u = nxt
    return acc


def _row_jobs(src, src_row0, n_rows, dst, chunk):
    assert n_rows % chunk == 0
    return [(src, src_row0 + r, dst, r) for r in range(0, n_rows, chunk)]


def _cast_weights_into_vmem(jobs, stage, sems):
    slots, rows = stage.shape[0], stage.shape[1]
    ahead = slots - 1

    def copy(j):
        src, r0, _, _ = jobs[j]
        return pltpu.make_async_copy(src.at[pl.ds(r0, rows), :], stage.at[j % slots], sems.at[j % slots])

    for j in range(min(ahead, len(jobs))):
        copy(j).start()
    for j in range(len(jobs)):
        if j + ahead < len(jobs):
            copy(j + ahead).start()
        copy(j).wait()
        _, _, dst, d0 = jobs[j]
        dst[d0:d0 + rows, :] = stage[j % slots].astype(BF16)


def _cast_scratch(wide_rows, narrow_rows):
    return [pltpu.VMEM((CAST_SLOTS, wide_rows, D_FF), F32), pltpu.VMEM((CAST_SLOTS, narrow_rows, D_MODEL), F32),
            pltpu.SemaphoreType.DMA((CAST_SLOTS,)), pltpu.SemaphoreType.DMA((CAST_SLOTS,))]


def _rope(x, cos4, sin4):
    return x * cos4 + pltpu.roll(x, 2 * ROPE_HALF, 1) * sin4


def _pre_body(tiles_per_seq, n_tiles,
              x_ref, pos_ref, freq_ref, n1_ref, mixn_ref, win_ref, qn_ref, wq_ref, kvn_ref, wk_ref, wv_ref,
              poolw_ref, pscale_ref, wg_hbm, wu_hbm, wd_hbm, wout_hbm,
              h_ref, q_ref, k_ref, v_ref,
              ext_ref, h1_ref, z_ref, cos_ref, sin_ref, wg_ref, wu_ref, wd_ref, woutp_ref,
              stage_wide, stage, sem_wide, sem):
    step = pl.program_id(0)
    tm = x_ref.shape[0]

    @pl.when(step == 0)
    def _():
        _cast_weights_into_vmem(_row_jobs(wg_hbm, 0, D_MODEL, wg_ref, CAST_ROWS_WIDE)
                                + _row_jobs(wu_hbm, 0, D_MODEL, wu_ref, CAST_ROWS_WIDE), stage_wide, sem_wide)
        _cast_weights_into_vmem(_row_jobs(wd_hbm, 0, D_FF, wd_ref, CAST_ROWS)
                                + _row_jobs(wout_hbm, ATT_WIDTH, POOL_WIDTH, woutp_ref, CAST_ROWS), stage, sem)
        h1_ref[...] = jnp.zeros(h1_ref.shape, F32)
        z_ref[...] = jnp.zeros(z_ref.shape, F32)
        ext_ref[...] = jnp.zeros(ext_ref.shape, F32)
        cos_ref[...] = jnp.zeros(cos_ref.shape, F32)
        sin_ref[...] = jnp.zeros(sin_ref.shape, F32)

    def ffn_and_project():
        x = x_ref[...]
        xn = _rmsnorm(x, n1_ref[...]).astype(BF16)
        h1 = x + 0.5 * _swiglu(xn, wg_ref, wu_ref, wd_ref)
        un = _rmsnorm(h1, mixn_ref[...]).astype(BF16)
        z = _dot(un, win_ref[...])
        ang = freq_ref[...] * pos_ref[...].astype(F32)
        return h1, z, jnp.cos(ang), jnp.sin(ang)

    def mixer_inputs():
        seq_tile = jnp.maximum(step - 1, 0) % tiles_per_seq

        cos = cos_ref[...]
        sin = sin_ref[...]
        cos4_t = jnp.concatenate([cos, cos, cos, cos], axis=0)
        sin4_t = jnp.concatenate([-sin, -sin, sin, sin], axis=0)

        qn = _rmsnorm(z_ref[:, :Q_RANK], qn_ref[...]).astype(BF16)
        q_t = _dot_nt(wq_ref[...], qn)
        for h in range(N_HEADS):
            base = h * QK_DIM
            q_ref[0, h, :HEAD_DIM, :] = (q_t[base:base + HEAD_DIM] * Q_PRESCALE).astype(BF16)
            pe = q_t[base + HEAD_DIM:base + QK_DIM]
            pe = pe * cos4_t + pltpu.roll(pe, 2 * ROPE_HALF, 0) * sin4_t
            q_ref[0, h, HEAD_DIM:, :] = (pe * Q_PRESCALE).astype(BF16)

        kvn = _rmsnorm(z_ref[:, Q_RANK:Q_RANK + KV_RANK], kvn_ref[...]).astype(BF16)
        k_nope = _dot(kvn, wk_ref[...])
        v_t = _dot_nt(wv_ref[...], kvn)
        k_pe = _rope(z_ref[:, Q_RANK + KV_RANK:Q_RANK + KV_RANK + LANES], cos4_t.T, sin4_t.T).astype(BF16)
        for h in range(N_HEADS):
            k_ref[0, h, :, :HEAD_DIM] = k_nope[:, h * HEAD_DIM:(h + 1) * HEAD_DIM].astype(BF16)
            k_ref[0, h, :, HEAD_DIM:] = k_pe
            v_ref[0, h, 0] = v_t[h * HEAD_DIM:(h + 1) * HEAD_DIM].astype(BF16)

        zp = z_ref[:, D_MODEL - POOL_WIDTH:]
        ext_ref[0:POOL_HALO, :] = jnp.where(seq_tile == 0, 0.0, ext_ref[0:POOL_HALO, :])
        ext_ref[POOL_HALO:, :] = zp
        t1 = (seq_tile * tm + 1 + lax.broadcasted_iota(jnp.int32, (tm, 1), 0)).astype(F32)
        assert POOL_WINDOWS == (2, 4, 8, 16) and POOL_HALO >= sum(POOL_WINDOWS) // 2
        level = ext_ref[...]
        diffs = []
        for g, w in enumerate(POOL_WINDOWS):
            level = level + pltpu.roll(level, w // 2, 0)
            mean = level[POOL_HALO:, :POOL_CH] / jnp.minimum(t1, float(w))
            diffs.append(mean - zp[:, g * POOL_CH:(g + 1) * POOL_CH])
            if g + 1 < len(POOL_WINDOWS):
                level = level[:, POOL_CH:]
        ext_ref[0:POOL_HALO, :] = ext_ref[tm:tm + POOL_HALO, :]

        y01 = _dot(jnp.concatenate(diffs[:2], axis=1).astype(BF16), poolw_ref[0])
        y23 = _dot(jnp.concatenate(diffs[2:], axis=1).astype(BF16), poolw_ref[1])
        p = jnp.concatenate([y01, y23], axis=1) * pscale_ref[...]
        h_ref[...] = h1_ref[...] + _dot(p.astype(BF16), woutp_ref[...])

    @pl.when(step < n_tiles)
    def _():
        mixer_inputs()
        h1, z, cos, sin = ffn_and_project()
        h1_ref[...] = h1
        z_ref[...] = z
        cos_ref[...] = cos
        sin_ref[...] = sin

    @pl.when(step == n_tiles)
    def _():
        mixer_inputs()


def _const_spec(shape):
    nd = len(shape)
    return pl.BlockSpec(shape, lambda *_: (0,) * nd, pipeline_mode=pl.Buffered(1))


def _pre(x2d, pos, freqs, n1, mixn, win, qn, wq, kvn, wk, wv, poolw, pscale, wg, wu, wd, wout, batch, seq):
    tm = TOKEN_TILE
    assert tm == KV_TILE
    n_tok = x2d.shape[0]
    tiles_per_seq = seq // tm
    n_tiles = n_tok // tm
    lag = lambda i: jnp.maximum(i - 1, 0)
    bt = lambda i: (lag(i) // tiles_per_seq, lag(i) % tiles_per_seq)
    x_spec = pl.BlockSpec((tm, D_MODEL), lambda i: (jnp.minimum(i, n_tiles - 1), 0))
    h_spec = pl.BlockSpec((tm, D_MODEL), lambda i: (lag(i), 0))
    pos_spec = pl.BlockSpec((1, tm), lambda i: (0, jnp.minimum(i, n_tiles - 1)))
    q_spec = pl.BlockSpec((1, N_HEADS, QK_DIM, tm), lambda i: (bt(i)[0], 0, 0, bt(i)[1]))
    k_spec = pl.BlockSpec((1, N_HEADS, tm, QK_DIM), lambda i: (bt(i)[0], 0, bt(i)[1], 0))
    v_spec = pl.BlockSpec((1, N_HEADS, 1, HEAD_DIM, tm), lambda i: (bt(i)[0], 0, bt(i)[1], 0, 0))
    weights = (n1, mixn, win, qn, wq, kvn, wk, wv, poolw, pscale)
    hbm_weights = (wg, wu, wd, wout)
    return pl.pallas_call(
        functools.partial(_pre_body, tiles_per_seq, n_tiles),
        grid=(n_tiles + 1,),
        in_specs=[x_spec, pos_spec, _const_spec(freqs.shape)] + [_const_spec(w.shape) for w in weights]
                 + [pl.BlockSpec(memory_space=pl.ANY)] * len(hbm_weights),
        out_specs=[h_spec, q_spec, k_spec, v_spec],
        out_shape=[jax.ShapeDtypeStruct((n_tok, D_MODEL), F32),
                   jax.ShapeDtypeStruct((batch, N_HEADS, QK_DIM, seq), BF16),
                   jax.ShapeDtypeStruct((batch, N_HEADS, seq, QK_DIM), BF16),
                   jax.ShapeDtypeStruct((batch, N_HEADS, tiles_per_seq, HEAD_DIM, tm), BF16)],
        scratch_shapes=[pltpu.VMEM((tm + POOL_HALO, POOL_WIDTH), F32),
                        pltpu.VMEM((tm, D_MODEL), F32), pltpu.VMEM((tm, D_MODEL), F32),
                        pltpu.VMEM((ROPE_HALF, tm), F32), pltpu.VMEM((ROPE_HALF, tm), F32),
                        pltpu.VMEM((D_MODEL, D_FF), BF16), pltpu.VMEM((D_MODEL, D_FF), BF16),
                        pltpu.VMEM((D_FF, D_MODEL), BF16), pltpu.VMEM((POOL_WIDTH, D_MODEL), BF16)]
                       + _cast_scratch(CAST_ROWS_WIDE, CAST_ROWS),
        compiler_params=pltpu.CompilerParams(dimension_semantics=("arbitrary",),
                                             vmem_limit_bytes=VMEM_LIMIT_BYTES),
        name="pre",
    )(x2d, pos, freqs, *weights, *hbm_weights)


def _attn_body(n_cast, q_ref, k_ref, v_ref, mem_ref, memn_ref, wmkv_ref, *refs):
    cast_in, o_ref, mkv_ref = refs[:n_cast], refs[n_cast], refs[n_cast + 1]
    cast_out = refs[n_cast + 2:2 * n_cast + 2]
    m_ref, l_ref, acc_ref, gap_ref = refs[2 * n_cast + 2:]
    for src, dst in zip(cast_in, cast_out):
        dst[...] = src[...].astype(BF16)

    qi = pl.program_id(1)

    @pl.when(qi == 0)
    def _():
        mn = _rmsnorm(mem_ref[0], memn_ref[...]).astype(BF16)
        mkv_ref[0] = _dot(mn, wmkv_ref[...]).astype(BF16)

    tq = q_ref.shape[3]
    tk = KV_TILE
    half = tk // 2

    def scores(h, c):
        return _dot(k_ref[0, h, pl.ds(pl.multiple_of(c * tk, tk), tk), :], q_ref[0, h])


    def init_with_diagonal():
        causal = (lax.broadcasted_iota(jnp.int32, (half, half), 0)
                  <= lax.broadcasted_iota(jnp.int32, (half, half), 1))
        start = pl.multiple_of(qi * tk, tk)
        s_lo = [_dot(k_ref[0, h, pl.ds(start, half), :], q_ref[0, h]) for h in range(N_HEADS)]
        s_hi = [_dot(k_ref[0, h, pl.ds(start + half, half), :], q_ref[0, h, :, half:]) for h in range(N_HEADS)]
        probs = []
        for h in range(N_HEADS):
            s_ll = jnp.where(causal, s_lo[h][:, :half], -jnp.inf)
            s_lu = s_lo[h][:, half:]
            s_uu = jnp.where(causal, s_hi[h], -jnp.inf)
            m_l = jnp.max(s_ll, axis=0, keepdims=True)
            m_u = jnp.maximum(jnp.max(s_lu, axis=0, keepdims=True), jnp.max(s_uu, axis=0, keepdims=True))
            p_ll = jnp.exp2(s_ll - m_l)
            p_lu = jnp.exp2(s_lu - m_u)
            p_uu = jnp.exp2(s_uu - m_u)
            l_l = jnp.sum(p_ll, axis=0, keepdims=True)
            l_u = jnp.sum(p_lu, axis=0, keepdims=True) + jnp.sum(p_uu, axis=0, keepdims=True)
            m_ref[h, :, :half] = jnp.broadcast_to(m_l, (STAT_ROWS, half))
            m_ref[h, :, half:] = jnp.broadcast_to(m_u, (STAT_ROWS, half))
            l_ref[h, :, :half] = jnp.broadcast_to(l_l, (STAT_ROWS, half))
            l_ref[h, :, half:] = jnp.broadcast_to(l_u, (STAT_ROWS, half))
            probs.append((jnp.concatenate([p_ll, p_lu], axis=1).astype(BF16), p_uu.astype(BF16)))
        for h in range(N_HEADS):
            v_t = v_ref[0, h, qi]
            acc_l = _dot(v_t[:, :half], probs[h][0])
            acc_u = _dot(v_t[:, half:], probs[h][1])
            acc_ref[h, :, :half] = acc_l[:, :half]
            acc_ref[h, :, half:] = acc_l[:, half:] + acc_u

    def update_single_pass(blocks):
        s_all = [[scores(h, c) for c in blocks] for h in range(N_HEADS)]
        betas, probs = [], []
        for h in range(N_HEADS):
            m_old = m_ref[h, 0:1, :]
            ps = [jnp.exp2(s - m_old) for s in s_all[h]]
            m_blk = jnp.max(s_all[h][0], axis=0, keepdims=True)
            for s in s_all[h][1:]:
                m_blk = jnp.maximum(m_blk, jnp.max(s, axis=0, keepdims=True))
            m_new = jnp.maximum(m_old, m_blk)
            beta = jnp.exp2(m_old - m_new)
            l_new = l_ref[h, 0:1, :]
            for p in ps:
                l_new = l_new + jnp.sum(p, axis=0, keepdims=True)
            m_ref[h] = jnp.broadcast_to(m_new, (STAT_ROWS, tq))
            l_ref[h] = jnp.broadcast_to(l_new * beta, (STAT_ROWS, tq))
            gap_ref[h] = jnp.maximum(gap_ref[h], jnp.broadcast_to(m_blk - m_old, (STAT_ROWS, tq)))
            betas.append(beta)
            probs.append([p.astype(BF16) for p in ps])
        for h in range(N_HEADS):
            acc = acc_ref[h]
            for c, p in zip(blocks, probs[h]):
                acc = acc + _dot(v_ref[0, h, c], p)
            acc_ref[h] = acc * betas[h]

    def update_two_pass(c):
        s_all = [scores(h, c) for h in range(N_HEADS)]
        alphas, probs = [], []
        for h in range(N_HEADS):
            m_old = m_ref[h, 0:1, :]
            m_new = jnp.maximum(m_old, jnp.max(s_all[h], axis=0, keepdims=True))
            alpha = jnp.exp2(m_old - m_new)
            p = jnp.exp2(s_all[h] - m_new)
            l_new = alpha * l_ref[h, 0:1, :] + jnp.sum(p, axis=0, keepdims=True)
            m_ref[h] = jnp.broadcast_to(m_new, (STAT_ROWS, tq))
            l_ref[h] = jnp.broadcast_to(l_new, (STAT_ROWS, tq))
            alphas.append(alpha)
            probs.append(p.astype(BF16))
        for h in range(N_HEADS):
            acc_ref[h] = alphas[h] * acc_ref[h] + _dot(v_ref[0, h, c], probs[h])

    @pl.when(qi % 2 == 0)
    def _():
        init_with_diagonal()
        gap_ref[...] = jnp.zeros(gap_ref.shape, F32)

    @pl.when(qi % 2 == 1)
    def _():
        init_with_diagonal()
        gap_ref[...] = jnp.zeros(gap_ref.shape, F32)
        update_single_pass([qi - 1])

    def pair(j, _):
        update_single_pass([2 * j, 2 * j + 1])
        return 0

    lax.fori_loop(0, lax.shift_right_logical(qi, 1), pair, 0)

    worst = gap_ref[0]
    for h in range(1, N_HEADS):
        worst = jnp.maximum(worst, gap_ref[h])

    @pl.when(jnp.max(worst) > MAX_SINGLE_PASS_JUMP)
    def _():
        init_with_diagonal()

        def one(c, _):
            update_two_pass(c)
            return 0

        lax.fori_loop(0, qi, one, 0)

    for h in range(N_HEADS):
        o_ref[0, :, h * HEAD_DIM:(h + 1) * HEAD_DIM] = (acc_ref[h] / l_ref[h, 0:1, :]).T.astype(BF16)


def _cast_block_specs(w, n_rows, steps, n_q):
    cols = w.shape[1]
    per_step = n_rows // steps
    blk = per_step if n_rows % steps == 0 and per_step % BF16_SUBLANES == 0 else LANES
    n_blk = n_rows // blk
    assert n_rows % blk == 0 and n_blk <= steps
    index = lambda b, i: (jnp.minimum(b * n_q + i, n_blk - 1), 0)
    return (pl.BlockSpec((blk, cols), index), pl.BlockSpec((blk, cols), index),
            jax.ShapeDtypeStruct((n_rows, cols), BF16))


def _mla_attention(q, k, v, mem, mem_norm, w_mkv, cast_jobs):
    batch, _, seq, _ = k.shape
    assert Q_TILE == KV_TILE
    n_q = seq // Q_TILE
    cast_specs = [_cast_block_specs(w, rows, batch * n_q, n_q) for w, rows in cast_jobs]
    outs = pl.pallas_call(
        functools.partial(_attn_body, len(cast_jobs)),
        grid=(batch, n_q),
        in_specs=[pl.BlockSpec((1, N_HEADS, QK_DIM, Q_TILE), lambda b, i: (b, 0, 0, i)),
                  pl.BlockSpec((1, N_HEADS, seq, QK_DIM), lambda b, i: (b, 0, 0, 0)),
                  pl.BlockSpec((1, N_HEADS, seq // KV_TILE, HEAD_DIM, KV_TILE), lambda b, i: (b, 0, 0, 0, 0)),
                  pl.BlockSpec((1, N_MEM, D_MODEL), lambda b, i: (b, 0, 0)),
                  _const_spec(mem_norm.shape), _const_spec(w_mkv.shape)]
                 + [c[0] for c in cast_specs],
        out_specs=[pl.BlockSpec((1, Q_TILE, ATT_WIDTH), lambda b, i: (b, i, 0)),
                   pl.BlockSpec((1, N_MEM, 2 * D_MODEL), lambda b, i: (b, 0, 0))] + [c[1] for c in cast_specs],
        out_shape=[jax.ShapeDtypeStruct((batch, seq, ATT_WIDTH), BF16),
                   jax.ShapeDtypeStruct((batch, N_MEM, 2 * D_MODEL), BF16)] + [c[2] for c in cast_specs],
        scratch_shapes=[pltpu.VMEM((N_HEADS, STAT_ROWS, Q_TILE), F32),
                        pltpu.VMEM((N_HEADS, STAT_ROWS, Q_TILE), F32),
                        pltpu.VMEM((N_HEADS, HEAD_DIM, Q_TILE), F32),
                        pltpu.VMEM((N_HEADS, STAT_ROWS, Q_TILE), F32)],
        compiler_params=pltpu.CompilerParams(dimension_semantics=("arbitrary", "arbitrary"),
                                             vmem_limit_bytes=VMEM_LIMIT_BYTES),
        name="mla_attn",
    )(q, k, v, mem, mem_norm, w_mkv, *[w for w, _ in cast_jobs])
    return outs[0], outs[1], outs[2:]


def _post_body(h_ref, a_ref, mkv_ref, xn_ref, n2_ref, fn_ref, wouta_ref, wmq_ref, wmo_ref, wg_ref, wu_ref, wd_ref,
               out_ref):
    h2 = h_ref[...] + _dot(a_ref[...], wouta_ref[...])

    hn = _rmsnorm(h2, xn_ref[...]).astype(BF16)
    q = (_dot(hn, wmq_ref[...]) * MEM_SCALE).astype(BF16)
    cols = [slice(h * MEM_HEAD_DIM, (h + 1) * MEM_HEAD_DIM) for h in range(MEM_HEADS)]
    scores = [_dot_nt(q[:, c], mkv_ref[0, :, c]) for c in cols]
    probs = []
    for s in scores:
        e = jnp.exp(s - jnp.max(s, axis=-1, keepdims=True))
        probs.append((e / jnp.sum(e, axis=-1, keepdims=True)).astype(BF16))
    heads = [_dot(p, mkv_ref[0, :, D_MODEL + h * MEM_HEAD_DIM:D_MODEL + (h + 1) * MEM_HEAD_DIM]).astype(BF16)
             for h, p in enumerate(probs)]
    h3 = h2 + _dot(jnp.concatenate(heads, axis=1), wmo_ref[...])

    xn = _rmsnorm(h3, n2_ref[...]).astype(BF16)
    h4 = h3 + 0.5 * _swiglu(xn, wg_ref, wu_ref, wd_ref)
    out_ref[...] = _rmsnorm(h4, fn_ref[...])


def _post(h2d, a2d, mkv, xn, n2, fn, wouta, wmq, wmo, wg, wu, wd, seq):
    tm = POST_TILE
    n_tok = h2d.shape[0]
    tiles_per_seq = seq // tm
    weights = (xn, n2, fn, wouta, wmq, wmo, wg, wu, wd)
    return pl.pallas_call(
        _post_body,
        grid=(n_tok // tm,),
        in_specs=[pl.BlockSpec((tm, D_MODEL), lambda i: (i, 0)),
                  pl.BlockSpec((tm, ATT_WIDTH), lambda i: (i, 0)),
                  pl.BlockSpec((1, N_MEM, 2 * D_MODEL), lambda i: (i // tiles_per_seq, 0, 0))]
                 + [_const_spec(w.shape) for w in weights],
        out_specs=pl.BlockSpec((tm, D_MODEL), lambda i: (i, 0)),
        out_shape=jax.ShapeDtypeStruct((n_tok, D_MODEL), F32),
        compiler_params=pltpu.CompilerParams(dimension_semantics=("arbitrary",),
                                             vmem_limit_bytes=VMEM_LIMIT_BYTES),
        name="post",
    )(h2d, a2d, mkv, *weights)


def _pad_rope_cols(w):
    zeros = jnp.zeros(w.shape[:-1] + (ROPE_HALF,), w.dtype)
    return jnp.concatenate([w[..., :ROPE_HALF], zeros, w[..., ROPE_HALF:], zeros], axis=-1)


def _block_diag2(a, b):
    za = jnp.zeros((a.shape[0], b.shape[1]), a.dtype)
    zb = jnp.zeros((b.shape[0], a.shape[1]), a.dtype)
    return jnp.concatenate([jnp.concatenate([a, za], axis=1), jnp.concatenate([zb, b], axis=1)], axis=0)


def _layer(h2d, mem, pos, freqs, batch, seq, ffn1_norm, ffn1_w_gate, ffn1_w_up, ffn1_w_down, mix_norm, w_in,
           q_norm, w_q_up, kv_norm, w_kv_up, pool_w, pool_scale, w_out, xattn_norm, mem_norm,
           w_mq, w_mkv, w_mo, ffn2_norm, ffn2_w_gate, ffn2_w_up, ffn2_w_down, out_norm):
    row = lambda g: g.reshape(1, -1).astype(F32)
    rope_lo = Q_RANK + KV_RANK
    w_in_p = jnp.concatenate([w_in[:, :rope_lo], _pad_rope_cols(w_in[:, rope_lo:rope_lo + ROPE_DIM]),
                              w_in[:, rope_lo + ROPE_DIM:]], axis=1).astype(BF16)
    wq = w_q_up.reshape(Q_RANK, N_HEADS, HEAD_DIM + ROPE_DIM)
    wq_p = jnp.concatenate([wq[..., :HEAD_DIM], _pad_rope_cols(wq[..., HEAD_DIM:])], axis=-1)
    wq_t = wq_p.reshape(Q_RANK, N_HEADS * QK_DIM).T.astype(BF16)
    wkv = w_kv_up.reshape(KV_RANK, N_HEADS, 2 * HEAD_DIM)
    wk = wkv[..., :HEAD_DIM].reshape(KV_RANK, ATT_WIDTH).astype(BF16)
    wv_t = wkv[..., HEAD_DIM:].reshape(KV_RANK, ATT_WIDTH).T.astype(BF16)
    poolw = jnp.stack([_block_diag2(pool_w[0], pool_w[1]), _block_diag2(pool_w[2], pool_w[3])]).astype(BF16)

    h1p, q, k, v = _pre(h2d, pos, freqs, row(ffn1_norm), row(mix_norm), w_in_p, row(q_norm), wq_t, row(kv_norm),
                        wk, wv_t, poolw, row(pool_scale), ffn1_w_gate, ffn1_w_up, ffn1_w_down, w_out, batch, seq)
    a, mkv, post_weights = _mla_attention(
        q, k, v, mem, row(mem_norm), w_mkv.astype(BF16),
        [(w_out, ATT_WIDTH), (w_mq, D_MODEL), (w_mo, D_MODEL),
         (ffn2_w_gate, D_MODEL), (ffn2_w_up, D_MODEL), (ffn2_w_down, D_FF)])
    return _post(h1p, a.reshape(batch * seq, ATT_WIDTH), mkv, row(xattn_norm), row(ffn2_norm), row(out_norm),
                 *post_weights, seq)


def kernel(x, mem, positions, ffn1_norm, ffn1_w_gate, ffn1_w_up, ffn1_w_down, mix_norm, w_in, q_norm, w_q_up,
           kv_norm, w_kv_up, pool_w, pool_scale, w_out, xattn_norm, mem_norm, w_mq, w_mkv, w_mo, ffn2_norm,
           ffn2_w_gate, ffn2_w_up, ffn2_w_down, final_norm):
    batch, seq, d_model = x.shape
    depth = ffn1_norm.shape[0]
    assert d_model == D_MODEL and depth == 1
    assert seq % TOKEN_TILE == 0 and seq % POST_TILE == 0 and seq % Q_TILE == 0
    freqs = 1.0 / (ROPE_BASE ** (jnp.arange(0, ROPE_DIM, 2, dtype=F32) / ROPE_DIM))
    layer = (ffn1_norm, ffn1_w_gate, ffn1_w_up, ffn1_w_down, mix_norm, w_in, q_norm, w_q_up, kv_norm, w_kv_up,
             pool_w, pool_scale, w_out, xattn_norm, mem_norm, w_mq, w_mkv, w_mo, ffn2_norm, ffn2_w_gate,
             ffn2_w_up, ffn2_w_down)
    out = _layer(x.reshape(batch * seq, d_model), mem, positions.reshape(1, batch * seq),
                 freqs.reshape(ROPE_HALF, 1), batch, seq, *[w[0] for w in layer], final_norm)
    return out.reshape(batch, seq, d_model)
```

```python
import functools

import jax
import jax.numpy as jnp
from jax import lax
from jax.experimental import pallas as pl
from jax.experimental.pallas import tpu as pltpu

F32 = jnp.float32
BF16 = jnp.bfloat16

D_MODEL = 1024
N_HEADS = 4
HEAD_DIM = 128
ROPE_DIM = 64
ROPE_HALF = ROPE_DIM // 2
Q_RANK = 256
KV_RANK = 128
ATT_WIDTH = N_HEADS * HEAD_DIM
POOL_WIDTH = D_MODEL - ATT_WIDTH
POOL_WINDOWS = (2, 4, 8, 16)
POOL_CH = POOL_WIDTH // len(POOL_WINDOWS)
POOL_HALO = 16
D_FF = 2816
N_MEM = 256
MEM_HEADS = 4
MEM_HEAD_DIM = D_MODEL // MEM_HEADS
ROPE_BASE = 10000.0
RMS_EPS = 1e-6
ATT_SCALE = (HEAD_DIM + ROPE_DIM) ** -0.5
LOG2_E = 1.4426950408889634
Q_PRESCALE = ATT_SCALE * LOG2_E
MEM_SCALE = MEM_HEAD_DIM ** -0.5

LANES = 128
BF16_SUBLANES = 16
MXU_DIM = 256
VMEM_LIMIT_BYTES = 56 * 1024 * 1024

TOKEN_TILE = 512
POST_TILE = 1024
FF_CHUNK = MXU_DIM
Q_TILE = 512
KV_TILE = 512
QK_DIM = 2 * HEAD_DIM
CAST_ROWS_WIDE = 128
CAST_ROWS = 256
CAST_SLOTS = 4
STAT_ROWS = 8
MAX_SINGLE_PASS_JUMP = 32.0


def _rmsnorm(x, g):
    ms = jnp.mean(x * x, axis=-1, keepdims=True)
    return (x * lax.rsqrt(ms + RMS_EPS)) * g


def _dot(a, b):
    return jnp.dot(a, b, preferred_element_type=F32)


def _dot_nt(a, b):
    return lax.dot_general(a, b, (((1,), (1,)), ((), ())), preferred_element_type=F32)


def _swiglu(xn, wg_ref, wu_ref, wd_ref):
    bounds = list(range(0, D_FF, FF_CHUNK)) + [D_FF]
    n_chunks = len(bounds) - 1
    cols = lambda c: slice(bounds[c], bounds[c + 1])
    gate_up = lambda c: (_dot(xn, wg_ref[:, cols(c)]), _dot(xn, wu_ref[:, cols(c)]))
    acc = None
    g, u = gate_up(0)
    for c in range(n_chunks):
        nxt = gate_up(c + 1) if c + 1 < n_chunks else None
        a = (g / (1.0 + jnp.exp(-g))) * u
        d = _dot(a.astype(BF16), wd_ref[cols(c), :])
        acc = d if acc is None else acc + d
        if nxt is not None:
            g, u = nxt
    return acc


def _row_jobs(src, src_row0, n_rows, dst, chunk):
    assert n_rows % chunk == 0
    return [(src, src_row0 + r, dst, r) for r in range(0, n_rows, chunk)]


def _cast_weights_into_vmem(jobs, stage, sems):
    slots, rows = stage.shape[0], stage.shape[1]
    ahead = slots - 1

    def copy(j):
        src, r0, _, _ = jobs[j]
        return pltpu.make_async_copy(src.at[pl.ds(r0, rows), :], stage.at[j % slots], sems.at[j % slots])

    for j in range(min(ahead, len(jobs))):
        copy(j).start()
    for j in range(len(jobs)):
        if j + ahead < len(jobs):
            copy(j + ahead).start()
        copy(j).wait()
        _, _, dst, d0 = jobs[j]
        dst[d0:d0 + rows, :] = stage[j % slots].astype(BF16)


def _cast_scratch(wide_rows, narrow_rows):
    return [pltpu.VMEM((CAST_SLOTS, wide_rows, D_FF), F32), pltpu.VMEM((CAST_SLOTS, narrow_rows, D_MODEL), F32),
            pltpu.SemaphoreType.DMA((CAST_SLOTS,)), pltpu.SemaphoreType.DMA((CAST_SLOTS,))]


def _rope(x, cos4, sin4):
    return x * cos4 + pltpu.roll(x, 2 * ROPE_HALF, 1) * sin4


def _pre_body(tiles_per_seq, n_tiles,
              x_ref, pos_ref, freq_ref, n1_ref, mixn_ref, win_ref, qn_ref, wq_ref, kvn_ref, wk_ref, wv_ref,
              poolw_ref, pscale_ref, wmkv_f32_ref, wg_hbm, wu_hbm, wd_hbm, wout_hbm,
              h_ref, q_ref, k_ref, v_ref, wmkv_bf16_ref,
              ext_ref, h1_ref, z_ref, cos_ref, sin_ref, wg_ref, wu_ref, wd_ref, woutp_ref,
              stage_wide, stage, sem_wide, sem):
    step = pl.program_id(0)
    tm = x_ref.shape[0]
    wmkv_bf16_ref[...] = wmkv_f32_ref[...].astype(BF16)

    @pl.when(step == 0)
    def _():
        _cast_weights_into_vmem(_row_jobs(wg_hbm, 0, D_MODEL, wg_ref, CAST_ROWS_WIDE)
                                + _row_jobs(wu_hbm, 0, D_MODEL, wu_ref, CAST_ROWS_WIDE), stage_wide, sem_wide)
        _cast_weights_into_vmem(_row_jobs(wd_hbm, 0, D_FF, wd_ref, CAST_ROWS)
                                + _row_jobs(wout_hbm, ATT_WIDTH, POOL_WIDTH, woutp_ref, CAST_ROWS), stage, sem)
        h1_ref[...] = jnp.zeros(h1_ref.shape, F32)
        z_ref[...] = jnp.zeros(z_ref.shape, F32)
        ext_ref[...] = jnp.zeros(ext_ref.shape, F32)
        cos_ref[...] = jnp.zeros(cos_ref.shape, F32)
        sin_ref[...] = jnp.zeros(sin_ref.shape, F32)

    def ffn_and_project():
        x = x_ref[...]
        xn = _rmsnorm(x, n1_ref[...]).astype(BF16)
        h1 = x + 0.5 * _swiglu(xn, wg_ref, wu_ref, wd_ref)
        un = _rmsnorm(h1, mixn_ref[...]).astype(BF16)
        z = _dot(un, win_ref[...])
        ang = freq_ref[...] * pos_ref[...].astype(F32)
        return h1, z, jnp.cos(ang), jnp.sin(ang)

    def mixer_inputs():
        seq_tile = jnp.maximum(step - 1, 0) % tiles_per_seq

        cos = cos_ref[...]
        sin = sin_ref[...]
        cos4_t = jnp.concatenate([cos, cos, cos, cos], axis=0)
        sin4_t = jnp.concatenate([-sin, -sin, sin, sin], axis=0)

        qn = _rmsnorm(z_ref[:, :Q_RANK], qn_ref[...]).astype(BF16)
        q_t = _dot_nt(wq_ref[...], qn)
        for h in range(N_HEADS):
            base = h * QK_DIM
            q_ref[0, h, :HEAD_DIM, :] = (q_t[base:base + HEAD_DIM] * Q_PRESCALE).astype(BF16)
            pe = q_t[base + HEAD_DIM:base + QK_DIM]
            pe = pe * cos4_t + pltpu.roll(pe, 2 * ROPE_HALF, 0) * sin4_t
            q_ref[0, h, HEAD_DIM:, :] = (pe * Q_PRESCALE).astype(BF16)

        kvn = _rmsnorm(z_ref[:, Q_RANK:Q_RANK + KV_RANK], kvn_ref[...]).astype(BF16)
        k_nope = _dot(kvn, wk_ref[...])
        v_t = _dot_nt(wv_ref[...], kvn)
        k_pe = _rope(z_ref[:, Q_RANK + KV_RANK:Q_RANK + KV_RANK + LANES], cos4_t.T, sin4_t.T).astype(BF16)
        for h in range(N_HEADS):
            k_ref[0, h, :, :HEAD_DIM] = k_nope[:, h * HEAD_DIM:(h + 1) * HEAD_DIM].astype(BF16)
            k_ref[0, h, :, HEAD_DIM:] = k_pe
            v_ref[0, h, 0] = v_t[h * HEAD_DIM:(h + 1) * HEAD_DIM].astype(BF16)

        zp = z_ref[:, D_MODEL - POOL_WIDTH:]
        ext_ref[0:POOL_HALO, :] = jnp.where(seq_tile == 0, 0.0, ext_ref[0:POOL_HALO, :])
        ext_ref[POOL_HALO:, :] = zp
        t1 = (seq_tile * tm + 1 + lax.broadcasted_iota(jnp.int32, (tm, 1), 0)).astype(F32)
        assert POOL_WINDOWS == (2, 4, 8, 16) and POOL_HALO >= sum(POOL_WINDOWS) // 2
        level = ext_ref[...]
        diffs = []
        for g, w in enumerate(POOL_WINDOWS):
            level = level + pltpu.roll(level, w // 2, 0)
            mean = level[POOL_HALO:, :POOL_CH] / jnp.minimum(t1, float(w))
            diffs.append(mean - zp[:, g * POOL_CH:(g + 1) * POOL_CH])
            if g + 1 < len(POOL_WINDOWS):
                level = level[:, POOL_CH:]
        ext_ref[0:POOL_HALO, :] = ext_ref[tm:tm + POOL_HALO, :]

        y01 = _dot(jnp.concatenate(diffs[:2], axis=1).astype(BF16), poolw_ref[0])
        y23 = _dot(jnp.concatenate(diffs[2:], axis=1).astype(BF16), poolw_ref[1])
        p = jnp.concatenate([y01, y23], axis=1) * pscale_ref[...]
        h_ref[...] = h1_ref[...] + _dot(p.astype(BF16), woutp_ref[...])

    @pl.when(step < n_tiles)
    def _():
        mixer_inputs()
        h1, z, cos, sin = ffn_and_project()
        h1_ref[...] = h1
        z_ref[...] = z
        cos_ref[...] = cos
        sin_ref[...] = sin

    @pl.when(step == n_tiles)
    def _():
        mixer_inputs()


def _const_spec(shape):
    nd = len(shape)
    return pl.BlockSpec(shape, lambda *_: (0,) * nd, pipeline_mode=pl.Buffered(1))


def _pre(x2d, pos, freqs, n1, mixn, win, qn, wq, kvn, wk, wv, poolw, pscale, wmkv, wg, wu, wd, wout, batch, seq):
    tm = TOKEN_TILE
    assert tm == KV_TILE
    n_tok = x2d.shape[0]
    tiles_per_seq = seq // tm
    n_tiles = n_tok // tm
    mkv_rows = wmkv.shape[0] // n_tiles
    assert wmkv.shape[0] % n_tiles == 0 and mkv_rows % BF16_SUBLANES == 0
    wmkv_spec = pl.BlockSpec((mkv_rows, wmkv.shape[1]), lambda i: (jnp.minimum(i, n_tiles - 1), 0))
    lag = lambda i: jnp.maximum(i - 1, 0)
    bt = lambda i: (lag(i) // tiles_per_seq, lag(i) % tiles_per_seq)
    x_spec = pl.BlockSpec((tm, D_MODEL), lambda i: (jnp.minimum(i, n_tiles - 1), 0))
    h_spec = pl.BlockSpec((tm, D_MODEL), lambda i: (lag(i), 0))
    pos_spec = pl.BlockSpec((1, tm), lambda i: (0, jnp.minimum(i, n_tiles - 1)))
    q_spec = pl.BlockSpec((1, N_HEADS, QK_DIM, tm), lambda i: (bt(i)[0], 0, 0, bt(i)[1]))
    k_spec = pl.BlockSpec((1, N_HEADS, tm, QK_DIM), lambda i: (bt(i)[0], 0, bt(i)[1], 0))
    v_spec = pl.BlockSpec((1, N_HEADS, 1, HEAD_DIM, tm), lambda i: (bt(i)[0], 0, bt(i)[1], 0, 0))
    weights = (n1, mixn, win, qn, wq, kvn, wk, wv, poolw, pscale)
    hbm_weights = (wg, wu, wd, wout)
    return pl.pallas_call(
        functools.partial(_pre_body, tiles_per_seq, n_tiles),
        grid=(n_tiles + 1,),
        in_specs=[x_spec, pos_spec, _const_spec(freqs.shape)] + [_const_spec(w.shape) for w in weights]
                 + [wmkv_spec] + [pl.BlockSpec(memory_space=pl.ANY)] * len(hbm_weights),
        out_specs=[h_spec, q_spec, k_spec, v_spec, wmkv_spec],
        out_shape=[jax.ShapeDtypeStruct((n_tok, D_MODEL), F32),
                   jax.ShapeDtypeStruct((batch, N_HEADS, QK_DIM, seq), BF16),
                   jax.ShapeDtypeStruct((batch, N_HEADS, seq, QK_DIM), BF16),
                   jax.ShapeDtypeStruct((batch, N_HEADS, tiles_per_seq, HEAD_DIM, tm), BF16),
                   jax.ShapeDtypeStruct(wmkv.shape, BF16)],
        scratch_shapes=[pltpu.VMEM((tm + POOL_HALO, POOL_WIDTH), F32),
                        pltpu.VMEM((tm, D_MODEL), F32), pltpu.VMEM((tm, D_MODEL), F32),
                        pltpu.VMEM((ROPE_HALF, tm), F32), pltpu.VMEM((ROPE_HALF, tm), F32),
                        pltpu.VMEM((D_MODEL, D_FF), BF16), pltpu.VMEM((D_MODEL, D_FF), BF16),
                        pltpu.VMEM((D_FF, D_MODEL), BF16), pltpu.VMEM((POOL_WIDTH, D_MODEL), BF16)]
                       + _cast_scratch(CAST_ROWS_WIDE, CAST_ROWS),
        compiler_params=pltpu.CompilerParams(dimension_semantics=("arbitrary",),
                                             vmem_limit_bytes=VMEM_LIMIT_BYTES),
        name="pre",
    )(x2d, pos, freqs, *weights, wmkv, *hbm_weights)


def _attn_body(n_cast, q_ref, k_ref, v_ref, mem_ref, memn_ref, wmkv_ref, *refs):
    cast_in, o_ref, mkv_ref = refs[:n_cast], refs[n_cast], refs[n_cast + 1]
    cast_out = refs[n_cast + 2:2 * n_cast + 2]
    m_ref, l_ref, acc_ref, gap_ref = refs[2 * n_cast + 2:]
    for src, dst in zip(cast_in, cast_out):
        dst[...] = src[...].astype(BF16)

    qi = pl.program_id(1)

    @pl.when(qi == 0)
    def _():
        mn = _rmsnorm(mem_ref[0], memn_ref[...]).astype(BF16)
        mkv_ref[0] = _dot(mn, wmkv_ref[...]).astype(BF16)

    tq = q_ref.shape[3]
    tk = KV_TILE
    half = tk // 2

    def scores(h, c):
        return _dot(k_ref[0, h, pl.ds(pl.multiple_of(c * tk, tk), tk), :], q_ref[0, h])


    def init_with_diagonal():
        causal = (lax.broadcasted_iota(jnp.int32, (half, half), 0)
                  <= lax.broadcasted_iota(jnp.int32, (half, half), 1))
        start = pl.multiple_of(qi * tk, tk)
        s_lo = [_dot(k_ref[0, h, pl.ds(start, half), :], q_ref[0, h]) for h in range(N_HEADS)]
        s_hi = [_dot(k_ref[0, h, pl.ds(start + half, half), :], q_ref[0, h, :, half:]) for h in range(N_HEADS)]
        probs = []
        for h in range(N_HEADS):
            s_ll = jnp.where(causal, s_lo[h][:, :half], -jnp.inf)
            s_lu = s_lo[h][:, half:]
            s_uu = jnp.where(causal, s_hi[h], -jnp.inf)
            m_l = jnp.max(s_ll, axis=0, keepdims=True)
            m_u = jnp.maximum(jnp.max(s_lu, axis=0, keepdims=True), jnp.max(s_uu, axis=0, keepdims=True))
            p_ll = jnp.exp2(s_ll - m_l)
            p_lu = jnp.exp2(s_lu - m_u)
            p_uu = jnp.exp2(s_uu - m_u)
            l_l = jnp.sum(p_ll, axis=0, keepdims=True)
            l_u = jnp.sum(p_lu, axis=0, keepdims=True) + jnp.sum(p_uu, axis=0, keepdims=True)
            m_ref[h, :, :half] = jnp.broadcast_to(m_l, (STAT_ROWS, half))
            m_ref[h, :, half:] = jnp.broadcast_to(m_u, (STAT_ROWS, half))
            l_ref[h, :, :half] = jnp.broadcast_to(l_l, (STAT_ROWS, half))
            l_ref[h, :, half:] = jnp.broadcast_to(l_u, (STAT_ROWS, half))
            probs.append((jnp.concatenate([p_ll, p_lu], axis=1).astype(BF16), p_uu.astype(BF16)))
        for h in range(N_HEADS):
            v_t = v_ref[0, h, qi]
            acc_l = _dot(v_t[:, :half], probs[h][0])
            acc_u = _dot(v_t[:, half:], probs[h][1])
            acc_ref[h, :, :half] = acc_l[:, :half]
            acc_ref[h, :, half:] = acc_l[:, half:] + acc_u

    def update_single_pass(blocks):
        s_all = [[scores(h, c) for c in blocks] for h in range(N_HEADS)]
        betas, probs = [], []
        for h in range(N_HEADS):
            m_old = m_ref[h, 0:1, :]
            ps = [jnp.exp2(s - m_old) for s in s_all[h]]
            m_blk = jnp.max(s_all[h][0], axis=0, keepdims=True)
            for s in s_all[h][1:]:
                m_blk = jnp.maximum(m_blk, jnp.max(s, axis=0, keepdims=True))
            m_new = jnp.maximum(m_old, m_blk)
            beta = jnp.exp2(m_old - m_new)
            l_new = l_ref[h, 0:1, :]
            for p in ps:
                l_new = l_new + jnp.sum(p, axis=0, keepdims=True)
            m_ref[h] = jnp.broadcast_to(m_new, (STAT_ROWS, tq))
            l_ref[h] = jnp.broadcast_to(l_new * beta, (STAT_ROWS, tq))
            gap_ref[h] = jnp.maximum(gap_ref[h], jnp.broadcast_to(m_blk - m_old, (STAT_ROWS, tq)))
            betas.append(beta)
            probs.append([p.astype(BF16) for p in ps])
        for h in range(N_HEADS):
            acc = acc_ref[h]
            for c, p in zip(blocks, probs[h]):
                acc = acc + _dot(v_ref[0, h, c], p)
            acc_ref[h] = acc * betas[h]

    def update_two_pass(c):
        s_all = [scores(h, c) for h in range(N_HEADS)]
        alphas, probs = [], []
        for h in range(N_HEADS):
            m_old = m_ref[h, 0:1, :]
            m_new = jnp.maximum(m_old, jnp.max(s_all[h], axis=0, keepdims=True))
            alpha = jnp.exp2(m_old - m_new)
            p = jnp.exp2(s_all[h] - m_new)
            l_new = alpha * l_ref[h, 0:1, :] + jnp.sum(p, axis=0, keepdims=True)
            m_ref[h] = jnp.broadcast_to(m_new, (STAT_ROWS, tq))
            l_ref[h] = jnp.broadcast_to(l_new, (STAT_ROWS, tq))
            alphas.append(alpha)
            probs.append(p.astype(BF16))
        for h in range(N_HEADS):
            acc_ref[h] = alphas[h] * acc_ref[h] + _dot(v_ref[0, h, c], probs[h])

    @pl.when(qi % 2 == 0)
    def _():
        init_with_diagonal()
        gap_ref[...] = jnp.zeros(gap_ref.shape, F32)

    @pl.when(qi % 2 == 1)
    def _():
        init_with_diagonal()
        gap_ref[...] = jnp.zeros(gap_ref.shape, F32)
        update_single_pass([qi - 1])

    def pair(j, _):
        update_single_pass([2 * j, 2 * j + 1])
        return 0

    lax.fori_loop(0, lax.shift_right_logical(qi, 1), pair, 0)

    worst = gap_ref[0]
    for h in range(1, N_HEADS):
        worst = jnp.maximum(worst, gap_ref[h])

    @pl.when(jnp.max(worst) > MAX_SINGLE_PASS_JUMP)
    def _():
        init_with_diagonal()

        def one(c, _):
            update_two_pass(c)
            return 0

        lax.fori_loop(0, qi, one, 0)

    for h in range(N_HEADS):
        o_ref[0, :, h * HEAD_DIM:(h + 1) * HEAD_DIM] = (acc_ref[h] / l_ref[h, 0:1, :]).T.astype(BF16)


def _cast_block_specs(w, n_rows, steps, n_q):
    cols = w.shape[1]
    per_step = n_rows // steps
    blk = per_step if n_rows % steps == 0 and per_step % BF16_SUBLANES == 0 else LANES
    n_blk = n_rows // blk
    assert n_rows % blk == 0 and n_blk <= steps
    index = lambda b, i: (jnp.minimum(b * n_q + i, n_blk - 1), 0)
    return (pl.BlockSpec((blk, cols), index), pl.BlockSpec((blk, cols), index),
            jax.ShapeDtypeStruct((n_rows, cols), BF16))


def _mla_attention(q, k, v, mem, mem_norm, w_mkv, cast_jobs):
    batch, _, seq, _ = k.shape
    assert Q_TILE == KV_TILE
    n_q = seq // Q_TILE
    cast_specs = [_cast_block_specs(w, rows, batch * n_q, n_q) for w, rows in cast_jobs]
    outs = pl.pallas_call(
        functools.partial(_attn_body, len(cast_jobs)),
        grid=(batch, n_q),
        in_specs=[pl.BlockSpec((1, N_HEADS, QK_DIM, Q_TILE), lambda b, i: (b, 0, 0, i)),
                  pl.BlockSpec((1, N_HEADS, seq, QK_DIM), lambda b, i: (b, 0, 0, 0)),
                  pl.BlockSpec((1, N_HEADS, seq // KV_TILE, HEAD_DIM, KV_TILE), lambda b, i: (b, 0, 0, 0, 0)),
                  pl.BlockSpec((1, N_MEM, D_MODEL), lambda b, i: (b, 0, 0)),
                  _const_spec(mem_norm.shape), _const_spec(w_mkv.shape)]
                 + [c[0] for c in cast_specs],
        out_specs=[pl.BlockSpec((1, Q_TILE, ATT_WIDTH), lambda b, i: (b, i, 0)),
                   pl.BlockSpec((1, N_MEM, 2 * D_MODEL), lambda b, i: (b, 0, 0))] + [c[1] for c in cast_specs],
        out_shape=[jax.ShapeDtypeStruct((batch, seq, ATT_WIDTH), BF16),
                   jax.ShapeDtypeStruct((batch, N_MEM, 2 * D_MODEL), BF16)] + [c[2] for c in cast_specs],
        scratch_shapes=[pltpu.VMEM((N_HEADS, STAT_ROWS, Q_TILE), F32),
                        pltpu.VMEM((N_HEADS, STAT_ROWS, Q_TILE), F32),
                        pltpu.VMEM((N_HEADS, HEAD_DIM, Q_TILE), F32),
                        pltpu.VMEM((N_HEADS, STAT_ROWS, Q_TILE), F32)],
        compiler_params=pltpu.CompilerParams(dimension_semantics=("arbitrary", "arbitrary"),
                                             vmem_limit_bytes=VMEM_LIMIT_BYTES),
        name="mla_attn",
    )(q, k, v, mem, mem_norm, w_mkv, *[w for w, _ in cast_jobs])
    return outs[0], outs[1], outs[2:]


def _post_body(h_ref, a_ref, mkv_ref, xn_ref, n2_ref, fn_ref, wouta_ref, wmq_ref, wmo_ref, wg_ref, wu_ref, wd_ref,
               out_ref):
    h2 = h_ref[...] + _dot(a_ref[...], wouta_ref[...])

    hn = _rmsnorm(h2, xn_ref[...]).astype(BF16)
    q = (_dot(hn, wmq_ref[...]) * MEM_SCALE).astype(BF16)
    cols = [slice(h * MEM_HEAD_DIM, (h + 1) * MEM_HEAD_DIM) for h in range(MEM_HEADS)]
    scores = [_dot_nt(q[:, c], mkv_ref[0, :, c]) for c in cols]
    probs = []
    for s in scores:
        e = jnp.exp(s - jnp.max(s, axis=-1, keepdims=True))
        probs.append((e / jnp.sum(e, axis=-1, keepdims=True)).astype(BF16))
    heads = [_dot(p, mkv_ref[0, :, D_MODEL + h * MEM_HEAD_DIM:D_MODEL + (h + 1) * MEM_HEAD_DIM]).astype(BF16)
             for h, p in enumerate(probs)]
    h3 = h2 + _dot(jnp.concatenate(heads, axis=1), wmo_ref[...])

    xn = _rmsnorm(h3, n2_ref[...]).astype(BF16)
    h4 = h3 + 0.5 * _swiglu(xn, wg_ref, wu_ref, wd_ref)
    out_ref[...] = _rmsnorm(h4, fn_ref[...])


def _post(h2d, a2d, mkv, xn, n2, fn, wouta, wmq, wmo, wg, wu, wd, seq):
    tm = POST_TILE
    n_tok = h2d.shape[0]
    tiles_per_seq = seq // tm
    weights = (xn, n2, fn, wouta, wmq, wmo, wg, wu, wd)
    return pl.pallas_call(
        _post_body,
        grid=(n_tok // tm,),
        in_specs=[pl.BlockSpec((tm, D_MODEL), lambda i: (i, 0)),
                  pl.BlockSpec((tm, ATT_WIDTH), lambda i: (i, 0)),
                  pl.BlockSpec((1, N_MEM, 2 * D_MODEL), lambda i: (i // tiles_per_seq, 0, 0))]
                 + [_const_spec(w.shape) for w in weights],
        out_specs=pl.BlockSpec((tm, D_MODEL), lambda i: (i, 0)),
        out_shape=jax.ShapeDtypeStruct((n_tok, D_MODEL), F32),
        compiler_params=pltpu.CompilerParams(dimension_semantics=("arbitrary",),
                                             vmem_limit_bytes=VMEM_LIMIT_BYTES),
        name="post",
    )(h2d, a2d, mkv, *weights)


def _pad_rope_cols(w):
    zeros = jnp.zeros(w.shape[:-1] + (ROPE_HALF,), w.dtype)
    return jnp.concatenate([w[..., :ROPE_HALF], zeros, w[..., ROPE_HALF:], zeros], axis=-1)


def _block_diag2(a, b):
    za = jnp.zeros((a.shape[0], b.shape[1]), a.dtype)
    zb = jnp.zeros((b.shape[0], a.shape[1]), a.dtype)
    return jnp.concatenate([jnp.concatenate([a, za], axis=1), jnp.concatenate([zb, b], axis=1)], axis=0)


def _layer(h2d, mem, pos, freqs, batch, seq, ffn1_norm, ffn1_w_gate, ffn1_w_up, ffn1_w_down, mix_norm, w_in,
           q_norm, w_q_up, kv_norm, w_kv_up, pool_w, pool_scale, w_out, xattn_norm, mem_norm,
           w_mq, w_mkv, w_mo, ffn2_norm, ffn2_w_gate, ffn2_w_up, ffn2_w_down, out_norm):
    row = lambda g: g.reshape(1, -1).astype(F32)
    rope_lo = Q_RANK + KV_RANK
    w_in_p = jnp.concatenate([w_in[:, :rope_lo], _pad_rope_cols(w_in[:, rope_lo:rope_lo + ROPE_DIM]),
                              w_in[:, rope_lo + ROPE_DIM:]], axis=1).astype(BF16)
    wq = w_q_up.reshape(Q_RANK, N_HEADS, HEAD_DIM + ROPE_DIM)
    wq_p = jnp.concatenate([wq[..., :HEAD_DIM], _pad_rope_cols(wq[..., HEAD_DIM:])], axis=-1)
    wq_t = wq_p.reshape(Q_RANK, N_HEADS * QK_DIM).T.astype(BF16)
    wkv = w_kv_up.reshape(KV_RANK, N_HEADS, 2 * HEAD_DIM)
    wk = wkv[..., :HEAD_DIM].reshape(KV_RANK, ATT_WIDTH).astype(BF16)
    wv_t = wkv[..., HEAD_DIM:].reshape(KV_RANK, ATT_WIDTH).T.astype(BF16)
    poolw = jnp.stack([_block_diag2(pool_w[0], pool_w[1]), _block_diag2(pool_w[2], pool_w[3])]).astype(BF16)

    h1p, q, k, v, w_mkv_bf16 = _pre(h2d, pos, freqs, row(ffn1_norm), row(mix_norm), w_in_p, row(q_norm), wq_t,
                                    row(kv_norm), wk, wv_t, poolw, row(pool_scale), w_mkv,
                                    ffn1_w_gate, ffn1_w_up, ffn1_w_down, w_out, batch, seq)
    a, mkv, post_weights = _mla_attention(
        q, k, v, mem, row(mem_norm), w_mkv_bf16,
        [(w_out, ATT_WIDTH), (w_mq, D_MODEL), (w_mo, D_MODEL),
         (ffn2_w_gate, D_MODEL), (ffn2_w_up, D_MODEL), (ffn2_w_down, D_FF)])
    return _post(h1p, a.reshape(batch * seq, ATT_WIDTH), mkv, row(xattn_norm), row(ffn2_norm), row(out_norm),
                 *post_weights, seq)


def kernel(x, mem, positions, ffn1_norm, ffn1_w_gate, ffn1_w_up, ffn1_w_down, mix_norm, w_in, q_norm, w_q_up,
           kv_norm, w_kv_up, pool_w, pool_scale, w_out, xattn_norm, mem_norm, w_mq, w_mkv, w_mo, ffn2_norm,
           ffn2_w_gate, ffn2_w_up, ffn2_w_down, final_norm):
    batch, seq, d_model = x.shape
    depth = ffn1_norm.shape[0]
    assert d_model == D_MODEL and depth == 1
    assert seq % TOKEN_TILE == 0 and seq % POST_TILE == 0 and seq % Q_TILE == 0
    freqs = 1.0 / (ROPE_BASE ** (jnp.arange(0, ROPE_DIM, 2, dtype=F32) / ROPE_DIM))
    layer = (ffn1_norm, ffn1_w_gate, ffn1_w_up, ffn1_w_down, mix_norm, w_in, q_norm, w_q_up, kv_norm, w_kv_up,
             pool_w, pool_scale, w_out, xattn_norm, mem_norm, w_mq, w_mkv, w_mo, ffn2_norm, ffn2_w_gate,
             ffn2_w_up, ffn2_w_down)
    out = _layer(x.reshape(batch * seq, d_model), mem, positions.reshape(1, batch * seq),
                 freqs.reshape(ROPE_HALF, 1), batch, seq, *[w[0] for w in layer], final_norm)
    return out.reshape(batch, seq, d_model)
```

```python
import functools

import jax
import jax.numpy as jnp
from jax import lax
from jax.experimental import pallas as pl
from jax.experimental.pallas import tpu as pltpu

F32 = jnp.float32
BF16 = jnp.bfloat16

D_MODEL = 1024
N_HEADS = 4
HEAD_DIM = 128
ROPE_DIM = 64
ROPE_HALF = ROPE_DIM // 2
Q_RANK = 256
KV_RANK = 128
ATT_WIDTH = N_HEADS * HEAD_DIM
POOL_WIDTH = D_MODEL - ATT_WIDTH
POOL_WINDOWS = (2, 4, 8, 16)
POOL_CH = POOL_WIDTH // len(POOL_WINDOWS)
POOL_HALO = 16
D_FF = 2816
N_MEM = 256
MEM_HEADS = 4
MEM_HEAD_DIM = D_MODEL // MEM_HEADS
ROPE_BASE = 10000.0
RMS_EPS = 1e-6
ATT_SCALE = (HEAD_DIM + ROPE_DIM) ** -0.5
LOG2_E = 1.4426950408889634
Q_PRESCALE = ATT_SCALE * LOG2_E
MEM_SCALE = MEM_HEAD_DIM ** -0.5

LANES = 128
BF16_SUBLANES = 16
MXU_DIM = 256
VMEM_LIMIT_BYTES = 56 * 1024 * 1024

TOKEN_TILE = 512
POST_TILE = 1024
POST_ROW_GROUP = 512
FF_CHUNK = MXU_DIM
Q_TILE = 512
KV_TILE = 512
QK_DIM = 2 * HEAD_DIM
CAST_ROWS_WIDE = 128
CAST_ROWS = 256
CAST_SLOTS = 4
STAT_ROWS = 8
MAX_SINGLE_PASS_JUMP = 32.0


def _rmsnorm(x, g):
    ms = jnp.mean(x * x, axis=-1, keepdims=True)
    return (x * lax.rsqrt(ms + RMS_EPS)) * g


def _dot(a, b):
    return jnp.dot(a, b, preferred_element_type=F32)


def _dot_nt(a, b):
    return lax.dot_general(a, b, (((1,), (1,)), ((), ())), preferred_element_type=F32)


def _swiglu(xn, wg_ref, wu_ref, wd_ref):
    n_chunks = D_FF // FF_CHUNK
    cols = lambda c: slice(c * FF_CHUNK, (c + 1) * FF_CHUNK)
    gate_up = lambda c: (_dot(xn, wg_ref[:, cols(c)]), _dot(xn, wu_ref[:, cols(c)]))
    acc = None
    g, u = gate_up(0)
    for c in range(n_chunks):
        nxt = gate_up(c + 1) if c + 1 < n_chunks else None
        a = (g / (1.0 + jnp.exp(-g))) * u
        d = _dot(a.astype(BF16), wd_ref[cols(c), :])
        acc = d if acc is None else acc + d
        if nxt is not None:
            g, u = nxt
    return acc


def _row_jobs(src, src_row0, n_rows, dst, chunk):
    assert n_rows % chunk == 0
    return [(src, src_row0 + r, dst, r) for r in range(0, n_rows, chunk)]


def _cast_weights_into_vmem(jobs, stage, sems):
    slots, rows = stage.shape[0], stage.shape[1]
    ahead = slots - 1

    def copy(j):
        src, r0, _, _ = jobs[j]
        return pltpu.make_async_copy(src.at[pl.ds(r0, rows), :], stage.at[j % slots], sems.at[j % slots])

    for j in range(min(ahead, len(jobs))):
        copy(j).start()
    for j in range(len(jobs)):
        if j + ahead < len(jobs):
            copy(j + ahead).start()
        copy(j).wait()
        _, _, dst, d0 = jobs[j]
        dst[d0:d0 + rows, :] = stage[j % slots].astype(BF16)


def _cast_scratch(wide_rows, narrow_rows):
    return [pltpu.VMEM((CAST_SLOTS, wide_rows, D_FF), F32), pltpu.VMEM((CAST_SLOTS, narrow_rows, D_MODEL), F32),
            pltpu.SemaphoreType.DMA((CAST_SLOTS,)), pltpu.SemaphoreType.DMA((CAST_SLOTS,))]


def _rope(x, cos4, sin4):
    return x * cos4 + pltpu.roll(x, 2 * ROPE_HALF, 1) * sin4


def _pre_body(tiles_per_seq, n_tiles,
              x_ref, pos_ref, freq_ref, n1_ref, mixn_ref, win_ref, qn_ref, wq_ref, kvn_ref, wk_ref, wv_ref,
              poolw_ref, pscale_ref, wg_hbm, wu_hbm, wd_hbm, wout_hbm,
              h_ref, q_ref, k_ref, v_ref,
              ext_ref, h1_ref, z_ref, cos_ref, sin_ref, wg_ref, wu_ref, wd_ref, woutp_ref,
              stage_wide, stage, sem_wide, sem):
    step = pl.program_id(0)
    tm = x_ref.shape[0]

    @pl.when(step == 0)
    def _():
        _cast_weights_into_vmem(_row_jobs(wg_hbm, 0, D_MODEL, wg_ref, CAST_ROWS_WIDE)
                                + _row_jobs(wu_hbm, 0, D_MODEL, wu_ref, CAST_ROWS_WIDE), stage_wide, sem_wide)
        _cast_weights_into_vmem(_row_jobs(wd_hbm, 0, D_FF, wd_ref, CAST_ROWS)
                                + _row_jobs(wout_hbm, ATT_WIDTH, POOL_WIDTH, woutp_ref, CAST_ROWS), stage, sem)
        h1_ref[...] = jnp.zeros(h1_ref.shape, F32)
        z_ref[...] = jnp.zeros(z_ref.shape, F32)
        ext_ref[...] = jnp.zeros(ext_ref.shape, F32)
        cos_ref[...] = jnp.zeros(cos_ref.shape, F32)
        sin_ref[...] = jnp.zeros(sin_ref.shape, F32)

    def ffn_and_project():
        x = x_ref[...]
        xn = _rmsnorm(x, n1_ref[...]).astype(BF16)
        h1 = x + 0.5 * _swiglu(xn, wg_ref, wu_ref, wd_ref)
        un = _rmsnorm(h1, mixn_ref[...]).astype(BF16)
        z = _dot(un, win_ref[...])
        batch_row = jnp.minimum(step, n_tiles - 1) // tiles_per_seq
        ang = freq_ref[...] * pos_ref[pl.ds(batch_row, 1), :].astype(F32)
        return h1, z, jnp.cos(ang), jnp.sin(ang)

    def mixer_inputs():
        seq_tile = jnp.maximum(step - 1, 0) % tiles_per_seq

        cos = cos_ref[...]
        sin = sin_ref[...]
        cos4_t = jnp.concatenate([cos, cos, cos, cos], axis=0)
        sin4_t = jnp.concatenate([-sin, -sin, sin, sin], axis=0)

        qn = _rmsnorm(z_ref[:, :Q_RANK], qn_ref[...]).astype(BF16)
        q_t = _dot_nt(wq_ref[...], qn)
        for h in range(N_HEADS):
            base = h * QK_DIM
            q_ref[0, h, :HEAD_DIM, :] = (q_t[base:base + HEAD_DIM] * Q_PRESCALE).astype(BF16)
            pe = q_t[base + HEAD_DIM:base + QK_DIM]
            pe = pe * cos4_t + pltpu.roll(pe, 2 * ROPE_HALF, 0) * sin4_t
            q_ref[0, h, HEAD_DIM:, :] = (pe * Q_PRESCALE).astype(BF16)

        kvn = _rmsnorm(z_ref[:, Q_RANK:Q_RANK + KV_RANK], kvn_ref[...]).astype(BF16)
        k_nope = _dot(kvn, wk_ref[...])
        v_t = _dot_nt(wv_ref[...], kvn)
        k_pe = _rope(z_ref[:, Q_RANK + KV_RANK:Q_RANK + KV_RANK + LANES], cos4_t.T, sin4_t.T).astype(BF16)
        for h in range(N_HEADS):
            k_ref[0, h, :, :HEAD_DIM] = k_nope[:, h * HEAD_DIM:(h + 1) * HEAD_DIM].astype(BF16)
            k_ref[0, h, :, HEAD_DIM:] = k_pe
            v_ref[0, h, 0] = v_t[h * HEAD_DIM:(h + 1) * HEAD_DIM].astype(BF16)

        zp = z_ref[:, D_MODEL - POOL_WIDTH:]
        ext_ref[0:POOL_HALO, :] = jnp.where(seq_tile == 0, 0.0, ext_ref[0:POOL_HALO, :])
        ext_ref[POOL_HALO:, :] = zp
        t1 = (seq_tile * tm + 1 + lax.broadcasted_iota(jnp.int32, (tm, 1), 0)).astype(F32)
        assert POOL_WINDOWS == (2, 4, 8, 16) and POOL_HALO >= sum(POOL_WINDOWS) // 2
        level = ext_ref[...]
        diffs = []
        for g, w in enumerate(POOL_WINDOWS):
            level = level + pltpu.roll(level, w // 2, 0)
            mean = level[POOL_HALO:, :POOL_CH] / jnp.minimum(t1, float(w))
            diffs.append(mean - zp[:, g * POOL_CH:(g + 1) * POOL_CH])
            if g + 1 < len(POOL_WINDOWS):
                level = level[:, POOL_CH:]
        ext_ref[0:POOL_HALO, :] = ext_ref[tm:tm + POOL_HALO, :]

        y01 = _dot(jnp.concatenate(diffs[:2], axis=1).astype(BF16), poolw_ref[0])
        y23 = _dot(jnp.concatenate(diffs[2:], axis=1).astype(BF16), poolw_ref[1])
        p = jnp.concatenate([y01, y23], axis=1) * pscale_ref[...]
        h_ref[...] = h1_ref[...] + _dot(p.astype(BF16), woutp_ref[...])

    @pl.when(step < n_tiles)
    def _():
        mixer_inputs()
        h1, z, cos, sin = ffn_and_project()
        h1_ref[...] = h1
        z_ref[...] = z
        cos_ref[...] = cos
        sin_ref[...] = sin

    @pl.when(step == n_tiles)
    def _():
        mixer_inputs()


def _const_spec(shape):
    nd = len(shape)
    return pl.BlockSpec(shape, lambda *_: (0,) * nd, pipeline_mode=pl.Buffered(1))


def _pre(x2d, pos, freqs, n1, mixn, win, qn, wq, kvn, wk, wv, poolw, pscale, wg, wu, wd, wout, batch, seq):
    tm = TOKEN_TILE
    assert tm == KV_TILE
    n_tok = x2d.shape[0]
    tiles_per_seq = seq // tm
    n_tiles = n_tok // tm
    lag = lambda i: jnp.maximum(i - 1, 0)
    bt = lambda i: (lag(i) // tiles_per_seq, lag(i) % tiles_per_seq)
    x_spec = pl.BlockSpec((tm, D_MODEL), lambda i: (jnp.minimum(i, n_tiles - 1), 0))
    h_spec = pl.BlockSpec((tm, D_MODEL), lambda i: (lag(i), 0))
    pos_spec = pl.BlockSpec((batch, tm), lambda i: (0, jnp.minimum(i, n_tiles - 1) % tiles_per_seq))
    q_spec = pl.BlockSpec((1, N_HEADS, QK_DIM, tm), lambda i: (bt(i)[0], 0, 0, bt(i)[1]))
    k_spec = pl.BlockSpec((1, N_HEADS, tm, QK_DIM), lambda i: (bt(i)[0], 0, bt(i)[1], 0))
    v_spec = pl.BlockSpec((1, N_HEADS, 1, HEAD_DIM, tm), lambda i: (bt(i)[0], 0, bt(i)[1], 0, 0))
    weights = (n1, mixn, win, qn, wq, kvn, wk, wv, poolw, pscale)
    hbm_weights = (wg, wu, wd, wout)
    return pl.pallas_call(
        functools.partial(_pre_body, tiles_per_seq, n_tiles),
        grid=(n_tiles + 1,),
        in_specs=[x_spec, pos_spec, _const_spec(freqs.shape)] + [_const_spec(w.shape) for w in weights]
                 + [pl.BlockSpec(memory_space=pl.ANY)] * len(hbm_weights),
        out_specs=[h_spec, q_spec, k_spec, v_spec],
        out_shape=[jax.ShapeDtypeStruct((n_tok, D_MODEL), F32),
                   jax.ShapeDtypeStruct((batch, N_HEADS, QK_DIM, seq), BF16),
                   jax.ShapeDtypeStruct((batch, N_HEADS, seq, QK_DIM), BF16),
                   jax.ShapeDtypeStruct((batch, N_HEADS, tiles_per_seq, HEAD_DIM, tm), BF16)],
        scratch_shapes=[pltpu.VMEM((tm + POOL_HALO, POOL_WIDTH), F32),
                        pltpu.VMEM((tm, D_MODEL), F32), pltpu.VMEM((tm, D_MODEL), F32),
                        pltpu.VMEM((ROPE_HALF, tm), F32), pltpu.VMEM((ROPE_HALF, tm), F32),
                        pltpu.VMEM((D_MODEL, D_FF), BF16), pltpu.VMEM((D_MODEL, D_FF), BF16),
                        pltpu.VMEM((D_FF, D_MODEL), BF16), pltpu.VMEM((POOL_WIDTH, D_MODEL), BF16)]
                       + _cast_scratch(CAST_ROWS_WIDE, CAST_ROWS),
        compiler_params=pltpu.CompilerParams(dimension_semantics=("arbitrary",),
                                             vmem_limit_bytes=VMEM_LIMIT_BYTES),
        name="pre",
    )(x2d, pos, freqs, *weights, *hbm_weights)


def _attn_body(n_cast, q_ref, k_ref, v_ref, mem_ref, memn_ref, wmkv_ref, *refs):
    cast_in, o_ref, mkv_ref = refs[:n_cast], refs[n_cast], refs[n_cast + 1]
    cast_out = refs[n_cast + 2:2 * n_cast + 2]
    m_ref, l_ref, acc_ref, gap_ref = refs[2 * n_cast + 2:]
    def cast_slices():
        for src, dst in zip(cast_in, cast_out):
            dst[...] = src[...].astype(BF16)

    qi = pl.program_id(1)

    @pl.when(qi == 0)
    def _():
        mn = _rmsnorm(mem_ref[0], memn_ref[...]).astype(BF16)
        mkv_ref[0] = _dot(mn, wmkv_ref[...]).astype(BF16)

    tq = q_ref.shape[3]
    tk = KV_TILE
    half = tk // 2

    def scores(h, c):
        return _dot(k_ref[0, h, pl.ds(pl.multiple_of(c * tk, tk), tk), :], q_ref[0, h])


    def init_with_diagonal():
        causal = (lax.broadcasted_iota(jnp.int32, (half, half), 0)
                  <= lax.broadcasted_iota(jnp.int32, (half, half), 1))
        start = pl.multiple_of(qi * tk, tk)
        s_lo = [_dot(k_ref[0, h, pl.ds(start, half), :], q_ref[0, h]) for h in range(N_HEADS)]
        s_hi = [_dot(k_ref[0, h, pl.ds(start + half, half), :], q_ref[0, h, :, half:]) for h in range(N_HEADS)]
        probs = []
        for h in range(N_HEADS):
            s_ll = jnp.where(causal, s_lo[h][:, :half], -jnp.inf)
            s_lu = s_lo[h][:, half:]
            s_uu = jnp.where(causal, s_hi[h], -jnp.inf)
            m_l = jnp.max(s_ll, axis=0, keepdims=True)
            m_u = jnp.maximum(jnp.max(s_lu, axis=0, keepdims=True), jnp.max(s_uu, axis=0, keepdims=True))
            p_ll = jnp.exp2(s_ll - m_l)
            p_lu = jnp.exp2(s_lu - m_u)
            p_uu = jnp.exp2(s_uu - m_u)
            l_l = jnp.sum(p_ll, axis=0, keepdims=True)
            l_u = jnp.sum(p_lu, axis=0, keepdims=True) + jnp.sum(p_uu, axis=0, keepdims=True)
            m_ref[h, :, :half] = jnp.broadcast_to(m_l, (STAT_ROWS, half))
            m_ref[h, :, half:] = jnp.broadcast_to(m_u, (STAT_ROWS, half))
            l_ref[h, :, :half] = jnp.broadcast_to(l_l, (STAT_ROWS, half))
            l_ref[h, :, half:] = jnp.broadcast_to(l_u, (STAT_ROWS, half))
            probs.append((jnp.concatenate([p_ll, p_lu], axis=1).astype(BF16), p_uu.astype(BF16)))
        for h in range(N_HEADS):
            v_t = v_ref[0, h, qi]
            acc_l = _dot(v_t[:, :half], probs[h][0])
            acc_u = _dot(v_t[:, half:], probs[h][1])
            acc_ref[h, :, :half] = acc_l[:, :half]
            acc_ref[h, :, half:] = acc_l[:, half:] + acc_u

    def update_single_pass(blocks):
        s_all = [[scores(h, c) for c in blocks] for h in range(N_HEADS)]
        betas, probs = [], []
        for h in range(N_HEADS):
            m_old = m_ref[h, 0:1, :]
            ps = [jnp.exp2(s - m_old) for s in s_all[h]]
            m_blk = jnp.max(s_all[h][0], axis=0, keepdims=True)
            for s in s_all[h][1:]:
                m_blk = jnp.maximum(m_blk, jnp.max(s, axis=0, keepdims=True))
            m_new = jnp.maximum(m_old, m_blk)
            beta = jnp.exp2(m_old - m_new)
            l_new = l_ref[h, 0:1, :]
            for p in ps:
                l_new = l_new + jnp.sum(p, axis=0, keepdims=True)
            m_ref[h] = jnp.broadcast_to(m_new, (STAT_ROWS, tq))
            l_ref[h] = jnp.broadcast_to(l_new * beta, (STAT_ROWS, tq))
            gap_ref[h] = jnp.maximum(gap_ref[h], jnp.broadcast_to(m_blk - m_old, (STAT_ROWS, tq)))
            betas.append(beta)
            probs.append([p.astype(BF16) for p in ps])
        for h in range(N_HEADS):
            acc = acc_ref[h]
            for c, p in zip(blocks, probs[h]):
                acc = acc + _dot(v_ref[0, h, c], p)
            acc_ref[h] = acc * betas[h]

    def update_two_pass(c):
        s_all = [scores(h, c) for h in range(N_HEADS)]
        alphas, probs = [], []
        for h in range(N_HEADS):
            m_old = m_ref[h, 0:1, :]
            m_new = jnp.maximum(m_old, jnp.max(s_all[h], axis=0, keepdims=True))
            alpha = jnp.exp2(m_old - m_new)
            p = jnp.exp2(s_all[h] - m_new)
            l_new = alpha * l_ref[h, 0:1, :] + jnp.sum(p, axis=0, keepdims=True)
            m_ref[h] = jnp.broadcast_to(m_new, (STAT_ROWS, tq))
            l_ref[h] = jnp.broadcast_to(l_new, (STAT_ROWS, tq))
            alphas.append(alpha)
            probs.append(p.astype(BF16))
        for h in range(N_HEADS):
            acc_ref[h] = alphas[h] * acc_ref[h] + _dot(v_ref[0, h, c], probs[h])

    @pl.when(qi % 2 == 0)
    def _():
        cast_slices()
        init_with_diagonal()
        gap_ref[...] = jnp.zeros(gap_ref.shape, F32)

    @pl.when(qi % 2 == 1)
    def _():
        cast_slices()
        init_with_diagonal()
        gap_ref[...] = jnp.zeros(gap_ref.shape, F32)
        update_single_pass([qi - 1])

    def pair(j, _):
        update_single_pass([2 * j, 2 * j + 1])
        return 0

    lax.fori_loop(0, lax.shift_right_logical(qi, 1), pair, 0)

    def write_output():
        for h in range(N_HEADS):
            o_ref[0, :, h * HEAD_DIM:(h + 1) * HEAD_DIM] = (acc_ref[h] / l_ref[h, 0:1, :]).T.astype(BF16)

    worst = gap_ref[0]
    for h in range(1, N_HEADS):
        worst = jnp.maximum(worst, gap_ref[h])
    write_output()

    @pl.when(jnp.max(worst) > MAX_SINGLE_PASS_JUMP)
    def _():
        init_with_diagonal()

        def one(c, _):
            update_two_pass(c)
            return 0

        lax.fori_loop(0, qi, one, 0)
        write_output()


def _cast_block_specs(w, n_rows, steps, n_q):
    cols = w.shape[1]
    per_step = n_rows // steps
    blk = per_step if n_rows % steps == 0 and per_step % BF16_SUBLANES == 0 else LANES
    n_blk = n_rows // blk
    assert n_rows % blk == 0 and n_blk <= steps
    index = lambda b, i: (jnp.minimum(b * n_q + i, n_blk - 1), 0)
    return (pl.BlockSpec((blk, cols), index), pl.BlockSpec((blk, cols), index),
            jax.ShapeDtypeStruct((n_rows, cols), BF16))


def _mla_attention(q, k, v, mem, mem_norm, w_mkv, cast_jobs):
    batch, _, seq, _ = k.shape
    assert Q_TILE == KV_TILE
    n_q = seq // Q_TILE
    cast_specs = [_cast_block_specs(w, rows, batch * n_q, n_q) for w, rows in cast_jobs]
    outs = pl.pallas_call(
        functools.partial(_attn_body, len(cast_jobs)),
        grid=(batch, n_q),
        in_specs=[pl.BlockSpec((1, N_HEADS, QK_DIM, Q_TILE), lambda b, i: (b, 0, 0, i)),
                  pl.BlockSpec((1, N_HEADS, seq, QK_DIM), lambda b, i: (b, 0, 0, 0)),
                  pl.BlockSpec((1, N_HEADS, seq // KV_TILE, HEAD_DIM, KV_TILE), lambda b, i: (b, 0, 0, 0, 0)),
                  pl.BlockSpec((1, N_MEM, D_MODEL), lambda b, i: (b, 0, 0)),
                  _const_spec(mem_norm.shape), _const_spec(w_mkv.shape)]
                 + [c[0] for c in cast_specs],
        out_specs=[pl.BlockSpec((1, Q_TILE, ATT_WIDTH), lambda b, i: (b, i, 0)),
                   pl.BlockSpec((1, N_MEM, 2 * D_MODEL), lambda b, i: (b, 0, 0))] + [c[1] for c in cast_specs],
        out_shape=[jax.ShapeDtypeStruct((batch, seq, ATT_WIDTH), BF16),
                   jax.ShapeDtypeStruct((batch, N_MEM, 2 * D_MODEL), BF16)] + [c[2] for c in cast_specs],
        scratch_shapes=[pltpu.VMEM((N_HEADS, STAT_ROWS, Q_TILE), F32),
                        pltpu.VMEM((N_HEADS, STAT_ROWS, Q_TILE), F32),
                        pltpu.VMEM((N_HEADS, HEAD_DIM, Q_TILE), F32),
                        pltpu.VMEM((N_HEADS, STAT_ROWS, Q_TILE), F32)],
        compiler_params=pltpu.CompilerParams(dimension_semantics=("arbitrary", "arbitrary"),
                                             vmem_limit_bytes=VMEM_LIMIT_BYTES),
        name="mla_attn",
    )(q, k, v, mem, mem_norm, w_mkv, *[w for w, _ in cast_jobs])
    return outs[0], outs[1], outs[2:]


def _post_body(h_ref, a_ref, mkv_ref, xn_ref, n2_ref, fn_ref, wouta_ref, wmq_ref, wmo_ref, wg_ref, wu_ref, wd_ref,
               out_ref):
    def mix(rows):
        h2 = h_ref[rows, :] + _dot(a_ref[rows, :], wouta_ref[...])

        hn = _rmsnorm(h2, xn_ref[...]).astype(BF16)
        q = (_dot(hn, wmq_ref[...]) * MEM_SCALE).astype(BF16)
        cols = [slice(h * MEM_HEAD_DIM, (h + 1) * MEM_HEAD_DIM) for h in range(MEM_HEADS)]
        scores = [_dot_nt(q[:, c], mkv_ref[0, :, c]) for c in cols]
        probs = []
        for s in scores:
            e = jnp.exp(s - jnp.max(s, axis=-1, keepdims=True))
            probs.append((e / jnp.sum(e, axis=-1, keepdims=True)).astype(BF16))
        heads = [_dot(p, mkv_ref[0, :, D_MODEL + h * MEM_HEAD_DIM:D_MODEL + (h + 1) * MEM_HEAD_DIM]).astype(BF16)
                 for h, p in enumerate(probs)]
        return h2 + _dot(jnp.concatenate(heads, axis=1), wmo_ref[...])

    def ffn(rows, h3):
        xn = _rmsnorm(h3, n2_ref[...]).astype(BF16)
        h4 = h3 + 0.5 * _swiglu(xn, wg_ref, wu_ref, wd_ref)
        out_ref[rows, :] = _rmsnorm(h4, fn_ref[...])

    tm = h_ref.shape[0]
    groups = [slice(r, r + POST_ROW_GROUP) for r in range(0, tm, POST_ROW_GROUP)]
    mixed = [mix(rows) for rows in groups]
    for rows, h3 in zip(groups, mixed):
        ffn(rows, h3)


def _post(h2d, a2d, mkv, xn, n2, fn, wouta, wmq, wmo, wg, wu, wd, seq):
    tm = POST_TILE
    n_tok = h2d.shape[0]
    tiles_per_seq = seq // tm
    weights = (xn, n2, fn, wouta, wmq, wmo, wg, wu, wd)
    return pl.pallas_call(
        _post_body,
        grid=(n_tok // tm,),
        in_specs=[pl.BlockSpec((tm, D_MODEL), lambda i: (i, 0)),
                  pl.BlockSpec((tm, ATT_WIDTH), lambda i: (i, 0)),
                  pl.BlockSpec((1, N_MEM, 2 * D_MODEL), lambda i: (i // tiles_per_seq, 0, 0))]
                 + [_const_spec(w.shape) for w in weights],
        out_specs=pl.BlockSpec((tm, D_MODEL), lambda i: (i, 0)),
        out_shape=jax.ShapeDtypeStruct((n_tok, D_MODEL), F32),
        compiler_params=pltpu.CompilerParams(dimension_semantics=("arbitrary",),
                                             vmem_limit_bytes=VMEM_LIMIT_BYTES),
        name="post",
    )(h2d, a2d, mkv, *weights)


def _pad_rope_cols(w):
    zeros = jnp.zeros(w.shape[:-1] + (ROPE_HALF,), w.dtype)
    return jnp.concatenate([w[..., :ROPE_HALF], zeros, w[..., ROPE_HALF:], zeros], axis=-1)


def _block_diag2(a, b):
    za = jnp.zeros((a.shape[0], b.shape[1]), a.dtype)
    zb = jnp.zeros((b.shape[0], a.shape[1]), a.dtype)
    return jnp.concatenate([jnp.concatenate([a, za], axis=1), jnp.concatenate([zb, b], axis=1)], axis=0)


def _layer(h2d, mem, pos, freqs, batch, seq, ffn1_norm, ffn1_w_gate, ffn1_w_up, ffn1_w_down, mix_norm, w_in,
           q_norm, w_q_up, kv_norm, w_kv_up, pool_w, pool_scale, w_out, xattn_norm, mem_norm,
           w_mq, w_mkv, w_mo, ffn2_norm, ffn2_w_gate, ffn2_w_up, ffn2_w_down, out_norm):
    row = lambda g: g.reshape(1, -1).astype(F32)
    rope_lo = Q_RANK + KV_RANK
    w_in_p = jnp.concatenate([w_in[:, :rope_lo], _pad_rope_cols(w_in[:, rope_lo:rope_lo + ROPE_DIM]),
                              w_in[:, rope_lo + ROPE_DIM:]], axis=1).astype(BF16)
    wq = w_q_up.reshape(Q_RANK, N_HEADS, HEAD_DIM + ROPE_DIM)
    wq_p = jnp.concatenate([wq[..., :HEAD_DIM], _pad_rope_cols(wq[..., HEAD_DIM:])], axis=-1)
    wq_t = wq_p.reshape(Q_RANK, N_HEADS * QK_DIM).T.astype(BF16)
    wkv = w_kv_up.reshape(KV_RANK, N_HEADS, 2 * HEAD_DIM)
    wk = wkv[..., :HEAD_DIM].reshape(KV_RANK, ATT_WIDTH).astype(BF16)
    wv_t = wkv[..., HEAD_DIM:].reshape(KV_RANK, ATT_WIDTH).T.astype(BF16)
    poolw = jnp.stack([_block_diag2(pool_w[0], pool_w[1]), _block_diag2(pool_w[2], pool_w[3])]).astype(BF16)

    h1p, q, k, v = _pre(h2d, pos, freqs, row(ffn1_norm), row(mix_norm), w_in_p, row(q_norm), wq_t, row(kv_norm),
                        wk, wv_t, poolw, row(pool_scale), ffn1_w_gate, ffn1_w_up, ffn1_w_down, w_out, batch, seq)
    a, mkv, post_weights = _mla_attention(
        q, k, v, mem, row(mem_norm), w_mkv.astype(BF16),
        [(w_out, ATT_WIDTH), (w_mq, D_MODEL), (w_mo, D_MODEL),
         (ffn2_w_gate, D_MODEL), (ffn2_w_up, D_MODEL), (ffn2_w_down, D_FF)])
    return _post(h1p, a.reshape(batch * seq, ATT_WIDTH), mkv, row(xattn_norm), row(ffn2_norm), row(out_norm),
                 *post_weights, seq)


def kernel(x, mem, positions, ffn1_norm, ffn1_w_gate, ffn1_w_up, ffn1_w_down, mix_norm, w_in, q_norm, w_q_up,
           kv_norm, w_kv_up, pool_w, pool_scale, w_out, xattn_norm, mem_norm, w_mq, w_mkv, w_mo, ffn2_norm,
           ffn2_w_gate, ffn2_w_up, ffn2_w_down, final_norm):
    batch, seq, d_model = x.shape
    depth = ffn1_norm.shape[0]
    assert d_model == D_MODEL and depth == 1
    assert seq % TOKEN_TILE == 0 and seq % POST_TILE == 0 and seq % Q_TILE == 0
    freqs = 1.0 / (ROPE_BASE ** (jnp.arange(0, ROPE_DIM, 2, dtype=F32) / ROPE_DIM))
    layer = (ffn1_norm, ffn1_w_gate, ffn1_w_up, ffn1_w_down, mix_norm, w_in, q_norm, w_q_up, kv_norm, w_kv_up,
             pool_w, pool_scale, w_out, xattn_norm, mem_norm, w_mq, w_mkv, w_mo, ffn2_norm, ffn2_w_gate,
             ffn2_w_up, ffn2_w_down)
    out = _layer(x.reshape(batch * seq, d_model), mem, positions,
                 freqs.reshape(ROPE_HALF, 1), batch, seq, *[w[0] for w in layer], final_norm)
    return out.reshape(batch, seq, d_model)
```

```python
import functools

import jax
import jax.numpy as jnp
from jax import lax
from jax.experimental import pallas as pl
from jax.experimental.pallas import tpu as pltpu

F32 = jnp.float32
BF16 = jnp.bfloat16

D_MODEL = 1024
N_HEADS = 4
HEAD_DIM = 128
ROPE_DIM = 64
ROPE_HALF = ROPE_DIM // 2
Q_RANK = 256
KV_RANK = 128
ATT_WIDTH = N_HEADS * HEAD_DIM
POOL_WIDTH = D_MODEL - ATT_WIDTH
POOL_WINDOWS = (2, 4, 8, 16)
POOL_CH = POOL_WIDTH // len(POOL_WINDOWS)
POOL_HALO = 16
D_FF = 2816
N_MEM = 256
MEM_HEADS = 4
MEM_HEAD_DIM = D_MODEL // MEM_HEADS
ROPE_BASE = 10000.0
RMS_EPS = 1e-6
ATT_SCALE = (HEAD_DIM + ROPE_DIM) ** -0.5
LOG2_E = 1.4426950408889634
Q_PRESCALE = ATT_SCALE * LOG2_E
MEM_SCALE = MEM_HEAD_DIM ** -0.5

LANES = 128
BF16_SUBLANES = 16
MXU_DIM = 256
VMEM_LIMIT_BYTES = 56 * 1024 * 1024

TOKEN_TILE = 512
POST_TILE = 1024
POST_ROW_GROUP = 512
FF_CHUNK = MXU_DIM
Q_TILE = 512
KV_TILE = 512
QK_DIM = 2 * HEAD_DIM
CAST_ROWS_WIDE = 128
CAST_ROWS = 256
CAST_SLOTS = 4
STAT_ROWS = 8
MAX_SINGLE_PASS_JUMP = 32.0


def _rmsnorm(x, g):
    ms = jnp.mean(x * x, axis=-1, keepdims=True)
    return (x * lax.rsqrt(ms + RMS_EPS)) * g


def _dot(a, b):
    return jnp.dot(a, b, preferred_element_type=F32)


def _dot_nt(a, b):
    return lax.dot_general(a, b, (((1,), (1,)), ((), ())), preferred_element_type=F32)


def _swiglu(xn, wg_ref, wu_ref, wd_ref):
    n_chunks = D_FF // FF_CHUNK
    cols = lambda c: slice(c * FF_CHUNK, (c + 1) * FF_CHUNK)
    gate_up = lambda c: (_dot(xn, wg_ref[:, cols(c)]), _dot(xn, wu_ref[:, cols(c)]))
    acc = None
    g, u = gate_up(0)
    for c in range(n_chunks):
        nxt = gate_up(c + 1) if c + 1 < n_chunks else None
        a = (g / (1.0 + jnp.exp(-g))) * u
        d = _dot(a.astype(BF16), wd_ref[cols(c), :])
        acc = d if acc is None else acc + d
        if nxt is not None:
            g, u = nxt
    return acc


def _row_jobs(src, src_row0, n_rows, dst, chunk):
    assert n_rows % chunk == 0
    return [(src, src_row0 + r, dst, r) for r in range(0, n_rows, chunk)]


def _cast_weights_into_vmem(jobs, stage, sems, place=None):
    slots, rows = stage.shape[0], stage.shape[1]
    ahead = slots - 1

    def copy(j):
        src, r0, _, _ = jobs[j]
        return pltpu.make_async_copy(src.at[pl.ds(r0, rows), :], stage.at[j % slots], sems.at[j % slots])

    for j in range(min(ahead, len(jobs))):
        copy(j).start()
    for j in range(len(jobs)):
        if j + ahead < len(jobs):
            copy(j + ahead).start()
        copy(j).wait()
        _, _, dst, d0 = jobs[j]
        block = stage[j % slots]
        dst[d0:d0 + rows, :] = (block if place is None else place(block)).astype(BF16)


def _cast_scratch(wide_rows, narrow_rows):
    return [pltpu.VMEM((CAST_SLOTS, wide_rows, D_FF), F32), pltpu.VMEM((CAST_SLOTS, narrow_rows, D_MODEL), F32),
            pltpu.SemaphoreType.DMA((CAST_SLOTS,)), pltpu.SemaphoreType.DMA((CAST_SLOTS,))]


def _pad_rope_cols(w_in_rows):
    lo = Q_RANK + KV_RANK
    zeros = jnp.zeros((w_in_rows.shape[0], ROPE_HALF), w_in_rows.dtype)
    return jnp.concatenate([w_in_rows[:, :lo + ROPE_HALF], zeros, w_in_rows[:, lo + ROPE_HALF:lo + ROPE_DIM], zeros,
                            w_in_rows[:, lo + ROPE_DIM:]], axis=1)


def _load_small_weights(wq_hbm, wkv_hbm, poolw_hbm, stage_q, stage_kv, stage_pool, wq_ref, wk_ref, wv_ref, poolw_ref):
    pltpu.sync_copy(wq_hbm, stage_q)
    pltpu.sync_copy(wkv_hbm, stage_kv)
    pltpu.sync_copy(poolw_hbm, stage_pool)

    per_head = HEAD_DIM + ROPE_DIM
    heads = []
    for h in range(N_HEADS):
        cols = stage_q[:, h * per_head:(h + 1) * per_head]
        zeros = jnp.zeros((cols.shape[0], ROPE_HALF), F32)
        heads.append(jnp.concatenate([cols[:, :HEAD_DIM + ROPE_HALF], zeros, cols[:, HEAD_DIM + ROPE_HALF:], zeros],
                                     axis=1))
    wq_ref[...] = jnp.concatenate(heads, axis=1).T.astype(BF16)

    wk_ref[...] = jnp.concatenate([stage_kv[:, 2 * h * HEAD_DIM:(2 * h + 1) * HEAD_DIM] for h in range(N_HEADS)],
                                  axis=1).astype(BF16)
    wv_ref[...] = jnp.concatenate([stage_kv[:, (2 * h + 1) * HEAD_DIM:(2 * h + 2) * HEAD_DIM]
                                   for h in range(N_HEADS)], axis=1).T.astype(BF16)

    poolw_ref[...] = jnp.zeros(poolw_ref.shape, BF16)
    for g in range(len(POOL_WINDOWS)):
        lo = (g % 2) * POOL_CH
        poolw_ref[g // 2, lo:lo + POOL_CH, lo:lo + POOL_CH] = stage_pool[g].astype(BF16)


def _rope(x, cos4, sin4):
    return x * cos4 + pltpu.roll(x, 2 * ROPE_HALF, 1) * sin4


def _pre_body(tiles_per_seq, n_tiles,
              x_ref, pos_ref, freq_ref, n1_ref, mixn_ref, qn_ref, kvn_ref, pscale_ref,
              wg_hbm, wu_hbm, wd_hbm, wout_hbm, win_hbm, wq_hbm, wkv_hbm, poolw_hbm,
              h_ref, q_ref, k_ref, v_ref,
              ext_ref, h1_ref, z_ref, cos_ref, sin_ref, wg_ref, wu_ref, wd_ref, woutp_ref, win_ref,
              wq_ref, wk_ref, wv_ref, poolw_ref,
              stage_wide, stage, sem_wide, sem, stage_in, sem_in, stage_q, stage_kv, stage_pool):
    step = pl.program_id(0)
    tm = x_ref.shape[0]

    @pl.when(step == 0)
    def _():
        _cast_weights_into_vmem(_row_jobs(wg_hbm, 0, D_MODEL, wg_ref, CAST_ROWS_WIDE)
                                + _row_jobs(wu_hbm, 0, D_MODEL, wu_ref, CAST_ROWS_WIDE), stage_wide, sem_wide)
        _cast_weights_into_vmem(_row_jobs(wd_hbm, 0, D_FF, wd_ref, CAST_ROWS)
                                + _row_jobs(wout_hbm, ATT_WIDTH, POOL_WIDTH, woutp_ref, CAST_ROWS), stage, sem)
        _cast_weights_into_vmem(_row_jobs(win_hbm, 0, D_MODEL, win_ref, stage_in.shape[1]), stage_in, sem_in,
                                _pad_rope_cols)
        _load_small_weights(wq_hbm, wkv_hbm, poolw_hbm, stage_q, stage_kv, stage_pool,
                            wq_ref, wk_ref, wv_ref, poolw_ref)
        h1_ref[...] = jnp.zeros(h1_ref.shape, F32)
        z_ref[...] = jnp.zeros(z_ref.shape, F32)
        ext_ref[...] = jnp.zeros(ext_ref.shape, F32)
        cos_ref[...] = jnp.zeros(cos_ref.shape, F32)
        sin_ref[...] = jnp.zeros(sin_ref.shape, F32)

    def ffn_and_project():
        x = x_ref[...]
        xn = _rmsnorm(x, n1_ref[...]).astype(BF16)
        h1 = x + 0.5 * _swiglu(xn, wg_ref, wu_ref, wd_ref)
        un = _rmsnorm(h1, mixn_ref[...]).astype(BF16)
        z = _dot(un, win_ref[...])
        batch_row = jnp.minimum(step, n_tiles - 1) // tiles_per_seq
        ang = freq_ref[...] * pos_ref[pl.ds(batch_row, 1), :].astype(F32)
        return h1, z, jnp.cos(ang), jnp.sin(ang)

    def mixer_inputs():
        seq_tile = jnp.maximum(step - 1, 0) % tiles_per_seq

        cos = cos_ref[...]
        sin = sin_ref[...]
        cos4_t = jnp.concatenate([cos, cos, cos, cos], axis=0)
        sin4_t = jnp.concatenate([-sin, -sin, sin, sin], axis=0)

        qn = _rmsnorm(z_ref[:, :Q_RANK], qn_ref[...]).astype(BF16)
        q_t = _dot_nt(wq_ref[...], qn)
        for h in range(N_HEADS):
            base = h * QK_DIM
            q_ref[0, h, :HEAD_DIM, :] = (q_t[base:base + HEAD_DIM] * Q_PRESCALE).astype(BF16)
            pe = q_t[base + HEAD_DIM:base + QK_DIM]
            pe = pe * cos4_t + pltpu.roll(pe, 2 * ROPE_HALF, 0) * sin4_t
            q_ref[0, h, HEAD_DIM:, :] = (pe * Q_PRESCALE).astype(BF16)

        kvn = _rmsnorm(z_ref[:, Q_RANK:Q_RANK + KV_RANK], kvn_ref[...]).astype(BF16)
        k_nope = _dot(kvn, wk_ref[...])
        v_t = _dot_nt(wv_ref[...], kvn)
        k_pe = _rope(z_ref[:, Q_RANK + KV_RANK:Q_RANK + KV_RANK + LANES], cos4_t.T, sin4_t.T).astype(BF16)
        for h in range(N_HEADS):
            k_ref[0, h, :, :HEAD_DIM] = k_nope[:, h * HEAD_DIM:(h + 1) * HEAD_DIM].astype(BF16)
            k_ref[0, h, :, HEAD_DIM:] = k_pe
            v_ref[0, h, 0] = v_t[h * HEAD_DIM:(h + 1) * HEAD_DIM].astype(BF16)

        zp = z_ref[:, D_MODEL - POOL_WIDTH:]
        ext_ref[0:POOL_HALO, :] = jnp.where(seq_tile == 0, 0.0, ext_ref[0:POOL_HALO, :])
        ext_ref[POOL_HALO:, :] = zp
        t1 = (seq_tile * tm + 1 + lax.broadcasted_iota(jnp.int32, (tm, 1), 0)).astype(F32)
        assert POOL_WINDOWS == (2, 4, 8, 16) and POOL_HALO >= sum(POOL_WINDOWS) // 2
        level = ext_ref[...]
        diffs = []
        for g, w in enumerate(POOL_WINDOWS):
            level = level + pltpu.roll(level, w // 2, 0)
            mean = level[POOL_HALO:, :POOL_CH] / jnp.minimum(t1, float(w))
            diffs.append(mean - zp[:, g * POOL_CH:(g + 1) * POOL_CH])
            if g + 1 < len(POOL_WINDOWS):
                level = level[:, POOL_CH:]
        ext_ref[0:POOL_HALO, :] = ext_ref[tm:tm + POOL_HALO, :]

        y01 = _dot(jnp.concatenate(diffs[:2], axis=1).astype(BF16), poolw_ref[0])
        y23 = _dot(jnp.concatenate(diffs[2:], axis=1).astype(BF16), poolw_ref[1])
        p = jnp.concatenate([y01, y23], axis=1) * pscale_ref[...]
        h_ref[...] = h1_ref[...] + _dot(p.astype(BF16), woutp_ref[...])

    @pl.when(step < n_tiles)
    def _():
        mixer_inputs()
        h1, z, cos, sin = ffn_and_project()
        h1_ref[...] = h1
        z_ref[...] = z
        cos_ref[...] = cos
        sin_ref[...] = sin

    @pl.when(step == n_tiles)
    def _():
        mixer_inputs()


def _const_spec(shape):
    nd = len(shape)
    return pl.BlockSpec(shape, lambda *_: (0,) * nd, pipeline_mode=pl.Buffered(1))


def _pre(x2d, pos, freqs, n1, mixn, qn, kvn, pscale, wg, wu, wd, wout, win, wq, wkv, poolw, batch, seq):
    tm = TOKEN_TILE
    assert tm == KV_TILE
    n_tok = x2d.shape[0]
    tiles_per_seq = seq // tm
    n_tiles = n_tok // tm
    lag = lambda i: jnp.maximum(i - 1, 0)
    bt = lambda i: (lag(i) // tiles_per_seq, lag(i) % tiles_per_seq)
    x_spec = pl.BlockSpec((tm, D_MODEL), lambda i: (jnp.minimum(i, n_tiles - 1), 0))
    h_spec = pl.BlockSpec((tm, D_MODEL), lambda i: (lag(i), 0))
    pos_spec = pl.BlockSpec((batch, tm), lambda i: (0, jnp.minimum(i, n_tiles - 1) % tiles_per_seq))
    q_spec = pl.BlockSpec((1, N_HEADS, QK_DIM, tm), lambda i: (bt(i)[0], 0, 0, bt(i)[1]))
    k_spec = pl.BlockSpec((1, N_HEADS, tm, QK_DIM), lambda i: (bt(i)[0], 0, bt(i)[1], 0))
    v_spec = pl.BlockSpec((1, N_HEADS, 1, HEAD_DIM, tm), lambda i: (bt(i)[0], 0, bt(i)[1], 0, 0))
    weights = (n1, mixn, qn, kvn, pscale)
    hbm_weights = (wg, wu, wd, wout, win, wq, wkv, poolw)
    return pl.pallas_call(
        functools.partial(_pre_body, tiles_per_seq, n_tiles),
        grid=(n_tiles + 1,),
        in_specs=[x_spec, pos_spec, _const_spec(freqs.shape)] + [_const_spec(w.shape) for w in weights]
                 + [pl.BlockSpec(memory_space=pl.ANY)] * len(hbm_weights),
        out_specs=[h_spec, q_spec, k_spec, v_spec],
        out_shape=[jax.ShapeDtypeStruct((n_tok, D_MODEL), F32),
                   jax.ShapeDtypeStruct((batch, N_HEADS, QK_DIM, seq), BF16),
                   jax.ShapeDtypeStruct((batch, N_HEADS, seq, QK_DIM), BF16),
                   jax.ShapeDtypeStruct((batch, N_HEADS, tiles_per_seq, HEAD_DIM, tm), BF16)],
        scratch_shapes=[pltpu.VMEM((tm + POOL_HALO, POOL_WIDTH), F32),
                        pltpu.VMEM((tm, D_MODEL), F32), pltpu.VMEM((tm, D_MODEL), F32),
                        pltpu.VMEM((ROPE_HALF, tm), F32), pltpu.VMEM((ROPE_HALF, tm), F32),
                        pltpu.VMEM((D_MODEL, D_FF), BF16), pltpu.VMEM((D_MODEL, D_FF), BF16),
                        pltpu.VMEM((D_FF, D_MODEL), BF16), pltpu.VMEM((POOL_WIDTH, D_MODEL), BF16),
                        pltpu.VMEM((D_MODEL, D_MODEL), BF16),
                        pltpu.VMEM((N_HEADS * QK_DIM, Q_RANK), BF16), pltpu.VMEM((KV_RANK, ATT_WIDTH), BF16),
                        pltpu.VMEM((ATT_WIDTH, KV_RANK), BF16),
                        pltpu.VMEM((len(POOL_WINDOWS) // 2, 2 * POOL_CH, 2 * POOL_CH), BF16)]
                       + _cast_scratch(CAST_ROWS_WIDE, CAST_ROWS)
                       + [pltpu.VMEM((2, CAST_ROWS_WIDE) + win.shape[1:], F32), pltpu.SemaphoreType.DMA((2,)),
                          pltpu.VMEM(wq.shape, F32), pltpu.VMEM(wkv.shape, F32), pltpu.VMEM(poolw.shape, F32)],
        compiler_params=pltpu.CompilerParams(dimension_semantics=("arbitrary",),
                                             vmem_limit_bytes=VMEM_LIMIT_BYTES),
        name="pre",
    )(x2d, pos, freqs, *weights, *hbm_weights)


def _attn_body(n_cast, q_ref, k_ref, v_ref, mem_ref, memn_ref, wmkv_ref, *refs):
    cast_in, o_ref, mkv_ref = refs[:n_cast], refs[n_cast], refs[n_cast + 1]
    cast_out = refs[n_cast + 2:2 * n_cast + 2]
    m_ref, l_ref, acc_ref, gap_ref = refs[2 * n_cast + 2:]
    def cast_slices():
        for src, dst in zip(cast_in, cast_out):
            dst[...] = src[...].astype(BF16)

    qi = pl.program_id(1)

    @pl.when(qi == 0)
    def _():
        mn = _rmsnorm(mem_ref[0], memn_ref[...]).astype(BF16)
        mkv_ref[0] = _dot(mn, wmkv_ref[...]).astype(BF16)

    tq = q_ref.shape[3]
    tk = KV_TILE
    half = tk // 2

    def scores(h, c):
        return _dot(k_ref[0, h, pl.ds(pl.multiple_of(c * tk, tk), tk), :], q_ref[0, h])


    def init_with_diagonal():
        causal = (lax.broadcasted_iota(jnp.int32, (half, half), 0)
                  <= lax.broadcasted_iota(jnp.int32, (half, half), 1))
        start = pl.multiple_of(qi * tk, tk)
        s_lo = [_dot(k_ref[0, h, pl.ds(start, half), :], q_ref[0, h]) for h in range(N_HEADS)]
        s_hi = [_dot(k_ref[0, h, pl.ds(start + half, half), :], q_ref[0, h, :, half:]) for h in range(N_HEADS)]
        probs = []
        for h in range(N_HEADS):
            s_ll = jnp.where(causal, s_lo[h][:, :half], -jnp.inf)
            s_lu = s_lo[h][:, half:]
            s_uu = jnp.where(causal, s_hi[h], -jnp.inf)
            m_l = jnp.max(s_ll, axis=0, keepdims=True)
            m_u = jnp.maximum(jnp.max(s_lu, axis=0, keepdims=True), jnp.max(s_uu, axis=0, keepdims=True))
            p_ll = jnp.exp2(s_ll - m_l)
            p_lu = jnp.exp2(s_lu - m_u)
            p_uu = jnp.exp2(s_uu - m_u)
            l_l = jnp.sum(p_ll, axis=0, keepdims=True)
            l_u = jnp.sum(p_lu, axis=0, keepdims=True) + jnp.sum(p_uu, axis=0, keepdims=True)
            m_ref[h, :, :half] = jnp.broadcast_to(m_l, (STAT_ROWS, half))
            m_ref[h, :, half:] = jnp.broadcast_to(m_u, (STAT_ROWS, half))
            l_ref[h, :, :half] = jnp.broadcast_to(l_l, (STAT_ROWS, half))
            l_ref[h, :, half:] = jnp.broadcast_to(l_u, (STAT_ROWS, half))
            probs.append((jnp.concatenate([p_ll, p_lu], axis=1).astype(BF16), p_uu.astype(BF16)))
        for h in range(N_HEADS):
            v_t = v_ref[0, h, qi]
            acc_l = _dot(v_t[:, :half], probs[h][0])
            acc_u = _dot(v_t[:, half:], probs[h][1])
            acc_ref[h, :, :half] = acc_l[:, :half]
            acc_ref[h, :, half:] = acc_l[:, half:] + acc_u

    def update_single_pass(blocks):
        s_all = [[scores(h, c) for c in blocks] for h in range(N_HEADS)]
        betas, probs = [], []
        for h in range(N_HEADS):
            m_old = m_ref[h, 0:1, :]
            ps = [jnp.exp2(s - m_old) for s in s_all[h]]
            m_blk = jnp.max(s_all[h][0], axis=0, keepdims=True)
            for s in s_all[h][1:]:
                m_blk = jnp.maximum(m_blk, jnp.max(s, axis=0, keepdims=True))
            m_new = jnp.maximum(m_old, m_blk)
            beta = jnp.exp2(m_old - m_new)
            l_new = l_ref[h, 0:1, :]
            for p in ps:
                l_new = l_new + jnp.sum(p, axis=0, keepdims=True)
            m_ref[h] = jnp.broadcast_to(m_new, (STAT_ROWS, tq))
            l_ref[h] = jnp.broadcast_to(l_new * beta, (STAT_ROWS, tq))
            gap_ref[h] = jnp.maximum(gap_ref[h], jnp.broadcast_to(m_blk - m_old, (STAT_ROWS, tq)))
            betas.append(beta)
            probs.append([p.astype(BF16) for p in ps])
        for h in range(N_HEADS):
            acc = acc_ref[h]
            for c, p in zip(blocks, probs[h]):
                acc = acc + _dot(v_ref[0, h, c], p)
            acc_ref[h] = acc * betas[h]

    def update_two_pass(c):
        s_all = [scores(h, c) for h in range(N_HEADS)]
        alphas, probs = [], []
        for h in range(N_HEADS):
            m_old = m_ref[h, 0:1, :]
            m_new = jnp.maximum(m_old, jnp.max(s_all[h], axis=0, keepdims=True))
            alpha = jnp.exp2(m_old - m_new)
            p = jnp.exp2(s_all[h] - m_new)
            l_new = alpha * l_ref[h, 0:1, :] + jnp.sum(p, axis=0, keepdims=True)
            m_ref[h] = jnp.broadcast_to(m_new, (STAT_ROWS, tq))
            l_ref[h] = jnp.broadcast_to(l_new, (STAT_ROWS, tq))
            alphas.append(alpha)
            probs.append(p.astype(BF16))
        for h in range(N_HEADS):
            acc_ref[h] = alphas[h] * acc_ref[h] + _dot(v_ref[0, h, c], probs[h])

    @pl.when(qi % 2 == 0)
    def _():
        cast_slices()
        init_with_diagonal()
        gap_ref[...] = jnp.zeros(gap_ref.shape, F32)

    @pl.when(qi % 2 == 1)
    def _():
        cast_slices()
        init_with_diagonal()
        gap_ref[...] = jnp.zeros(gap_ref.shape, F32)
        update_single_pass([qi - 1])

    def pair(j, _):
        update_single_pass([2 * j, 2 * j + 1])
        return 0

    lax.fori_loop(0, lax.shift_right_logical(qi, 1), pair, 0)

    def write_output():
        for h in range(N_HEADS):
            o_ref[0, :, h * HEAD_DIM:(h + 1) * HEAD_DIM] = (acc_ref[h] / l_ref[h, 0:1, :]).T.astype(BF16)

    worst = gap_ref[0]
    for h in range(1, N_HEADS):
        worst = jnp.maximum(worst, gap_ref[h])
    write_output()

    @pl.when(jnp.max(worst) > MAX_SINGLE_PASS_JUMP)
    def _():
        init_with_diagonal()

        def one(c, _):
            update_two_pass(c)
            return 0

        lax.fori_loop(0, qi, one, 0)
        write_output()


def _cast_block_specs(w, n_rows, steps, n_q):
    cols = w.shape[1]
    per_step = n_rows // steps
    blk = per_step if n_rows % steps == 0 and per_step % BF16_SUBLANES == 0 else LANES
    n_blk = n_rows // blk
    assert n_rows % blk == 0 and n_blk <= steps
    index = lambda b, i: (jnp.minimum(b * n_q + i, n_blk - 1), 0)
    return (pl.BlockSpec((blk, cols), index), pl.BlockSpec((blk, cols), index),
            jax.ShapeDtypeStruct((n_rows, cols), BF16))


def _mla_attention(q, k, v, mem, mem_norm, w_mkv, cast_jobs):
    batch, _, seq, _ = k.shape
    assert Q_TILE == KV_TILE
    n_q = seq // Q_TILE
    cast_specs = [_cast_block_specs(w, rows, batch * n_q, n_q) for w, rows in cast_jobs]
    outs = pl.pallas_call(
        functools.partial(_attn_body, len(cast_jobs)),
        grid=(batch, n_q),
        in_specs=[pl.BlockSpec((1, N_HEADS, QK_DIM, Q_TILE), lambda b, i: (b, 0, 0, i)),
                  pl.BlockSpec((1, N_HEADS, seq, QK_DIM), lambda b, i: (b, 0, 0, 0)),
                  pl.BlockSpec((1, N_HEADS, seq // KV_TILE, HEAD_DIM, KV_TILE), lambda b, i: (b, 0, 0, 0, 0)),
                  pl.BlockSpec((1, N_MEM, D_MODEL), lambda b, i: (b, 0, 0)),
                  _const_spec(mem_norm.shape), _const_spec(w_mkv.shape)]
                 + [c[0] for c in cast_specs],
        out_specs=[pl.BlockSpec((1, Q_TILE, ATT_WIDTH), lambda b, i: (b, i, 0)),
                   pl.BlockSpec((1, N_MEM, 2 * D_MODEL), lambda b, i: (b, 0, 0))] + [c[1] for c in cast_specs],
        out_shape=[jax.ShapeDtypeStruct((batch, seq, ATT_WIDTH), BF16),
                   jax.ShapeDtypeStruct((batch, N_MEM, 2 * D_MODEL), BF16)] + [c[2] for c in cast_specs],
        scratch_shapes=[pltpu.VMEM((N_HEADS, STAT_ROWS, Q_TILE), F32),
                        pltpu.VMEM((N_HEADS, STAT_ROWS, Q_TILE), F32),
                        pltpu.VMEM((N_HEADS, HEAD_DIM, Q_TILE), F32),
                        pltpu.VMEM((N_HEADS, STAT_ROWS, Q_TILE), F32)],
        compiler_params=pltpu.CompilerParams(dimension_semantics=("arbitrary", "arbitrary"),
                                             vmem_limit_bytes=VMEM_LIMIT_BYTES),
        name="mla_attn",
    )(q, k, v, mem, mem_norm, w_mkv, *[w for w, _ in cast_jobs])
    return outs[0], outs[1], outs[2:]


def _post_body(h_ref, a_ref, mkv_ref, xn_ref, n2_ref, fn_ref, wouta_ref, wmq_ref, wmo_ref, wg_ref, wu_ref, wd_ref,
               out_ref):
    def mix(rows):
        h2 = h_ref[rows, :] + _dot(a_ref[rows, :], wouta_ref[...])

        hn = _rmsnorm(h2, xn_ref[...]).astype(BF16)
        q = (_dot(hn, wmq_ref[...]) * MEM_SCALE).astype(BF16)
        cols = [slice(h * MEM_HEAD_DIM, (h + 1) * MEM_HEAD_DIM) for h in range(MEM_HEADS)]
        scores = [_dot_nt(q[:, c], mkv_ref[0, :, c]) for c in cols]
        probs = []
        for s in scores:
            e = jnp.exp(s - jnp.max(s, axis=-1, keepdims=True))
            probs.append((e / jnp.sum(e, axis=-1, keepdims=True)).astype(BF16))
        heads = [_dot(p, mkv_ref[0, :, D_MODEL + h * MEM_HEAD_DIM:D_MODEL + (h + 1) * MEM_HEAD_DIM]).astype(BF16)
                 for h, p in enumerate(probs)]
        return h2 + _dot(jnp.concatenate(heads, axis=1), wmo_ref[...])

    def ffn(rows, h3):
        xn = _rmsnorm(h3, n2_ref[...]).astype(BF16)
        h4 = h3 + 0.5 * _swiglu(xn, wg_ref, wu_ref, wd_ref)
        out_ref[rows, :] = _rmsnorm(h4, fn_ref[...])

    tm = h_ref.shape[0]
    groups = [slice(r, r + POST_ROW_GROUP) for r in range(0, tm, POST_ROW_GROUP)]
    mixed = [mix(rows) for rows in groups]
    for rows, h3 in zip(groups, mixed):
        ffn(rows, h3)


def _post(h2d, a2d, mkv, xn, n2, fn, wouta, wmq, wmo, wg, wu, wd, seq):
    tm = POST_TILE
    n_tok = h2d.shape[0]
    tiles_per_seq = seq // tm
    weights = (xn, n2, fn, wouta, wmq, wmo, wg, wu, wd)
    return pl.pallas_call(
        _post_body,
        grid=(n_tok // tm,),
        in_specs=[pl.BlockSpec((tm, D_MODEL), lambda i: (i, 0)),
                  pl.BlockSpec((tm, ATT_WIDTH), lambda i: (i, 0)),
                  pl.BlockSpec((1, N_MEM, 2 * D_MODEL), lambda i: (i // tiles_per_seq, 0, 0))]
                 + [_const_spec(w.shape) for w in weights],
        out_specs=pl.BlockSpec((tm, D_MODEL), lambda i: (i, 0)),
        out_shape=jax.ShapeDtypeStruct((n_tok, D_MODEL), F32),
        compiler_params=pltpu.CompilerParams(dimension_semantics=("arbitrary",),
                                             vmem_limit_bytes=VMEM_LIMIT_BYTES),
        name="post",
    )(h2d, a2d, mkv, *weights)


def _layer(h2d, mem, pos, freqs, batch, seq, ffn1_norm, ffn1_w_gate, ffn1_w_up, ffn1_w_down, mix_norm, w_in,
           q_norm, w_q_up, kv_norm, w_kv_up, pool_w, pool_scale, w_out, xattn_norm, mem_norm,
           w_mq, w_mkv, w_mo, ffn2_norm, ffn2_w_gate, ffn2_w_up, ffn2_w_down, out_norm):
    row = lambda g: g.reshape(1, -1).astype(F32)
    h1p, q, k, v = _pre(h2d, pos, freqs, row(ffn1_norm), row(mix_norm), row(q_norm), row(kv_norm), row(pool_scale),
                        ffn1_w_gate, ffn1_w_up, ffn1_w_down, w_out, w_in, w_q_up, w_kv_up, pool_w, batch, seq)
    a, mkv, post_weights = _mla_attention(
        q, k, v, mem, row(mem_norm), w_mkv.astype(BF16),
        [(w_out, ATT_WIDTH), (w_mq, D_MODEL), (w_mo, D_MODEL),
         (ffn2_w_gate, D_MODEL), (ffn2_w_up, D_MODEL), (ffn2_w_down, D_FF)])
    return _post(h1p, a.reshape(batch * seq, ATT_WIDTH), mkv, row(xattn_norm), row(ffn2_norm), row(out_norm),
                 *post_weights, seq)


def kernel(x, mem, positions, ffn1_norm, ffn1_w_gate, ffn1_w_up, ffn1_w_down, mix_norm, w_in, q_norm, w_q_up,
           kv_norm, w_kv_up, pool_w, pool_scale, w_out, xattn_norm, mem_norm, w_mq, w_mkv, w_mo, ffn2_norm,
           ffn2_w_gate, ffn2_w_up, ffn2_w_down, final_norm):
    batch, seq, d_model = x.shape
    depth = ffn1_norm.shape[0]
    assert d_model == D_MODEL and depth == 1
    assert seq % TOKEN_TILE == 0 and seq % POST_TILE == 0 and seq % Q_TILE == 0
    freqs = 1.0 / (ROPE_BASE ** (jnp.arange(0, ROPE_DIM, 2, dtype=F32) / ROPE_DIM))
    layer = (ffn1_norm, ffn1_w_gate, ffn1_w_up, ffn1_w_down, mix_norm, w_in, q_norm, w_q_up, kv_norm, w_kv_up,
             pool_w, pool_scale, w_out, xattn_norm, mem_norm, w_mq, w_mkv, w_mo, ffn2_norm, ffn2_w_gate,
             ffn2_w_up, ffn2_w_down)
    out = _layer(x.reshape(batch * seq, d_model), mem, positions,
                 freqs.reshape(ROPE_HALF, 1), batch, seq, *[w[0] for w in layer], final_norm)
    return out.reshape(batch, seq, d_model)
```

```python
import functools

import jax
import jax.numpy as jnp
from jax import lax
from jax.experimental import pallas as pl
from jax.experimental.pallas import tpu as pltpu

F32 = jnp.float32
BF16 = jnp.bfloat16

D_MODEL = 1024
N_HEADS = 4
HEAD_DIM = 128
ROPE_DIM = 64
ROPE_HALF = ROPE_DIM // 2
Q_RANK = 256
KV_RANK = 128
ATT_WIDTH = N_HEADS * HEAD_DIM
POOL_WIDTH = D_MODEL - ATT_WIDTH
POOL_WINDOWS = (2, 4, 8, 16)
POOL_CH = POOL_WIDTH // len(POOL_WINDOWS)
POOL_HALO = 16
D_FF = 2816
N_MEM = 256
MEM_HEADS = 4
MEM_HEAD_DIM = D_MODEL // MEM_HEADS
ROPE_BASE = 10000.0
RMS_EPS = 1e-6
ATT_SCALE = (HEAD_DIM + ROPE_DIM) ** -0.5
LOG2_E = 1.4426950408889634
Q_PRESCALE = ATT_SCALE * LOG2_E
MEM_SCALE = MEM_HEAD_DIM ** -0.5

LANES = 128
BF16_SUBLANES = 16
MXU_DIM = 256
VMEM_LIMIT_BYTES = 56 * 1024 * 1024

TOKEN_TILE = 512
POST_TILE = 1024
POST_ROW_GROUP = 512
FF_CHUNK = MXU_DIM
Q_TILE = 512
KV_TILE = 512
QK_DIM = 2 * HEAD_DIM
CAST_ROWS_WIDE = 128
CAST_ROWS = 256
CAST_SLOTS = 4
STAT_ROWS = 8
MAX_SINGLE_PASS_JUMP = 32.0


def _rmsnorm(x, g):
    ms = jnp.mean(x * x, axis=-1, keepdims=True)
    return (x * lax.rsqrt(ms + RMS_EPS)) * g


def _dot(a, b):
    return jnp.dot(a, b, preferred_element_type=F32)


def _dot_nt(a, b):
    return lax.dot_general(a, b, (((1,), (1,)), ((), ())), preferred_element_type=F32)


def _swiglu(xn, wg_ref, wu_ref, wd_ref):
    n_chunks = D_FF // FF_CHUNK
    cols = lambda c: slice(c * FF_CHUNK, (c + 1) * FF_CHUNK)
    gate_up = lambda c: (_dot(xn, wg_ref[:, cols(c)]), _dot(xn, wu_ref[:, cols(c)]))
    acts = []
    for c in range(n_chunks):
        g, u = gate_up(c)
        acts.append(((g / (1.0 + jnp.exp(-g))) * u).astype(BF16))
    return _dot(jnp.concatenate(acts, axis=1), wd_ref[...])


def _row_jobs(src, src_row0, n_rows, dst, chunk):
    assert n_rows % chunk == 0
    return [(src, src_row0 + r, dst, r) for r in range(0, n_rows, chunk)]


def _cast_weights_into_vmem(jobs, stage, sems):
    slots, rows = stage.shape[0], stage.shape[1]
    ahead = slots - 1

    def copy(j):
        src, r0, _, _ = jobs[j]
        return pltpu.make_async_copy(src.at[pl.ds(r0, rows), :], stage.at[j % slots], sems.at[j % slots])

    for j in range(min(ahead, len(jobs))):
        copy(j).start()
    for j in range(len(jobs)):
        if j + ahead < len(jobs):
            copy(j + ahead).start()
        copy(j).wait()
        _, _, dst, d0 = jobs[j]
        dst[d0:d0 + rows, :] = stage[j % slots].astype(BF16)


def _cast_scratch(wide_rows, narrow_rows):
    return [pltpu.VMEM((CAST_SLOTS, wide_rows, D_FF), F32), pltpu.VMEM((CAST_SLOTS, narrow_rows, D_MODEL), F32),
            pltpu.SemaphoreType.DMA((CAST_SLOTS,)), pltpu.SemaphoreType.DMA((CAST_SLOTS,))]


def _rope(x, cos4, sin4):
    return x * cos4 + pltpu.roll(x, 2 * ROPE_HALF, 1) * sin4


def _pre_body(tiles_per_seq, n_tiles,
              x_ref, pos_ref, freq_ref, n1_ref, mixn_ref, win_ref, qn_ref, wq_ref, kvn_ref, wk_ref, wv_ref,
              poolw_ref, pscale_ref, wg_hbm, wu_hbm, wd_hbm, wout_hbm,
              h_ref, q_ref, k_ref, v_ref,
              ext_ref, z_ref, wg_ref, wu_ref, wd_ref, woutp_ref,
              stage_wide, stage, sem_wide, sem):
    step = pl.program_id(0)
    tm = x_ref.shape[0]

    @pl.when(step == 0)
    def _():
        _cast_weights_into_vmem(_row_jobs(wg_hbm, 0, D_MODEL, wg_ref, CAST_ROWS_WIDE)
                                + _row_jobs(wu_hbm, 0, D_MODEL, wu_ref, CAST_ROWS_WIDE), stage_wide, sem_wide)
        _cast_weights_into_vmem(_row_jobs(wd_hbm, 0, D_FF, wd_ref, CAST_ROWS)
                                + _row_jobs(wout_hbm, ATT_WIDTH, POOL_WIDTH, woutp_ref, CAST_ROWS), stage, sem)
        ext_ref[...] = jnp.zeros(ext_ref.shape, F32)

    def ffn_and_project():
        x = x_ref[...]
        xn = _rmsnorm(x, n1_ref[...]).astype(BF16)
        h1 = x + 0.5 * _swiglu(xn, wg_ref, wu_ref, wd_ref)
        un = _rmsnorm(h1, mixn_ref[...]).astype(BF16)
        z = _dot(un, win_ref[...])
        batch_row = step // tiles_per_seq
        ang = freq_ref[...] * pos_ref[pl.ds(batch_row, 1), :].astype(F32)
        return h1, z, jnp.cos(ang), jnp.sin(ang)

    def mixer_inputs(h1, z, cos, sin):
        seq_tile = step % tiles_per_seq
        z_ref[...] = z

        cos4_t = jnp.concatenate([cos, cos, cos, cos], axis=0)
        sin4_t = jnp.concatenate([-sin, -sin, sin, sin], axis=0)

        qn = _rmsnorm(z_ref[:, :Q_RANK], qn_ref[...]).astype(BF16)
        q_t = _dot_nt(wq_ref[...], qn)
        for h in range(N_HEADS):
            base = h * QK_DIM
            q_ref[0, h, :HEAD_DIM, :] = (q_t[base:base + HEAD_DIM] * Q_PRESCALE).astype(BF16)
            pe = q_t[base + HEAD_DIM:base + QK_DIM]
            pe = pe * cos4_t + pltpu.roll(pe, 2 * ROPE_HALF, 0) * sin4_t
            q_ref[0, h, HEAD_DIM:, :] = (pe * Q_PRESCALE).astype(BF16)

        kvn = _rmsnorm(z_ref[:, Q_RANK:Q_RANK + KV_RANK], kvn_ref[...]).astype(BF16)
        k_nope = _dot(kvn, wk_ref[...])
        v_t = _dot_nt(wv_ref[...], kvn)
        k_pe = _rope(z_ref[:, Q_RANK + KV_RANK:Q_RANK + KV_RANK + LANES], cos4_t.T, sin4_t.T).astype(BF16)
        for h in range(N_HEADS):
            k_ref[0, h, :, :HEAD_DIM] = k_nope[:, h * HEAD_DIM:(h + 1) * HEAD_DIM].astype(BF16)
            k_ref[0, h, :, HEAD_DIM:] = k_pe
            v_ref[0, h, 0] = v_t[h * HEAD_DIM:(h + 1) * HEAD_DIM].astype(BF16)

        zp = z_ref[:, D_MODEL - POOL_WIDTH:]
        ext_ref[0:POOL_HALO, :] = jnp.where(seq_tile == 0, 0.0, ext_ref[0:POOL_HALO, :])
        ext_ref[POOL_HALO:, :] = zp
        t1 = (seq_tile * tm + 1 + lax.broadcasted_iota(jnp.int32, (tm, 1), 0)).astype(F32)
        assert POOL_WINDOWS == (2, 4, 8, 16) and POOL_HALO >= sum(POOL_WINDOWS) // 2
        level = ext_ref[...]
        diffs = []
        for g, w in enumerate(POOL_WINDOWS):
            level = level + pltpu.roll(level, w // 2, 0)
            mean = level[POOL_HALO:, :POOL_CH] / jnp.minimum(t1, float(w))
            diffs.append(mean - zp[:, g * POOL_CH:(g + 1) * POOL_CH])
            if g + 1 < len(POOL_WINDOWS):
                level = level[:, POOL_CH:]
        ext_ref[0:POOL_HALO, :] = ext_ref[tm:tm + POOL_HALO, :]

        y01 = _dot(jnp.concatenate(diffs[:2], axis=1).astype(BF16), poolw_ref[0])
        y23 = _dot(jnp.concatenate(diffs[2:], axis=1).astype(BF16), poolw_ref[1])
        p = jnp.concatenate([y01, y23], axis=1) * pscale_ref[...]
        h_ref[...] = h1 + _dot(p.astype(BF16), woutp_ref[...])

    mixer_inputs(*ffn_and_project())


def _const_spec(shape):
    nd = len(shape)
    return pl.BlockSpec(shape, lambda *_: (0,) * nd, pipeline_mode=pl.Buffered(1))


def _pre(x2d, pos, freqs, n1, mixn, win, qn, wq, kvn, wk, wv, poolw, pscale, wg, wu, wd, wout, batch, seq):
    tm = TOKEN_TILE
    assert tm == KV_TILE
    n_tok = x2d.shape[0]
    tiles_per_seq = seq // tm
    n_tiles = n_tok // tm
    bt = lambda i: (i // tiles_per_seq, i % tiles_per_seq)
    x_spec = pl.BlockSpec((tm, D_MODEL), lambda i: (i, 0))
    h_spec = pl.BlockSpec((tm, D_MODEL), lambda i: (i, 0))
    pos_spec = pl.BlockSpec((batch, tm), lambda i: (0, i % tiles_per_seq))
    q_spec = pl.BlockSpec((1, N_HEADS, QK_DIM, tm), lambda i: (bt(i)[0], 0, 0, bt(i)[1]))
    k_spec = pl.BlockSpec((1, N_HEADS, tm, QK_DIM), lambda i: (bt(i)[0], 0, bt(i)[1], 0))
    v_spec = pl.BlockSpec((1, N_HEADS, 1, HEAD_DIM, tm), lambda i: (bt(i)[0], 0, bt(i)[1], 0, 0))
    weights = (n1, mixn, win, qn, wq, kvn, wk, wv, poolw, pscale)
    hbm_weights = (wg, wu, wd, wout)
    return pl.pallas_call(
        functools.partial(_pre_body, tiles_per_seq, n_tiles),
        grid=(n_tiles,),
        in_specs=[x_spec, pos_spec, _const_spec(freqs.shape)] + [_const_spec(w.shape) for w in weights]
                 + [pl.BlockSpec(memory_space=pl.ANY)] * len(hbm_weights),
        out_specs=[h_spec, q_spec, k_spec, v_spec],
        out_shape=[jax.ShapeDtypeStruct((n_tok, D_MODEL), F32),
                   jax.ShapeDtypeStruct((batch, N_HEADS, QK_DIM, seq), BF16),
                   jax.ShapeDtypeStruct((batch, N_HEADS, seq, QK_DIM), BF16),
                   jax.ShapeDtypeStruct((batch, N_HEADS, tiles_per_seq, HEAD_DIM, tm), BF16)],
        scratch_shapes=[pltpu.VMEM((tm + POOL_HALO, POOL_WIDTH), F32),
                        pltpu.VMEM((tm, D_MODEL), F32),
                        pltpu.VMEM((D_MODEL, D_FF), BF16), pltpu.VMEM((D_MODEL, D_FF), BF16),
                        pltpu.VMEM((D_FF, D_MODEL), BF16), pltpu.VMEM((POOL_WIDTH, D_MODEL), BF16)]
                       + _cast_scratch(CAST_ROWS_WIDE, CAST_ROWS),
        compiler_params=pltpu.CompilerParams(dimension_semantics=("arbitrary",),
                                             vmem_limit_bytes=VMEM_LIMIT_BYTES),
        name="pre",
    )(x2d, pos, freqs, *weights, *hbm_weights)


def _attn_body(n_cast, q_ref, k_ref, v_ref, mem_ref, memn_ref, wmkv_ref, *refs):
    cast_in, o_ref, mkv_ref = refs[:n_cast], refs[n_cast], refs[n_cast + 1]
    cast_out = refs[n_cast + 2:2 * n_cast + 2]
    m_ref, l_ref, acc_ref, gap_ref = refs[2 * n_cast + 2:]
    def cast_slices():
        for src, dst in zip(cast_in, cast_out):
            dst[...] = src[...].astype(BF16)

    qi = pl.program_id(1)

    @pl.when(qi == 0)
    def _():
        mn = _rmsnorm(mem_ref[0], memn_ref[...]).astype(BF16)
        mkv_ref[0] = _dot(mn, wmkv_ref[...]).astype(BF16)

    tq = q_ref.shape[3]
    tk = KV_TILE
    half = tk // 2

    def scores(h, c):
        return _dot(k_ref[0, h, pl.ds(pl.multiple_of(c * tk, tk), tk), :], q_ref[0, h])


    def init_with_diagonal():
        causal = (lax.broadcasted_iota(jnp.int32, (half, half), 0)
                  <= lax.broadcasted_iota(jnp.int32, (half, half), 1))
        start = pl.multiple_of(qi * tk, tk)
        s_lo = [_dot(k_ref[0, h, pl.ds(start, half), :], q_ref[0, h]) for h in range(N_HEADS)]
        s_hi = [_dot(k_ref[0, h, pl.ds(start + half, half), :], q_ref[0, h, :, half:]) for h in range(N_HEADS)]
        probs = []
        for h in range(N_HEADS):
            s_ll = jnp.where(causal, s_lo[h][:, :half], -jnp.inf)
            s_lu = s_lo[h][:, half:]
            s_uu = jnp.where(causal, s_hi[h], -jnp.inf)
            m_l = jnp.max(s_ll, axis=0, keepdims=True)
            m_u = jnp.maximum(jnp.max(s_lu, axis=0, keepdims=True), jnp.max(s_uu, axis=0, keepdims=True))
            p_ll = jnp.exp2(s_ll - m_l)
            p_lu = jnp.exp2(s_lu - m_u)
            p_uu = jnp.exp2(s_uu - m_u)
            l_l = jnp.sum(p_ll, axis=0, keepdims=True)
            l_u = jnp.sum(p_lu, axis=0, keepdims=True) + jnp.sum(p_uu, axis=0, keepdims=True)
            m_ref[h, :, :half] = jnp.broadcast_to(m_l, (STAT_ROWS, half))
            m_ref[h, :, half:] = jnp.broadcast_to(m_u, (STAT_ROWS, half))
            l_ref[h, :, :half] = jnp.broadcast_to(l_l, (STAT_ROWS, half))
            l_ref[h, :, half:] = jnp.broadcast_to(l_u, (STAT_ROWS, half))
            probs.append((jnp.concatenate([p_ll, p_lu], axis=1).astype(BF16), p_uu.astype(BF16)))
        for h in range(N_HEADS):
            v_t = v_ref[0, h, qi]
            acc_l = _dot(v_t[:, :half], probs[h][0])
            acc_u = _dot(v_t[:, half:], probs[h][1])
            acc_ref[h, :, :half] = acc_l[:, :half]
            acc_ref[h, :, half:] = acc_l[:, half:] + acc_u

    def update_single_pass(blocks):
        s_all = [[scores(h, c) for c in blocks] for h in range(N_HEADS)]
        betas, probs = [], []
        for h in range(N_HEADS):
            m_old = m_ref[h, 0:1, :]
            ps = [jnp.exp2(s - m_old) for s in s_all[h]]
            m_blk = jnp.max(s_all[h][0], axis=0, keepdims=True)
            for s in s_all[h][1:]:
                m_blk = jnp.maximum(m_blk, jnp.max(s, axis=0, keepdims=True))
            m_new = jnp.maximum(m_old, m_blk)
            beta = jnp.exp2(m_old - m_new)
            l_new = l_ref[h, 0:1, :]
            for p in ps:
                l_new = l_new + jnp.sum(p, axis=0, keepdims=True)
            m_ref[h] = jnp.broadcast_to(m_new, (STAT_ROWS, tq))
            l_ref[h] = jnp.broadcast_to(l_new * beta, (STAT_ROWS, tq))
            gap_ref[h] = jnp.maximum(gap_ref[h], jnp.broadcast_to(m_blk - m_old, (STAT_ROWS, tq)))
            betas.append(beta)
            probs.append([p.astype(BF16) for p in ps])
        for h in range(N_HEADS):
            acc = acc_ref[h]
            for c, p in zip(blocks, probs[h]):
                acc = acc + _dot(v_ref[0, h, c], p)
            acc_ref[h] = acc * betas[h]

    def update_two_pass(c):
        s_all = [scores(h, c) for h in range(N_HEADS)]
        alphas, probs = [], []
        for h in range(N_HEADS):
            m_old = m_ref[h, 0:1, :]
            m_new = jnp.maximum(m_old, jnp.max(s_all[h], axis=0, keepdims=True))
            alpha = jnp.exp2(m_old - m_new)
            p = jnp.exp2(s_all[h] - m_new)
            l_new = alpha * l_ref[h, 0:1, :] + jnp.sum(p, axis=0, keepdims=True)
            m_ref[h] = jnp.broadcast_to(m_new, (STAT_ROWS, tq))
            l_ref[h] = jnp.broadcast_to(l_new, (STAT_ROWS, tq))
            alphas.append(alpha)
            probs.append(p.astype(BF16))
        for h in range(N_HEADS):
            acc_ref[h] = alphas[h] * acc_ref[h] + _dot(v_ref[0, h, c], probs[h])

    @pl.when(qi % 2 == 0)
    def _():
        cast_slices()
        init_with_diagonal()
        gap_ref[...] = jnp.zeros(gap_ref.shape, F32)

    @pl.when(qi % 2 == 1)
    def _():
        cast_slices()
        init_with_diagonal()
        gap_ref[...] = jnp.zeros(gap_ref.shape, F32)
        update_single_pass([qi - 1])

    def pair(j, _):
        update_single_pass([2 * j, 2 * j + 1])
        return 0

    lax.fori_loop(0, lax.shift_right_logical(qi, 1), pair, 0)

    def write_output():
        for h in range(N_HEADS):
            o_ref[0, :, h * HEAD_DIM:(h + 1) * HEAD_DIM] = (acc_ref[h] / l_ref[h, 0:1, :]).T.astype(BF16)

    worst = gap_ref[0]
    for h in range(1, N_HEADS):
        worst = jnp.maximum(worst, gap_ref[h])
    write_output()

    @pl.when(jnp.max(worst) > MAX_SINGLE_PASS_JUMP)
    def _():
        init_with_diagonal()

        def one(c, _):
            update_two_pass(c)
            return 0

        lax.fori_loop(0, qi, one, 0)
        write_output()


def _cast_block_specs(w, n_rows, steps, n_q):
    cols = w.shape[1]
    per_step = n_rows // steps
    blk = per_step if n_rows % steps == 0 and per_step % BF16_SUBLANES == 0 else LANES
    n_blk = n_rows // blk
    assert n_rows % blk == 0 and n_blk <= steps
    index = lambda b, i: (jnp.minimum(b * n_q + i, n_blk - 1), 0)
    return (pl.BlockSpec((blk, cols), index), pl.BlockSpec((blk, cols), index),
            jax.ShapeDtypeStruct((n_rows, cols), BF16))


def _mla_attention(q, k, v, mem, mem_norm, w_mkv, cast_jobs):
    batch, _, seq, _ = k.shape
    assert Q_TILE == KV_TILE
    n_q = seq // Q_TILE
    cast_specs = [_cast_block_specs(w, rows, batch * n_q, n_q) for w, rows in cast_jobs]
    outs = pl.pallas_call(
        functools.partial(_attn_body, len(cast_jobs)),
        grid=(batch, n_q),
        in_specs=[pl.BlockSpec((1, N_HEADS, QK_DIM, Q_TILE), lambda b, i: (b, 0, 0, i)),
                  pl.BlockSpec((1, N_HEADS, seq, QK_DIM), lambda b, i: (b, 0, 0, 0)),
                  pl.BlockSpec((1, N_HEADS, seq // KV_TILE, HEAD_DIM, KV_TILE), lambda b, i: (b, 0, 0, 0, 0)),
                  pl.BlockSpec((1, N_MEM, D_MODEL), lambda b, i: (b, 0, 0)),
                  _const_spec(mem_norm.shape), _const_spec(w_mkv.shape)]
                 + [c[0] for c in cast_specs],
        out_specs=[pl.BlockSpec((1, Q_TILE, ATT_WIDTH), lambda b, i: (b, i, 0)),
                   pl.BlockSpec((1, N_MEM, 2 * D_MODEL), lambda b, i: (b, 0, 0))] + [c[1] for c in cast_specs],
        out_shape=[jax.ShapeDtypeStruct((batch, seq, ATT_WIDTH), BF16),
                   jax.ShapeDtypeStruct((batch, N_MEM, 2 * D_MODEL), BF16)] + [c[2] for c in cast_specs],
        scratch_shapes=[pltpu.VMEM((N_HEADS, STAT_ROWS, Q_TILE), F32),
                        pltpu.VMEM((N_HEADS, STAT_ROWS, Q_TILE), F32),
                        pltpu.VMEM((N_HEADS, HEAD_DIM, Q_TILE), F32),
                        pltpu.VMEM((N_HEADS, STAT_ROWS, Q_TILE), F32)],
        compiler_params=pltpu.CompilerParams(dimension_semantics=("arbitrary", "arbitrary"),
                                             vmem_limit_bytes=VMEM_LIMIT_BYTES),
        name="mla_attn",
    )(q, k, v, mem, mem_norm, w_mkv, *[w for w, _ in cast_jobs])
    return outs[0], outs[1], outs[2:]


def _post_body(h_ref, a_ref, mkv_ref, xn_ref, n2_ref, fn_ref, wouta_ref, wmq_ref, wmo_ref, wg_ref, wu_ref, wd_ref,
               out_ref):
    def mix(rows):
        h2 = h_ref[rows, :] + _dot(a_ref[rows, :], wouta_ref[...])

        hn = _rmsnorm(h2, xn_ref[...]).astype(BF16)
        q = (_dot(hn, wmq_ref[...]) * MEM_SCALE).astype(BF16)
        cols = [slice(h * MEM_HEAD_DIM, (h + 1) * MEM_HEAD_DIM) for h in range(MEM_HEADS)]
        scores = [_dot_nt(q[:, c], mkv_ref[0, :, c]) for c in cols]
        probs = []
        for s in scores:
            e = jnp.exp(s - jnp.max(s, axis=-1, keepdims=True))
            probs.append((e / jnp.sum(e, axis=-1, keepdims=True)).astype(BF16))
        heads = [_dot(p, mkv_ref[0, :, D_MODEL + h * MEM_HEAD_DIM:D_MODEL + (h + 1) * MEM_HEAD_DIM]).astype(BF16)
                 for h, p in enumerate(probs)]
        return h2 + _dot(jnp.concatenate(heads, axis=1), wmo_ref[...])

    def ffn(rows, h3):
        xn = _rmsnorm(h3, n2_ref[...]).astype(BF16)
        h4 = h3 + 0.5 * _swiglu(xn, wg_ref, wu_ref, wd_ref)
        out_ref[rows, :] = _rmsnorm(h4, fn_ref[...])

    tm = h_ref.shape[0]
    groups = [slice(r, r + POST_ROW_GROUP) for r in range(0, tm, POST_ROW_GROUP)]
    mixed = [mix(rows) for rows in groups]
    for rows, h3 in zip(groups, mixed):
        ffn(rows, h3)


def _post(h2d, a2d, mkv, xn, n2, fn, wouta, wmq, wmo, wg, wu, wd, seq):
    tm = POST_TILE
    n_tok = h2d.shape[0]
    tiles_per_seq = seq // tm
    weights = (xn, n2, fn, wouta, wmq, wmo, wg, wu, wd)
    return pl.pallas_call(
        _post_body,
        grid=(n_tok // tm,),
        in_specs=[pl.BlockSpec((tm, D_MODEL), lambda i: (i, 0)),
                  pl.BlockSpec((tm, ATT_WIDTH), lambda i: (i, 0)),
                  pl.BlockSpec((1, N_MEM, 2 * D_MODEL), lambda i: (i // tiles_per_seq, 0, 0))]
                 + [_const_spec(w.shape) for w in weights],
        out_specs=pl.BlockSpec((tm, D_MODEL), lambda i: (i, 0)),
        out_shape=jax.ShapeDtypeStruct((n_tok, D_MODEL), F32),
        compiler_params=pltpu.CompilerParams(dimension_semantics=("arbitrary",),
                                             vmem_limit_bytes=VMEM_LIMIT_BYTES),
        name="post",
    )(h2d, a2d, mkv, *weights)


def _pad_rope_cols(w):
    zeros = jnp.zeros(w.shape[:-1] + (ROPE_HALF,), w.dtype)
    return jnp.concatenate([w[..., :ROPE_HALF], zeros, w[..., ROPE_HALF:], zeros], axis=-1)


def _block_diag2(a, b):
    za = jnp.zeros((a.shape[0], b.shape[1]), a.dtype)
    zb = jnp.zeros((b.shape[0], a.shape[1]), a.dtype)
    return jnp.concatenate([jnp.concatenate([a, za], axis=1), jnp.concatenate([zb, b], axis=1)], axis=0)


def _layer(h2d, mem, pos, freqs, batch, seq, ffn1_norm, ffn1_w_gate, ffn1_w_up, ffn1_w_down, mix_norm, w_in,
           q_norm, w_q_up, kv_norm, w_kv_up, pool_w, pool_scale, w_out, xattn_norm, mem_norm,
           w_mq, w_mkv, w_mo, ffn2_norm, ffn2_w_gate, ffn2_w_up, ffn2_w_down, out_norm):
    row = lambda g: g.reshape(1, -1).astype(F32)
    rope_lo = Q_RANK + KV_RANK
    w_in_p = jnp.concatenate([w_in[:, :rope_lo], _pad_rope_cols(w_in[:, rope_lo:rope_lo + ROPE_DIM]),
                              w_in[:, rope_lo + ROPE_DIM:]], axis=1).astype(BF16)
    wq = w_q_up.reshape(Q_RANK, N_HEADS, HEAD_DIM + ROPE_DIM)
    wq_p = jnp.concatenate([wq[..., :HEAD_DIM], _pad_rope_cols(wq[..., HEAD_DIM:])], axis=-1)
    wq_t = wq_p.reshape(Q_RANK, N_HEADS * QK_DIM).T.astype(BF16)
    wkv = w_kv_up.reshape(KV_RANK, N_HEADS, 2 * HEAD_DIM)
    wk = wkv[..., :HEAD_DIM].reshape(KV_RANK, ATT_WIDTH).astype(BF16)
    wv_t = wkv[..., HEAD_DIM:].reshape(KV_RANK, ATT_WIDTH).T.astype(BF16)
    poolw = jnp.stack([_block_diag2(pool_w[0], pool_w[1]), _block_diag2(pool_w[2], pool_w[3])]).astype(BF16)

    h1p, q, k, v = _pre(h2d, pos, freqs, row(ffn1_norm), row(mix_norm), w_in_p, row(q_norm), wq_t, row(kv_norm),
                        wk, wv_t, poolw, row(pool_scale), ffn1_w_gate, ffn1_w_up, ffn1_w_down, w_out, batch, seq)
    a, mkv, post_weights = _mla_attention(
        q, k, v, mem, row(mem_norm), w_mkv.astype(BF16),
        [(w_out, ATT_WIDTH), (w_mq, D_MODEL), (w_mo, D_MODEL),
         (ffn2_w_gate, D_MODEL), (ffn2_w_up, D_MODEL), (ffn2_w_down, D_FF)])
    return _post(h1p, a.reshape(batch * seq, ATT_WIDTH), mkv, row(xattn_norm), row(ffn2_norm), row(out_norm),
                 *post_weights, seq)


def kernel(x, mem, positions, ffn1_norm, ffn1_w_gate, ffn1_w_up, ffn1_w_down, mix_norm, w_in, q_norm, w_q_up,
           kv_norm, w_kv_up, pool_w, pool_scale, w_out, xattn_norm, mem_norm, w_mq, w_mkv, w_mo, ffn2_norm,
           ffn2_w_gate, ffn2_w_up, ffn2_w_down, final_norm):
    batch, seq, d_model = x.shape
    depth = ffn1_norm.shape[0]
    assert d_model == D_MODEL and depth == 1
    assert seq % TOKEN_TILE == 0 and seq % POST_TILE == 0 and seq % Q_TILE == 0
    freqs = 1.0 / (ROPE_BASE ** (jnp.arange(0, ROPE_DIM, 2, dtype=F32) / ROPE_DIM))
    layer = (ffn1_norm, ffn1_w_gate, ffn1_w_up, ffn1_w_down, mix_norm, w_in, q_norm, w_q_up, kv_norm, w_kv_up,
             pool_w, pool_scale, w_out, xattn_norm, mem_norm, w_mq, w_mkv, w_mo, ffn2_norm, ffn2_w_gate,
             ffn2_w_up, ffn2_w_down)
    out = _layer(x.reshape(batch * seq, d_model), mem, positions,
                 freqs.reshape(ROPE_HALF, 1), batch, seq, *[w[0] for w in layer], final_norm)
    return out.reshape(batch, seq, d_model)
```

```python
import functools

import jax
import jax.numpy as jnp
from jax import lax
from jax.experimental import pallas as pl
from jax.experimental.pallas import tpu as pltpu

F32 = jnp.float32
BF16 = jnp.bfloat16

D_MODEL = 1024
N_HEADS = 4
HEAD_DIM = 128
ROPE_DIM = 64
ROPE_HALF = ROPE_DIM // 2
Q_RANK = 256
KV_RANK = 128
ATT_WIDTH = N_HEADS * HEAD_DIM
POOL_WIDTH = D_MODEL - ATT_WIDTH
POOL_WINDOWS = (2, 4, 8, 16)
POOL_CH = POOL_WIDTH // len(POOL_WINDOWS)
POOL_HALO = 16
D_FF = 2816
N_MEM = 256
MEM_HEADS = 4
MEM_HEAD_DIM = D_MODEL // MEM_HEADS
ROPE_BASE = 10000.0
RMS_EPS = 1e-6
ATT_SCALE = (HEAD_DIM + ROPE_DIM) ** -0.5
LOG2_E = 1.4426950408889634
Q_PRESCALE = ATT_SCALE * LOG2_E
MEM_SCALE = MEM_HEAD_DIM ** -0.5

LANES = 128
BF16_SUBLANES = 16
MXU_DIM = 256
VMEM_LIMIT_BYTES = 56 * 1024 * 1024

TOKEN_TILE = 512
POST_TILE = 1024
POST_ROW_GROUP = 512
FF_CHUNK = MXU_DIM
Q_TILE = 512
KV_TILE = 512
QK_DIM = 2 * HEAD_DIM
CAST_ROWS_WIDE = 128
CAST_ROWS = 256
CAST_SLOTS = 4
STAT_ROWS = 8
MAX_SINGLE_PASS_JUMP = 32.0


def _rmsnorm(x, g):
    ms = jnp.mean(x * x, axis=-1, keepdims=True)
    return (x * lax.rsqrt(ms + RMS_EPS)) * g


def _dot(a, b):
    return jnp.dot(a, b, preferred_element_type=F32)


def _dot_nt(a, b):
    return lax.dot_general(a, b, (((1,), (1,)), ((), ())), preferred_element_type=F32)


def _swiglu(xn, wg_ref, wu_ref, wd_ref):
    bounds = list(range(0, D_FF, FF_CHUNK)) + [D_FF]
    acts = []
    for lo, hi in zip(bounds[:-1], bounds[1:]):
        g = _dot(xn, wg_ref[:, lo:hi])
        u = _dot(xn, wu_ref[:, lo:hi])
        acts.append(((g / (1.0 + jnp.exp(-g))) * u).astype(BF16))
    return _dot(jnp.concatenate(acts, axis=1), wd_ref[...])


def _row_jobs(src, src_row0, n_rows, dst, chunk):
    assert n_rows % chunk == 0
    return [(src, src_row0 + r, dst, r) for r in range(0, n_rows, chunk)]


def _cast_weights_into_vmem(jobs, stage, sems):
    slots, rows = stage.shape[0], stage.shape[1]
    ahead = slots - 1

    def copy(j):
        src, r0, _, _ = jobs[j]
        return pltpu.make_async_copy(src.at[pl.ds(r0, rows), :], stage.at[j % slots], sems.at[j % slots])

    for j in range(min(ahead, len(jobs))):
        copy(j).start()
    for j in range(len(jobs)):
        if j + ahead < len(jobs):
            copy(j + ahead).start()
        copy(j).wait()
        _, _, dst, d0 = jobs[j]
        dst[d0:d0 + rows, :] = stage[j % slots].astype(BF16)


def _cast_scratch(wide_rows, narrow_rows):
    return [pltpu.VMEM((CAST_SLOTS, wide_rows, D_FF), F32), pltpu.VMEM((CAST_SLOTS, narrow_rows, D_MODEL), F32),
            pltpu.SemaphoreType.DMA((CAST_SLOTS,)), pltpu.SemaphoreType.DMA((CAST_SLOTS,))]


def _rope(x, cos4, sin4):
    return x * cos4 + pltpu.roll(x, 2 * ROPE_HALF, 1) * sin4


def _pre_body(tiles_per_seq, n_tiles,
              x_ref, pos_ref, freq_ref, n1_ref, mixn_ref, win_ref, qn_ref, wq_ref, kvn_ref, wk_ref, wv_ref,
              poolw_ref, pscale_ref, wg_hbm, wu_hbm, wd_hbm, wout_hbm,
              h_ref, q_ref, k_ref, v_ref,
              ext_ref, h1_ref, z_ref, cos_ref, sin_ref, wg_ref, wu_ref, wd_ref, woutp_ref,
              stage_wide, stage, sem_wide, sem):
    step = pl.program_id(0)
    tm = x_ref.shape[0]

    @pl.when(step == 0)
    def _():
        _cast_weights_into_vmem(_row_jobs(wg_hbm, 0, D_MODEL, wg_ref, CAST_ROWS_WIDE)
                                + _row_jobs(wu_hbm, 0, D_MODEL, wu_ref, CAST_ROWS_WIDE), stage_wide, sem_wide)
        _cast_weights_into_vmem(_row_jobs(wd_hbm, 0, D_FF, wd_ref, CAST_ROWS)
                                + _row_jobs(wout_hbm, ATT_WIDTH, POOL_WIDTH, woutp_ref, CAST_ROWS), stage, sem)
        h1_ref[...] = jnp.zeros(h1_ref.shape, F32)
        z_ref[...] = jnp.zeros(z_ref.shape, F32)
        ext_ref[...] = jnp.zeros(ext_ref.shape, F32)
        cos_ref[...] = jnp.zeros(cos_ref.shape, F32)
        sin_ref[...] = jnp.zeros(sin_ref.shape, F32)

    def ffn_and_project():
        x = x_ref[...]
        xn = _rmsnorm(x, n1_ref[...]).astype(BF16)
        h1 = x + 0.5 * _swiglu(xn, wg_ref, wu_ref, wd_ref)
        un = _rmsnorm(h1, mixn_ref[...]).astype(BF16)
        z = _dot(un, win_ref[...])
        batch_row = jnp.minimum(step, n_tiles - 1) // tiles_per_seq
        ang = freq_ref[...] * pos_ref[pl.ds(batch_row, 1), :].astype(F32)
        return h1, z, jnp.cos(ang), jnp.sin(ang)

    def mixer_inputs():
        seq_tile = jnp.maximum(step - 1, 0) % tiles_per_seq

        cos = cos_ref[...]
        sin = sin_ref[...]
        cos4_t = jnp.concatenate([cos, cos, cos, cos], axis=0)
        sin4_t = jnp.concatenate([-sin, -sin, sin, sin], axis=0)

        qn = _rmsnorm(z_ref[:, :Q_RANK], qn_ref[...]).astype(BF16)
        q_t = _dot_nt(wq_ref[...], qn)
        for h in range(N_HEADS):
            base = h * QK_DIM
            q_ref[0, h, :HEAD_DIM, :] = (q_t[base:base + HEAD_DIM] * Q_PRESCALE).astype(BF16)
            pe = q_t[base + HEAD_DIM:base + QK_DIM]
            pe = pe * cos4_t + pltpu.roll(pe, 2 * ROPE_HALF, 0) * sin4_t
            q_ref[0, h, HEAD_DIM:, :] = (pe * Q_PRESCALE).astype(BF16)

        kvn = _rmsnorm(z_ref[:, Q_RANK:Q_RANK + KV_RANK], kvn_ref[...]).astype(BF16)
        k_nope = _dot(kvn, wk_ref[...])
        v_t = _dot_nt(wv_ref[...], kvn)
        k_pe = _rope(z_ref[:, Q_RANK + KV_RANK:Q_RANK + KV_RANK + LANES], cos4_t.T, sin4_t.T).astype(BF16)
        for h in range(N_HEADS):
            k_ref[0, h, :, :HEAD_DIM] = k_nope[:, h * HEAD_DIM:(h + 1) * HEAD_DIM].astype(BF16)
            k_ref[0, h, :, HEAD_DIM:] = k_pe
            v_ref[0, h, 0] = v_t[h * HEAD_DIM:(h + 1) * HEAD_DIM].astype(BF16)

        zp = z_ref[:, D_MODEL - POOL_WIDTH:]
        ext_ref[0:POOL_HALO, :] = jnp.where(seq_tile == 0, 0.0, ext_ref[0:POOL_HALO, :])
        ext_ref[POOL_HALO:, :] = zp
        t1 = (seq_tile * tm + 1 + lax.broadcasted_iota(jnp.int32, (tm, 1), 0)).astype(F32)
        assert POOL_WINDOWS == (2, 4, 8, 16) and POOL_HALO >= sum(POOL_WINDOWS) // 2
        level = ext_ref[...]
        diffs = []
        for g, w in enumerate(POOL_WINDOWS):
            level = level + pltpu.roll(level, w // 2, 0)
            mean = level[POOL_HALO:, :POOL_CH] / jnp.minimum(t1, float(w))
            diffs.append(mean - zp[:, g * POOL_CH:(g + 1) * POOL_CH])
            if g + 1 < len(POOL_WINDOWS):
                level = level[:, POOL_CH:]
        ext_ref[0:POOL_HALO, :] = ext_ref[tm:tm + POOL_HALO, :]

        y01 = _dot(jnp.concatenate(diffs[:2], axis=1).astype(BF16), poolw_ref[0])
        y23 = _dot(jnp.concatenate(diffs[2:], axis=1).astype(BF16), poolw_ref[1])
        p = jnp.concatenate([y01, y23], axis=1) * pscale_ref[...]
        h_ref[...] = h1_ref[...] + _dot(p.astype(BF16), woutp_ref[...])

    @pl.when(step < n_tiles)
    def _():
        mixer_inputs()
        h1, z, cos, sin = ffn_and_project()
        h1_ref[...] = h1
        z_ref[...] = z
        cos_ref[...] = cos
        sin_ref[...] = sin

    @pl.when(step == n_tiles)
    def _():
        mixer_inputs()


def _const_spec(shape):
    nd = len(shape)
    return pl.BlockSpec(shape, lambda *_: (0,) * nd, pipeline_mode=pl.Buffered(1))


def _pre(x2d, pos, freqs, n1, mixn, win, qn, wq, kvn, wk, wv, poolw, pscale, wg, wu, wd, wout, batch, seq):
    tm = TOKEN_TILE
    assert tm == KV_TILE
    n_tok = x2d.shape[0]
    tiles_per_seq = seq // tm
    n_tiles = n_tok // tm
    lag = lambda i: jnp.maximum(i - 1, 0)
    bt = lambda i: (lag(i) // tiles_per_seq, lag(i) % tiles_per_seq)
    x_spec = pl.BlockSpec((tm, D_MODEL), lambda i: (jnp.minimum(i, n_tiles - 1), 0))
    h_spec = pl.BlockSpec((tm, D_MODEL), lambda i: (lag(i), 0))
    pos_spec = pl.BlockSpec((batch, tm), lambda i: (0, jnp.minimum(i, n_tiles - 1) % tiles_per_seq))
    q_spec = pl.BlockSpec((1, N_HEADS, QK_DIM, tm), lambda i: (bt(i)[0], 0, 0, bt(i)[1]))
    k_spec = pl.BlockSpec((1, N_HEADS, tm, QK_DIM), lambda i: (bt(i)[0], 0, bt(i)[1], 0))
    v_spec = pl.BlockSpec((1, N_HEADS, 1, HEAD_DIM, tm), lambda i: (bt(i)[0], 0, bt(i)[1], 0, 0))
    weights = (n1, mixn, win, qn, wq, kvn, wk, wv, poolw, pscale)
    hbm_weights = (wg, wu, wd, wout)
    return pl.pallas_call(
        functools.partial(_pre_body, tiles_per_seq, n_tiles),
        grid=(n_tiles + 1,),
        in_specs=[x_spec, pos_spec, _const_spec(freqs.shape)] + [_const_spec(w.shape) for w in weights]
                 + [pl.BlockSpec(memory_space=pl.ANY)] * len(hbm_weights),
        out_specs=[h_spec, q_spec, k_spec, v_spec],
        out_shape=[jax.ShapeDtypeStruct((n_tok, D_MODEL), F32),
                   jax.ShapeDtypeStruct((batch, N_HEADS, QK_DIM, seq), BF16),
                   jax.ShapeDtypeStruct((batch, N_HEADS, seq, QK_DIM), BF16),
                   jax.ShapeDtypeStruct((batch, N_HEADS, tiles_per_seq, HEAD_DIM, tm), BF16)],
        scratch_shapes=[pltpu.VMEM((tm + POOL_HALO, POOL_WIDTH), F32),
                        pltpu.VMEM((tm, D_MODEL), F32), pltpu.VMEM((tm, D_MODEL), F32),
                        pltpu.VMEM((ROPE_HALF, tm), F32), pltpu.VMEM((ROPE_HALF, tm), F32),
                        pltpu.VMEM((D_MODEL, D_FF), BF16), pltpu.VMEM((D_MODEL, D_FF), BF16),
                        pltpu.VMEM((D_FF, D_MODEL), BF16), pltpu.VMEM((POOL_WIDTH, D_MODEL), BF16)]
                       + _cast_scratch(CAST_ROWS_WIDE, CAST_ROWS),
        compiler_params=pltpu.CompilerParams(dimension_semantics=("arbitrary",),
                                             vmem_limit_bytes=VMEM_LIMIT_BYTES),
        name="pre",
    )(x2d, pos, freqs, *weights, *hbm_weights)


def _attn_body(n_cast, q_ref, k_ref, v_ref, mem_ref, memn_ref, wmkv_ref, *refs):
    cast_in, o_ref, mkv_ref = refs[:n_cast], refs[n_cast], refs[n_cast + 1]
    cast_out = refs[n_cast + 2:2 * n_cast + 2]
    m_ref, l_ref, acc_ref, gap_ref = refs[2 * n_cast + 2:]

    def cast_slices():
        for src, dst in zip(cast_in, cast_out):
            dst[...] = src[...].astype(BF16)

    qi = pl.program_id(1)

    @pl.when(qi == 0)
    def _():
        mn = _rmsnorm(mem_ref[0], memn_ref[...]).astype(BF16)
        mkv_ref[0] = _dot(mn, wmkv_ref[...]).astype(BF16)

    tq = q_ref.shape[3]
    tk = KV_TILE
    half = tk // 2

    def scores(h, c):
        return _dot(k_ref[0, h, pl.ds(pl.multiple_of(c * tk, tk), tk), :], q_ref[0, h])


    def init_with_diagonal():
        causal = (lax.broadcasted_iota(jnp.int32, (half, half), 0)
                  <= lax.broadcasted_iota(jnp.int32, (half, half), 1))
        start = pl.multiple_of(qi * tk, tk)
        s_lo = [_dot(k_ref[0, h, pl.ds(start, half), :], q_ref[0, h]) for h in range(N_HEADS)]
        s_hi = [_dot(k_ref[0, h, pl.ds(start + half, half), :], q_ref[0, h, :, half:]) for h in range(N_HEADS)]
        probs = []
        for h in range(N_HEADS):
            s_ll = jnp.where(causal, s_lo[h][:, :half], -jnp.inf)
            s_lu = s_lo[h][:, half:]
            s_uu = jnp.where(causal, s_hi[h], -jnp.inf)
            m_l = jnp.max(s_ll, axis=0, keepdims=True)
            m_u = jnp.maximum(jnp.max(s_lu, axis=0, keepdims=True), jnp.max(s_uu, axis=0, keepdims=True))
            p_ll = jnp.exp2(s_ll - m_l)
            p_lu = jnp.exp2(s_lu - m_u)
            p_uu = jnp.exp2(s_uu - m_u)
            l_l = jnp.sum(p_ll, axis=0, keepdims=True)
            l_u = jnp.sum(p_lu, axis=0, keepdims=True) + jnp.sum(p_uu, axis=0, keepdims=True)
            m_ref[h, :, :half] = jnp.broadcast_to(m_l, (STAT_ROWS, half))
            m_ref[h, :, half:] = jnp.broadcast_to(m_u, (STAT_ROWS, half))
            l_ref[h, :, :half] = jnp.broadcast_to(l_l, (STAT_ROWS, half))
            l_ref[h, :, half:] = jnp.broadcast_to(l_u, (STAT_ROWS, half))
            probs.append((jnp.concatenate([p_ll, p_lu], axis=1).astype(BF16), p_uu.astype(BF16)))
        for h in range(N_HEADS):
            v_t = v_ref[0, h, qi]
            acc_l = _dot(v_t[:, :half], probs[h][0])
            acc_u = _dot(v_t[:, half:], probs[h][1])
            acc_ref[h, :, :half] = acc_l[:, :half]
            acc_ref[h, :, half:] = acc_l[:, half:] + acc_u

    def update_single_pass(blocks):
        s_all = [[scores(h, c) for c in blocks] for h in range(N_HEADS)]
        betas, probs = [], []
        for h in range(N_HEADS):
            m_old = m_ref[h, 0:1, :]
            ps = [jnp.exp2(s - m_old) for s in s_all[h]]
            m_blk = jnp.max(s_all[h][0], axis=0, keepdims=True)
            for s in s_all[h][1:]:
                m_blk = jnp.maximum(m_blk, jnp.max(s, axis=0, keepdims=True))
            m_new = jnp.maximum(m_old, m_blk)
            beta = jnp.exp2(m_old - m_new)
            l_new = l_ref[h, 0:1, :]
            for p in ps:
                l_new = l_new + jnp.sum(p, axis=0, keepdims=True)
            m_ref[h] = jnp.broadcast_to(m_new, (STAT_ROWS, tq))
            l_ref[h] = jnp.broadcast_to(l_new * beta, (STAT_ROWS, tq))
            gap_ref[h] = jnp.maximum(gap_ref[h], jnp.broadcast_to(m_blk - m_old, (STAT_ROWS, tq)))
            betas.append(beta)
            probs.append([p.astype(BF16) for p in ps])
        for h in range(N_HEADS):
            acc = acc_ref[h]
            for c, p in zip(blocks, probs[h]):
                acc = acc + _dot(v_ref[0, h, c], p)
            acc_ref[h] = acc * betas[h]

    def update_two_pass(c):
        s_all = [scores(h, c) for h in range(N_HEADS)]
        alphas, probs = [], []
        for h in range(N_HEADS):
            m_old = m_ref[h, 0:1, :]
            m_new = jnp.maximum(m_old, jnp.max(s_all[h], axis=0, keepdims=True))
            alpha = jnp.exp2(m_old - m_new)
            p = jnp.exp2(s_all[h] - m_new)
            l_new = alpha * l_ref[h, 0:1, :] + jnp.sum(p, axis=0, keepdims=True)
            m_ref[h] = jnp.broadcast_to(m_new, (STAT_ROWS, tq))
            l_ref[h] = jnp.broadcast_to(l_new, (STAT_ROWS, tq))
            alphas.append(alpha)
            probs.append(p.astype(BF16))
        for h in range(N_HEADS):
            acc_ref[h] = alphas[h] * acc_ref[h] + _dot(v_ref[0, h, c], probs[h])

    @pl.when(qi % 2 == 0)
    def _():
        cast_slices()
        init_with_diagonal()
        gap_ref[...] = jnp.zeros(gap_ref.shape, F32)

    @pl.when(qi % 2 == 1)
    def _():
        cast_slices()
        init_with_diagonal()
        gap_ref[...] = jnp.zeros(gap_ref.shape, F32)
        update_single_pass([qi - 1])

    def pair(j, _):
        update_single_pass([2 * j, 2 * j + 1])
        return 0

    lax.fori_loop(0, lax.shift_right_logical(qi, 1), pair, 0)

    def write_output():
        for h in range(N_HEADS):
            o_ref[0, :, h * HEAD_DIM:(h + 1) * HEAD_DIM] = (acc_ref[h] / l_ref[h, 0:1, :]).T.astype(BF16)

    worst = gap_ref[0]
    for h in range(1, N_HEADS):
        worst = jnp.maximum(worst, gap_ref[h])
    write_output()

    @pl.when(jnp.max(worst) > MAX_SINGLE_PASS_JUMP)
    def _():
        init_with_diagonal()

        def one(c, _):
            update_two_pass(c)
            return 0

        lax.fori_loop(0, qi, one, 0)
        write_output()


def _cast_block_specs(w, n_rows, steps, n_q):
    cols = w.shape[1]
    per_step = n_rows // steps
    blk = per_step if n_rows % steps == 0 and per_step % BF16_SUBLANES == 0 else LANES
    n_blk = n_rows // blk
    assert n_rows % blk == 0 and n_blk <= steps
    index = lambda b, i: (jnp.minimum(b * n_q + i, n_blk - 1), 0)
    return (pl.BlockSpec((blk, cols), index), pl.BlockSpec((blk, cols), index),
            jax.ShapeDtypeStruct((n_rows, cols), BF16))


def _mla_attention(q, k, v, mem, mem_norm, w_mkv, cast_jobs):
    batch, _, seq, _ = k.shape
    assert Q_TILE == KV_TILE
    n_q = seq // Q_TILE
    cast_specs = [_cast_block_specs(w, rows, batch * n_q, n_q) for w, rows in cast_jobs]
    outs = pl.pallas_call(
        functools.partial(_attn_body, len(cast_jobs)),
        grid=(batch, n_q),
        in_specs=[pl.BlockSpec((1, N_HEADS, QK_DIM, Q_TILE), lambda b, i: (b, 0, 0, i)),
                  pl.BlockSpec((1, N_HEADS, seq, QK_DIM), lambda b, i: (b, 0, 0, 0)),
                  pl.BlockSpec((1, N_HEADS, seq // KV_TILE, HEAD_DIM, KV_TILE), lambda b, i: (b, 0, 0, 0, 0)),
                  pl.BlockSpec((1, N_MEM, D_MODEL), lambda b, i: (b, 0, 0)),
                  _const_spec(mem_norm.shape), _const_spec(w_mkv.shape)]
                 + [c[0] for c in cast_specs],
        out_specs=[pl.BlockSpec((1, Q_TILE, ATT_WIDTH), lambda b, i: (b, i, 0)),
                   pl.BlockSpec((1, N_MEM, 2 * D_MODEL), lambda b, i: (b, 0, 0))] + [c[1] for c in cast_specs],
        out_shape=[jax.ShapeDtypeStruct((batch, seq, ATT_WIDTH), BF16),
                   jax.ShapeDtypeStruct((batch, N_MEM, 2 * D_MODEL), BF16)] + [c[2] for c in cast_specs],
        scratch_shapes=[pltpu.VMEM((N_HEADS, STAT_ROWS, Q_TILE), F32),
                        pltpu.VMEM((N_HEADS, STAT_ROWS, Q_TILE), F32),
                        pltpu.VMEM((N_HEADS, HEAD_DIM, Q_TILE), F32),
                        pltpu.VMEM((N_HEADS, STAT_ROWS, Q_TILE), F32)],
        compiler_params=pltpu.CompilerParams(dimension_semantics=("arbitrary", "arbitrary"),
                                             vmem_limit_bytes=VMEM_LIMIT_BYTES),
        name="mla_attn",
    )(q, k, v, mem, mem_norm, w_mkv, *[w for w, _ in cast_jobs])
    return outs[0], outs[1], outs[2:]


def _post_body(h_ref, a_ref, mkv_ref, xn_ref, n2_ref, fn_ref, wouta_ref, wmq_ref, wmo_ref, wg_ref, wu_ref, wd_ref,
               out_ref):
    def mix(rows):
        h2 = h_ref[rows, :] + _dot(a_ref[rows, :], wouta_ref[...])

        hn = _rmsnorm(h2, xn_ref[...]).astype(BF16)
        q = (_dot(hn, wmq_ref[...]) * MEM_SCALE).astype(BF16)
        cols = [slice(h * MEM_HEAD_DIM, (h + 1) * MEM_HEAD_DIM) for h in range(MEM_HEADS)]
        scores = [_dot_nt(q[:, c], mkv_ref[0, :, c]) for c in cols]
        probs = []
        for s in scores:
            e = jnp.exp(s - jnp.max(s, axis=-1, keepdims=True))
            probs.append((e / jnp.sum(e, axis=-1, keepdims=True)).astype(BF16))
        heads = [_dot(p, mkv_ref[0, :, D_MODEL + h * MEM_HEAD_DIM:D_MODEL + (h + 1) * MEM_HEAD_DIM]).astype(BF16)
                 for h, p in enumerate(probs)]
        return h2 + _dot(jnp.concatenate(heads, axis=1), wmo_ref[...])

    def ffn(rows, h3):
        xn = _rmsnorm(h3, n2_ref[...]).astype(BF16)
        h4 = h3 + 0.5 * _swiglu(xn, wg_ref, wu_ref, wd_ref)
        out_ref[rows, :] = _rmsnorm(h4, fn_ref[...])

    tm = h_ref.shape[0]
    groups = [slice(r, r + POST_ROW_GROUP) for r in range(0, tm, POST_ROW_GROUP)]
    mixed = [mix(rows) for rows in groups]
    for rows, h3 in zip(groups, mixed):
        ffn(rows, h3)


def _post(h2d, a2d, mkv, xn, n2, fn, wouta, wmq, wmo, wg, wu, wd, seq):
    tm = POST_TILE
    n_tok = h2d.shape[0]
    tiles_per_seq = seq // tm
    weights = (xn, n2, fn, wouta, wmq, wmo, wg, wu, wd)
    return pl.pallas_call(
        _post_body,
        grid=(n_tok // tm,),
        in_specs=[pl.BlockSpec((tm, D_MODEL), lambda i: (i, 0)),
                  pl.BlockSpec((tm, ATT_WIDTH), lambda i: (i, 0)),
                  pl.BlockSpec((1, N_MEM, 2 * D_MODEL), lambda i: (i // tiles_per_seq, 0, 0))]
                 + [_const_spec(w.shape) for w in weights],
        out_specs=pl.BlockSpec((tm, D_MODEL), lambda i: (i, 0)),
        out_shape=jax.ShapeDtypeStruct((n_tok, D_MODEL), F32),
        compiler_params=pltpu.CompilerParams(dimension_semantics=("arbitrary",),
                                             vmem_limit_bytes=VMEM_LIMIT_BYTES),
        name="post",
    )(h2d, a2d, mkv, *weights)


def _pad_rope_cols(w):
    zeros = jnp.zeros(w.shape[:-1] + (ROPE_HALF,), w.dtype)
    return jnp.concatenate([w[..., :ROPE_HALF], zeros, w[..., ROPE_HALF:], zeros], axis=-1)


def _block_diag2(a, b):
    za = jnp.zeros((a.shape[0], b.shape[1]), a.dtype)
    zb = jnp.zeros((b.shape[0], a.shape[1]), a.dtype)
    return jnp.concatenate([jnp.concatenate([a, za], axis=1), jnp.concatenate([zb, b], axis=1)], axis=0)


def _layer(h2d, mem, pos, freqs, batch, seq, ffn1_norm, ffn1_w_gate, ffn1_w_up, ffn1_w_down, mix_norm, w_in,
           q_norm, w_q_up, kv_norm, w_kv_up, pool_w, pool_scale, w_out, xattn_norm, mem_norm,
           w_mq, w_mkv, w_mo, ffn2_norm, ffn2_w_gate, ffn2_w_up, ffn2_w_down, out_norm):
    row = lambda g: g.reshape(1, -1).astype(F32)
    rope_lo = Q_RANK + KV_RANK
    w_in_p = jnp.concatenate([w_in[:, :rope_lo], _pad_rope_cols(w_in[:, rope_lo:rope_lo + ROPE_DIM]),
                              w_in[:, rope_lo + ROPE_DIM:]], axis=1).astype(BF16)
    wq = w_q_up.reshape(Q_RANK, N_HEADS, HEAD_DIM + ROPE_DIM)
    wq_p = jnp.concatenate([wq[..., :HEAD_DIM], _pad_rope_cols(wq[..., HEAD_DIM:])], axis=-1)
    wq_t = wq_p.reshape(Q_RANK, N_HEADS * QK_DIM).T.astype(BF16)
    wkv = w_kv_up.reshape(KV_RANK, N_HEADS, 2 * HEAD_DIM)
    wk = wkv[..., :HEAD_DIM].reshape(KV_RANK, ATT_WIDTH).astype(BF16)
    wv_t = wkv[..., HEAD_DIM:].reshape(KV_RANK, ATT_WIDTH).T.astype(BF16)
    poolw = jnp.stack([_block_diag2(pool_w[0], pool_w[1]), _block_diag2(pool_w[2], pool_w[3])]).astype(BF16)

    h1p, q, k, v = _pre(h2d, pos, freqs, row(ffn1_norm), row(mix_norm), w_in_p, row(q_norm), wq_t, row(kv_norm),
                        wk, wv_t, poolw, row(pool_scale), ffn1_w_gate, ffn1_w_up, ffn1_w_down, w_out, batch, seq)
    a, mkv, post_weights = _mla_attention(
        q, k, v, mem, row(mem_norm), w_mkv.astype(BF16),
        [(w_out, ATT_WIDTH), (w_mq, D_MODEL), (w_mo, D_MODEL),
         (ffn2_w_gate, D_MODEL), (ffn2_w_up, D_MODEL), (ffn2_w_down, D_FF)])
    return _post(h1p, a.reshape(batch * seq, ATT_WIDTH), mkv, row(xattn_norm), row(ffn2_norm), row(out_norm),
                 *post_weights, seq)


def kernel(x, mem, positions, ffn1_norm, ffn1_w_gate, ffn1_w_up, ffn1_w_down, mix_norm, w_in, q_norm, w_q_up,
           kv_norm, w_kv_up, pool_w, pool_scale, w_out, xattn_norm, mem_norm, w_mq, w_mkv, w_mo, ffn2_norm,
           ffn2_w_gate, ffn2_w_up, ffn2_w_down, final_norm):
    batch, seq, d_model = x.shape
    depth = ffn1_norm.shape[0]
    assert d_model == D_MODEL and depth == 1
    assert seq % TOKEN_TILE == 0 and seq % POST_TILE == 0 and seq % Q_TILE == 0
    freqs = 1.0 / (ROPE_BASE ** (jnp.arange(0, ROPE_DIM, 2, dtype=F32) / ROPE_DIM))
    layer = (ffn1_norm, ffn1_w_gate, ffn1_w_up, ffn1_w_down, mix_norm, w_in, q_norm, w_q_up, kv_norm, w_kv_up,
             pool_w, pool_scale, w_out, xattn_norm, mem_norm, w_mq, w_mkv, w_mo, ffn2_norm, ffn2_w_gate,
             ffn2_w_up, ffn2_w_down)
    out = _layer(x.reshape(batch * seq, d_model), mem, positions,
                 freqs.reshape(ROPE_HALF, 1), batch, seq, *[w[0] for w in layer], final_norm)
    return out.reshape(batch, seq, d_model)
```

```python
import functools

import jax
import jax.numpy as jnp
from jax import lax
from jax.experimental import pallas as pl
from jax.experimental.pallas import tpu as pltpu

F32 = jnp.float32
BF16 = jnp.bfloat16

D_MODEL = 1024
N_HEADS = 4
HEAD_DIM = 128
ROPE_DIM = 64
ROPE_HALF = ROPE_DIM // 2
Q_RANK = 256
KV_RANK = 128
ATT_WIDTH = N_HEADS * HEAD_DIM
POOL_WIDTH = D_MODEL - ATT_WIDTH
POOL_WINDOWS = (2, 4, 8, 16)
POOL_CH = POOL_WIDTH // len(POOL_WINDOWS)
POOL_HALO = 16
D_FF = 2816
N_MEM = 256
MEM_HEADS = 4
MEM_HEAD_DIM = D_MODEL // MEM_HEADS
ROPE_BASE = 10000.0
RMS_EPS = 1e-6
ATT_SCALE = (HEAD_DIM + ROPE_DIM) ** -0.5
LOG2_E = 1.4426950408889634
Q_PRESCALE = ATT_SCALE * LOG2_E
MEM_SCALE = MEM_HEAD_DIM ** -0.5

LANES = 128
BF16_SUBLANES = 16
MXU_DIM = 256
VMEM_LIMIT_BYTES = 56 * 1024 * 1024

TOKEN_TILE = 512
POST_TILE = 1024
POST_ROW_GROUP = 512
FF_CHUNK = MXU_DIM
Q_TILE = 512
KV_TILE = 512
QK_DIM = 2 * HEAD_DIM
CAST_ROWS_WIDE = 128
CAST_ROWS = 256
CAST_SLOTS = 3
STAT_ROWS = 8
MAX_SINGLE_PASS_JUMP = 32.0


def _rmsnorm(x, g):
    ms = jnp.mean(x * x, axis=-1, keepdims=True)
    return (x * lax.rsqrt(ms + RMS_EPS)) * g


def _dot(a, b):
    return jnp.dot(a, b, preferred_element_type=F32)


def _dot_nt(a, b):
    return lax.dot_general(a, b, (((1,), (1,)), ((), ())), preferred_element_type=F32)


def _swiglu(xn, wg_ref, wu_ref, wd_ref):
    bounds = list(range(0, D_FF, FF_CHUNK)) + [D_FF]
    acts = []
    for lo, hi in zip(bounds[:-1], bounds[1:]):
        g = _dot(xn, wg_ref[:, lo:hi])
        u = _dot(xn, wu_ref[:, lo:hi])
        acts.append(((g / (1.0 + jnp.exp(-g))) * u).astype(BF16))
    return _dot(jnp.concatenate(acts, axis=1), wd_ref[...])


def _row_jobs(src, src_row0, n_rows, dst, chunk):
    assert n_rows % chunk == 0
    return [(src, src_row0 + r, dst, r) for r in range(0, n_rows, chunk)]


def _cast_weights_into_vmem(jobs, stage, sems):
    slots, rows = stage.shape[0], stage.shape[1]
    ahead = slots - 1

    def copy(j):
        src, r0, _, _ = jobs[j]
        return pltpu.make_async_copy(src.at[pl.ds(r0, rows), :], stage.at[j % slots], sems.at[j % slots])

    for j in range(min(ahead, len(jobs))):
        copy(j).start()
    for j in range(len(jobs)):
        if j + ahead < len(jobs):
            copy(j + ahead).start()
        copy(j).wait()
        _, _, dst, d0 = jobs[j]
        dst[d0:d0 + rows, :] = stage[j % slots].astype(BF16)


def _cast_scratch(wide_rows, narrow_rows):
    return [pltpu.VMEM((CAST_SLOTS, wide_rows, D_FF), F32), pltpu.VMEM((CAST_SLOTS, narrow_rows, D_MODEL), F32),
            pltpu.SemaphoreType.DMA((CAST_SLOTS,)), pltpu.SemaphoreType.DMA((CAST_SLOTS,))]


def _rope(x, cos4, sin4):
    return x * cos4 + pltpu.roll(x, 2 * ROPE_HALF, 1) * sin4


def _pre_body(tiles_per_seq, n_tiles, n_cast,
              x_ref, pos_ref, freq_ref, n1_ref, mixn_ref, win_ref, qn_ref, wq_ref, kvn_ref, wk_ref, wv_ref,
              poolw_ref, pscale_ref, wg_hbm, wu_hbm, wd_hbm, wout_hbm, *refs):
    cast_in, (h_ref, q_ref, k_ref, v_ref) = refs[:n_cast], refs[n_cast:n_cast + 4]
    cast_out = refs[n_cast + 4:2 * n_cast + 4]
    (ext_ref, h1_ref, z_ref, cos_ref, sin_ref, wg_ref, wu_ref, wd_ref, woutp_ref,
     stage_wide, stage, sem_wide, sem) = refs[2 * n_cast + 4:]
    step = pl.program_id(0)
    tm = x_ref.shape[0]

    @pl.when(step == 0)
    def _():
        _cast_weights_into_vmem(_row_jobs(wg_hbm, 0, D_MODEL, wg_ref, CAST_ROWS_WIDE)
                                + _row_jobs(wu_hbm, 0, D_MODEL, wu_ref, CAST_ROWS_WIDE), stage_wide, sem_wide)
        _cast_weights_into_vmem(_row_jobs(wd_hbm, 0, D_FF, wd_ref, CAST_ROWS)
                                + _row_jobs(wout_hbm, ATT_WIDTH, POOL_WIDTH, woutp_ref, CAST_ROWS), stage, sem)
        h1_ref[...] = jnp.zeros(h1_ref.shape, F32)
        z_ref[...] = jnp.zeros(z_ref.shape, F32)
        ext_ref[...] = jnp.zeros(ext_ref.shape, F32)
        cos_ref[...] = jnp.zeros(cos_ref.shape, F32)
        sin_ref[...] = jnp.zeros(sin_ref.shape, F32)

    def ffn_and_project():
        x = x_ref[...]
        xn = _rmsnorm(x, n1_ref[...]).astype(BF16)
        h1 = x + 0.5 * _swiglu(xn, wg_ref, wu_ref, wd_ref)
        un = _rmsnorm(h1, mixn_ref[...]).astype(BF16)
        z = _dot(un, win_ref[...])
        batch_row = jnp.minimum(step, n_tiles - 1) // tiles_per_seq
        ang = freq_ref[...] * pos_ref[pl.ds(batch_row, 1), :].astype(F32)
        return h1, z, jnp.cos(ang), jnp.sin(ang)

    def mixer_inputs():
        seq_tile = jnp.maximum(step - 1, 0) % tiles_per_seq

        cos = cos_ref[...]
        sin = sin_ref[...]
        cos4_t = jnp.concatenate([cos, cos, cos, cos], axis=0)
        sin4_t = jnp.concatenate([-sin, -sin, sin, sin], axis=0)

        qn = _rmsnorm(z_ref[:, :Q_RANK], qn_ref[...]).astype(BF16)
        q_t = _dot_nt(wq_ref[...], qn)
        for h in range(N_HEADS):
            base = h * QK_DIM
            q_ref[0, h, :HEAD_DIM, :] = (q_t[base:base + HEAD_DIM] * Q_PRESCALE).astype(BF16)
            pe = q_t[base + HEAD_DIM:base + QK_DIM]
            pe = pe * cos4_t + pltpu.roll(pe, 2 * ROPE_HALF, 0) * sin4_t
            q_ref[0, h, HEAD_DIM:, :] = (pe * Q_PRESCALE).astype(BF16)

        kvn = _rmsnorm(z_ref[:, Q_RANK:Q_RANK + KV_RANK], kvn_ref[...]).astype(BF16)
        k_nope = _dot(kvn, wk_ref[...])
        v_t = _dot_nt(wv_ref[...], kvn)
        k_pe = _rope(z_ref[:, Q_RANK + KV_RANK:Q_RANK + KV_RANK + LANES], cos4_t.T, sin4_t.T).astype(BF16)
        for h in range(N_HEADS):
            k_ref[0, h, :, :HEAD_DIM] = k_nope[:, h * HEAD_DIM:(h + 1) * HEAD_DIM].astype(BF16)
            k_ref[0, h, :, HEAD_DIM:] = k_pe
            v_ref[0, h, 0] = v_t[h * HEAD_DIM:(h + 1) * HEAD_DIM].astype(BF16)

        zp = z_ref[:, D_MODEL - POOL_WIDTH:]
        ext_ref[0:POOL_HALO, :] = jnp.where(seq_tile == 0, 0.0, ext_ref[0:POOL_HALO, :])
        ext_ref[POOL_HALO:, :] = zp
        t1 = (seq_tile * tm + 1 + lax.broadcasted_iota(jnp.int32, (tm, 1), 0)).astype(F32)
        assert POOL_WINDOWS == (2, 4, 8, 16) and POOL_HALO >= sum(POOL_WINDOWS) // 2
        level = ext_ref[...]
        diffs = []
        for g, w in enumerate(POOL_WINDOWS):
            level = level + pltpu.roll(level, w // 2, 0)
            mean = level[POOL_HALO:, :POOL_CH] / jnp.minimum(t1, float(w))
            diffs.append(mean - zp[:, g * POOL_CH:(g + 1) * POOL_CH])
            if g + 1 < len(POOL_WINDOWS):
                level = level[:, POOL_CH:]
        ext_ref[0:POOL_HALO, :] = ext_ref[tm:tm + POOL_HALO, :]

        y01 = _dot(jnp.concatenate(diffs[:2], axis=1).astype(BF16), poolw_ref[0])
        y23 = _dot(jnp.concatenate(diffs[2:], axis=1).astype(BF16), poolw_ref[1])
        p = jnp.concatenate([y01, y23], axis=1) * pscale_ref[...]
        h_ref[...] = h1_ref[...] + _dot(p.astype(BF16), woutp_ref[...])

    @pl.when(step < n_tiles)
    def _():
        for src, dst in zip(cast_in, cast_out):
            dst[...] = src[...].astype(BF16)
        mixer_inputs()
        h1, z, cos, sin = ffn_and_project()
        h1_ref[...] = h1
        z_ref[...] = z
        cos_ref[...] = cos
        sin_ref[...] = sin

    @pl.when(step == n_tiles)
    def _():
        mixer_inputs()


def _const_spec(shape):
    nd = len(shape)
    return pl.BlockSpec(shape, lambda *_: (0,) * nd, pipeline_mode=pl.Buffered(1))


def _cast_block_specs(w, n_rows, steps):
    cols = w.shape[1]
    per_step = n_rows // steps
    blk = per_step if n_rows % steps == 0 and per_step % BF16_SUBLANES == 0 else LANES
    n_blk = n_rows // blk
    assert n_rows % blk == 0 and n_blk <= steps
    index = lambda i: (jnp.minimum(i, n_blk - 1), 0)
    return (pl.BlockSpec((blk, cols), index), pl.BlockSpec((blk, cols), index),
            jax.ShapeDtypeStruct((n_rows, cols), BF16))


def _pre(x2d, pos, freqs, n1, mixn, win, qn, wq, kvn, wk, wv, poolw, pscale, wg, wu, wd, wout, cast_jobs,
         batch, seq):
    tm = TOKEN_TILE
    assert tm == KV_TILE
    n_tok = x2d.shape[0]
    tiles_per_seq = seq // tm
    n_tiles = n_tok // tm
    lag = lambda i: jnp.maximum(i - 1, 0)
    bt = lambda i: (lag(i) // tiles_per_seq, lag(i) % tiles_per_seq)
    x_spec = pl.BlockSpec((tm, D_MODEL), lambda i: (jnp.minimum(i, n_tiles - 1), 0))
    h_spec = pl.BlockSpec((tm, D_MODEL), lambda i: (lag(i), 0))
    pos_spec = pl.BlockSpec((batch, tm), lambda i: (0, jnp.minimum(i, n_tiles - 1) % tiles_per_seq))
    q_spec = pl.BlockSpec((1, N_HEADS, QK_DIM, tm), lambda i: (bt(i)[0], 0, 0, bt(i)[1]))
    k_spec = pl.BlockSpec((1, N_HEADS, tm, QK_DIM), lambda i: (bt(i)[0], 0, bt(i)[1], 0))
    v_spec = pl.BlockSpec((1, N_HEADS, 1, HEAD_DIM, tm), lambda i: (bt(i)[0], 0, bt(i)[1], 0, 0))
    weights = (n1, mixn, win, qn, wq, kvn, wk, wv, poolw, pscale)
    hbm_weights = (wg, wu, wd, wout)
    cast_specs = [_cast_block_specs(w, rows, n_tiles) for w, rows in cast_jobs]
    outs = pl.pallas_call(
        functools.partial(_pre_body, tiles_per_seq, n_tiles, len(cast_jobs)),
        grid=(n_tiles + 1,),
        in_specs=[x_spec, pos_spec, _const_spec(freqs.shape)] + [_const_spec(w.shape) for w in weights]
                 + [pl.BlockSpec(memory_space=pl.ANY)] * len(hbm_weights) + [c[0] for c in cast_specs],
        out_specs=[h_spec, q_spec, k_spec, v_spec] + [c[1] for c in cast_specs],
        out_shape=[jax.ShapeDtypeStruct((n_tok, D_MODEL), F32),
                   jax.ShapeDtypeStruct((batch, N_HEADS, QK_DIM, seq), BF16),
                   jax.ShapeDtypeStruct((batch, N_HEADS, seq, QK_DIM), BF16),
                   jax.ShapeDtypeStruct((batch, N_HEADS, tiles_per_seq, HEAD_DIM, tm), BF16)]
                  + [c[2] for c in cast_specs],
        scratch_shapes=[pltpu.VMEM((tm + POOL_HALO, POOL_WIDTH), F32),
                        pltpu.VMEM((tm, D_MODEL), F32), pltpu.VMEM((tm, D_MODEL), F32),
                        pltpu.VMEM((ROPE_HALF, tm), F32), pltpu.VMEM((ROPE_HALF, tm), F32),
                        pltpu.VMEM((D_MODEL, D_FF), BF16), pltpu.VMEM((D_MODEL, D_FF), BF16),
                        pltpu.VMEM((D_FF, D_MODEL), BF16), pltpu.VMEM((POOL_WIDTH, D_MODEL), BF16)]
                       + _cast_scratch(CAST_ROWS_WIDE, CAST_ROWS),
        compiler_params=pltpu.CompilerParams(dimension_semantics=("arbitrary",),
                                             vmem_limit_bytes=VMEM_LIMIT_BYTES),
        name="pre",
    )(x2d, pos, freqs, *weights, *hbm_weights, *[w for w, _ in cast_jobs])
    return outs[:4], outs[4:]


def _attn_body(q_ref, k_ref, v_ref, mem_ref, memn_ref, wmkv_ref, o_ref, mkv_ref, m_ref, l_ref, acc_ref, gap_ref):
    qi = pl.program_id(1)

    @pl.when(qi == 0)
    def _():
        mn = _rmsnorm(mem_ref[0], memn_ref[...]).astype(BF16)
        mkv_ref[0] = _dot(mn, wmkv_ref[...]).astype(BF16)

    tq = q_ref.shape[3]
    tk = KV_TILE
    half = tk // 2

    def scores(h, c):
        return _dot(k_ref[0, h, pl.ds(pl.multiple_of(c * tk, tk), tk), :], q_ref[0, h])


    def init_with_diagonal():
        causal = (lax.broadcasted_iota(jnp.int32, (half, half), 0)
                  <= lax.broadcasted_iota(jnp.int32, (half, half), 1))
        start = pl.multiple_of(qi * tk, tk)
        s_lo = [_dot(k_ref[0, h, pl.ds(start, half), :], q_ref[0, h]) for h in range(N_HEADS)]
        s_hi = [_dot(k_ref[0, h, pl.ds(start + half, half), :], q_ref[0, h, :, half:]) for h in range(N_HEADS)]
        probs = []
        for h in range(N_HEADS):
            s_ll = jnp.where(causal, s_lo[h][:, :half], -jnp.inf)
            s_lu = s_lo[h][:, half:]
            s_uu = jnp.where(causal, s_hi[h], -jnp.inf)
            m_l = jnp.max(s_ll, axis=0, keepdims=True)
            m_u = jnp.maximum(jnp.max(s_lu, axis=0, keepdims=True), jnp.max(s_uu, axis=0, keepdims=True))
            p_ll = jnp.exp2(s_ll - m_l)
            p_lu = jnp.exp2(s_lu - m_u)
            p_uu = jnp.exp2(s_uu - m_u)
            l_l = jnp.sum(p_ll, axis=0, keepdims=True)
            l_u = jnp.sum(p_lu, axis=0, keepdims=True) + jnp.sum(p_uu, axis=0, keepdims=True)
            m_ref[h, :, :half] = jnp.broadcast_to(m_l, (STAT_ROWS, half))
            m_ref[h, :, half:] = jnp.broadcast_to(m_u, (STAT_ROWS, half))
            l_ref[h, :, :half] = jnp.broadcast_to(l_l, (STAT_ROWS, half))
            l_ref[h, :, half:] = jnp.broadcast_to(l_u, (STAT_ROWS, half))
            probs.append((jnp.concatenate([p_ll, p_lu], axis=1).astype(BF16), p_uu.astype(BF16)))
        for h in range(N_HEADS):
            v_t = v_ref[0, h, qi]
            acc_l = _dot(v_t[:, :half], probs[h][0])
            acc_u = _dot(v_t[:, half:], probs[h][1])
            acc_ref[h, :, :half] = acc_l[:, :half]
            acc_ref[h, :, half:] = acc_l[:, half:] + acc_u

    def update_single_pass(blocks):
        s_all = [[scores(h, c) for c in blocks] for h in range(N_HEADS)]
        betas, probs = [], []
        for h in range(N_HEADS):
            m_old = m_ref[h, 0:1, :]
            ps = [jnp.exp2(s - m_old) for s in s_all[h]]
            m_blk = jnp.max(s_all[h][0], axis=0, keepdims=True)
            for s in s_all[h][1:]:
                m_blk = jnp.maximum(m_blk, jnp.max(s, axis=0, keepdims=True))
            m_new = jnp.maximum(m_old, m_blk)
            beta = jnp.exp2(m_old - m_new)
            l_new = l_ref[h, 0:1, :]
            for p in ps:
                l_new = l_new + jnp.sum(p, axis=0, keepdims=True)
            m_ref[h] = jnp.broadcast_to(m_new, (STAT_ROWS, tq))
            l_ref[h] = jnp.broadcast_to(l_new * beta, (STAT_ROWS, tq))
            gap_ref[h] = jnp.maximum(gap_ref[h], jnp.broadcast_to(m_blk - m_old, (STAT_ROWS, tq)))
            betas.append(beta)
            probs.append([p.astype(BF16) for p in ps])
        for h in range(N_HEADS):
            acc = acc_ref[h]
            for c, p in zip(blocks, probs[h]):
                acc = acc + _dot(v_ref[0, h, c], p)
            acc_ref[h] = acc * betas[h]

    def update_two_pass(c):
        s_all = [scores(h, c) for h in range(N_HEADS)]
        alphas, probs = [], []
        for h in range(N_HEADS):
            m_old = m_ref[h, 0:1, :]
            m_new = jnp.maximum(m_old, jnp.max(s_all[h], axis=0, keepdims=True))
            alpha = jnp.exp2(m_old - m_new)
            p = jnp.exp2(s_all[h] - m_new)
            l_new = alpha * l_ref[h, 0:1, :] + jnp.sum(p, axis=0, keepdims=True)
            m_ref[h] = jnp.broadcast_to(m_new, (STAT_ROWS, tq))
            l_ref[h] = jnp.broadcast_to(l_new, (STAT_ROWS, tq))
            alphas.append(alpha)
            probs.append(p.astype(BF16))
        for h in range(N_HEADS):
            acc_ref[h] = alphas[h] * acc_ref[h] + _dot(v_ref[0, h, c], probs[h])

    @pl.when(qi % 2 == 0)
    def _():
        init_with_diagonal()
        gap_ref[...] = jnp.zeros(gap_ref.shape, F32)

    @pl.when(qi % 2 == 1)
    def _():
        init_with_diagonal()
        gap_ref[...] = jnp.zeros(gap_ref.shape, F32)
        update_single_pass([qi - 1])

    def pair(j, _):
        update_single_pass([2 * j, 2 * j + 1])
        return 0

    lax.fori_loop(0, lax.shift_right_logical(qi, 1), pair, 0)

    def write_output():
        for h in range(N_HEADS):
            o_ref[0, :, h * HEAD_DIM:(h + 1) * HEAD_DIM] = (acc_ref[h] / l_ref[h, 0:1, :]).T.astype(BF16)

    worst = gap_ref[0]
    for h in range(1, N_HEADS):
        worst = jnp.maximum(worst, gap_ref[h])
    write_output()

    @pl.when(jnp.max(worst) > MAX_SINGLE_PASS_JUMP)
    def _():
        init_with_diagonal()

        def one(c, _):
            update_two_pass(c)
            return 0

        lax.fori_loop(0, qi, one, 0)
        write_output()


def _mla_attention(q, k, v, mem, mem_norm, w_mkv):
    batch, _, seq, _ = k.shape
    assert Q_TILE == KV_TILE
    n_q = seq // Q_TILE
    return pl.pallas_call(
        _attn_body,
        grid=(batch, n_q),
        in_specs=[pl.BlockSpec((1, N_HEADS, QK_DIM, Q_TILE), lambda b, i: (b, 0, 0, i)),
                  pl.BlockSpec((1, N_HEADS, seq, QK_DIM), lambda b, i: (b, 0, 0, 0)),
                  pl.BlockSpec((1, N_HEADS, seq // KV_TILE, HEAD_DIM, KV_TILE), lambda b, i: (b, 0, 0, 0, 0)),
                  pl.BlockSpec((1, N_MEM, D_MODEL), lambda b, i: (b, 0, 0)),
                  _const_spec(mem_norm.shape), _const_spec(w_mkv.shape)],
        out_specs=[pl.BlockSpec((1, Q_TILE, ATT_WIDTH), lambda b, i: (b, i, 0)),
                   pl.BlockSpec((1, N_MEM, 2 * D_MODEL), lambda b, i: (b, 0, 0))],
        out_shape=[jax.ShapeDtypeStruct((batch, seq, ATT_WIDTH), BF16),
                   jax.ShapeDtypeStruct((batch, N_MEM, 2 * D_MODEL), BF16)],
        scratch_shapes=[pltpu.VMEM((N_HEADS, STAT_ROWS, Q_TILE), F32),
                        pltpu.VMEM((N_HEADS, STAT_ROWS, Q_TILE), F32),
                        pltpu.VMEM((N_HEADS, HEAD_DIM, Q_TILE), F32),
                        pltpu.VMEM((N_HEADS, STAT_ROWS, Q_TILE), F32)],
        compiler_params=pltpu.CompilerParams(dimension_semantics=("arbitrary", "arbitrary"),
                                             vmem_limit_bytes=VMEM_LIMIT_BYTES),
        name="mla_attn",
    )(q, k, v, mem, mem_norm, w_mkv)


def _post_body(h_ref, a_ref, mkv_ref, xn_ref, n2_ref, fn_ref, wouta_ref, wmq_ref, wmo_ref, wg_ref, wu_ref, wd_ref,
               out_ref):
    def mix(rows):
        h2 = h_ref[rows, :] + _dot(a_ref[rows, :], wouta_ref[...])

        hn = _rmsnorm(h2, xn_ref[...]).astype(BF16)
        q = (_dot(hn, wmq_ref[...]) * MEM_SCALE).astype(BF16)
        cols = [slice(h * MEM_HEAD_DIM, (h + 1) * MEM_HEAD_DIM) for h in range(MEM_HEADS)]
        scores = [_dot_nt(q[:, c], mkv_ref[0, :, c]) for c in cols]
        probs = []
        for s in scores:
            e = jnp.exp(s - jnp.max(s, axis=-1, keepdims=True))
            probs.append((e / jnp.sum(e, axis=-1, keepdims=True)).astype(BF16))
        heads = [_dot(p, mkv_ref[0, :, D_MODEL + h * MEM_HEAD_DIM:D_MODEL + (h + 1) * MEM_HEAD_DIM]).astype(BF16)
                 for h, p in enumerate(probs)]
        return h2 + _dot(jnp.concatenate(heads, axis=1), wmo_ref[...])

    def ffn(rows, h3):
        xn = _rmsnorm(h3, n2_ref[...]).astype(BF16)
        h4 = h3 + 0.5 * _swiglu(xn, wg_ref, wu_ref, wd_ref)
        out_ref[rows, :] = _rmsnorm(h4, fn_ref[...])

    tm = h_ref.shape[0]
    groups = [slice(r, r + POST_ROW_GROUP) for r in range(0, tm, POST_ROW_GROUP)]
    mixed = [mix(rows) for rows in groups]
    for rows, h3 in zip(groups, mixed):
        ffn(rows, h3)


def _post(h2d, a2d, mkv, xn, n2, fn, wouta, wmq, wmo, wg, wu, wd, seq):
    tm = POST_TILE
    n_tok = h2d.shape[0]
    tiles_per_seq = seq // tm
    weights = (xn, n2, fn, wouta, wmq, wmo, wg, wu, wd)
    return pl.pallas_call(
        _post_body,
        grid=(n_tok // tm,),
        in_specs=[pl.BlockSpec((tm, D_MODEL), lambda i: (i, 0)),
                  pl.BlockSpec((tm, ATT_WIDTH), lambda i: (i, 0)),
                  pl.BlockSpec((1, N_MEM, 2 * D_MODEL), lambda i: (i // tiles_per_seq, 0, 0))]
                 + [_const_spec(w.shape) for w in weights],
        out_specs=pl.BlockSpec((tm, D_MODEL), lambda i: (i, 0)),
        out_shape=jax.ShapeDtypeStruct((n_tok, D_MODEL), F32),
        compiler_params=pltpu.CompilerParams(dimension_semantics=("arbitrary",),
                                             vmem_limit_bytes=VMEM_LIMIT_BYTES),
        name="post",
    )(h2d, a2d, mkv, *weights)


def _pad_rope_cols(w):
    zeros = jnp.zeros(w.shape[:-1] + (ROPE_HALF,), w.dtype)
    return jnp.concatenate([w[..., :ROPE_HALF], zeros, w[..., ROPE_HALF:], zeros], axis=-1)


def _block_diag2(a, b):
    za = jnp.zeros((a.shape[0], b.shape[1]), a.dtype)
    zb = jnp.zeros((b.shape[0], a.shape[1]), a.dtype)
    return jnp.concatenate([jnp.concatenate([a, za], axis=1), jnp.concatenate([zb, b], axis=1)], axis=0)


def _layer(h2d, mem, pos, freqs, batch, seq, ffn1_norm, ffn1_w_gate, ffn1_w_up, ffn1_w_down, mix_norm, w_in,
           q_norm, w_q_up, kv_norm, w_kv_up, pool_w, pool_scale, w_out, xattn_norm, mem_norm,
           w_mq, w_mkv, w_mo, ffn2_norm, ffn2_w_gate, ffn2_w_up, ffn2_w_down, out_norm):
    row = lambda g: g.reshape(1, -1).astype(F32)
    rope_lo = Q_RANK + KV_RANK
    w_in_p = jnp.concatenate([w_in[:, :rope_lo], _pad_rope_cols(w_in[:, rope_lo:rope_lo + ROPE_DIM]),
                              w_in[:, rope_lo + ROPE_DIM:]], axis=1).astype(BF16)
    wq = w_q_up.reshape(Q_RANK, N_HEADS, HEAD_DIM + ROPE_DIM)
    wq_p = jnp.concatenate([wq[..., :HEAD_DIM], _pad_rope_cols(wq[..., HEAD_DIM:])], axis=-1)
    wq_t = wq_p.reshape(Q_RANK, N_HEADS * QK_DIM).T.astype(BF16)
    wkv = w_kv_up.reshape(KV_RANK, N_HEADS, 2 * HEAD_DIM)
    wk = wkv[..., :HEAD_DIM].reshape(KV_RANK, ATT_WIDTH).astype(BF16)
    wv_t = wkv[..., HEAD_DIM:].reshape(KV_RANK, ATT_WIDTH).T.astype(BF16)
    poolw = jnp.stack([_block_diag2(pool_w[0], pool_w[1]), _block_diag2(pool_w[2], pool_w[3])]).astype(BF16)

    (h1p, q, k, v), post_weights = _pre(
        h2d, pos, freqs, row(ffn1_norm), row(mix_norm), w_in_p, row(q_norm), wq_t, row(kv_norm), wk, wv_t, poolw,
        row(pool_scale), ffn1_w_gate, ffn1_w_up, ffn1_w_down, w_out,
        [(w_out, ATT_WIDTH), (w_mq, D_MODEL), (w_mo, D_MODEL),
         (ffn2_w_gate, D_MODEL), (ffn2_w_up, D_MODEL), (ffn2_w_down, D_FF)], batch, seq)
    a, mkv = _mla_attention(q, k, v, mem, row(mem_norm), w_mkv.astype(BF16))
    return _post(h1p, a.reshape(batch * seq, ATT_WIDTH), mkv, row(xattn_norm), row(ffn2_norm), row(out_norm),
                 *post_weights, seq)


def kernel(x, mem, positions, ffn1_norm, ffn1_w_gate, ffn1_w_up, ffn1_w_down, mix_norm, w_in, q_norm, w_q_up,
           kv_norm, w_kv_up, pool_w, pool_scale, w_out, xattn_norm, mem_norm, w_mq, w_mkv, w_mo, ffn2_norm,
           ffn2_w_gate, ffn2_w_up, ffn2_w_down, final_norm):
    batch, seq, d_model = x.shape
    depth = ffn1_norm.shape[0]
    assert d_model == D_MODEL and depth == 1
    assert seq % TOKEN_TILE == 0 and seq % POST_TILE == 0 and seq % Q_TILE == 0
    freqs = 1.0 / (ROPE_BASE ** (jnp.arange(0, ROPE_DIM, 2, dtype=F32) / ROPE_DIM))
    layer = (ffn1_norm, ffn1_w_gate, ffn1_w_up, ffn1_w_down, mix_norm, w_in, q_norm, w_q_up, kv_norm, w_kv_up,
             pool_w, pool_scale, w_out, xattn_norm, mem_norm, w_mq, w_mkv, w_mo, ffn2_norm, ffn2_w_gate,
             ffn2_w_up, ffn2_w_down)
    out = _layer(x.reshape(batch * seq, d_model), mem, positions,
                 freqs.reshape(ROPE_HALF, 1), batch, seq, *[w[0] for w in layer], final_norm)
    return out.reshape(batch, seq, d_model)
```

```python
import functools

import jax
import jax.numpy as jnp
from jax import lax
from jax.experimental import pallas as pl
from jax.experimental.pallas import tpu as pltpu

F32 = jnp.float32
BF16 = jnp.bfloat16

D_MODEL = 1024
N_HEADS = 4
HEAD_DIM = 128
ROPE_DIM = 64
ROPE_HALF = ROPE_DIM // 2
Q_RANK = 256
KV_RANK = 128
ATT_WIDTH = N_HEADS * HEAD_DIM
POOL_WIDTH = D_MODEL - ATT_WIDTH
POOL_WINDOWS = (2, 4, 8, 16)
POOL_CH = POOL_WIDTH // len(POOL_WINDOWS)
POOL_HALO = 16
D_FF = 2816
N_MEM = 256
MEM_HEADS = 4
MEM_HEAD_DIM = D_MODEL // MEM_HEADS
ROPE_BASE = 10000.0
RMS_EPS = 1e-6
ATT_SCALE = (HEAD_DIM + ROPE_DIM) ** -0.5
LOG2_E = 1.4426950408889634
Q_PRESCALE = ATT_SCALE * LOG2_E
MEM_SCALE = MEM_HEAD_DIM ** -0.5

LANES = 128
BF16_SUBLANES = 16
MXU_DIM = 256
VMEM_LIMIT_BYTES = 56 * 1024 * 1024

TOKEN_TILE = 512
POST_TILE = 1024
POST_ROW_GROUP = 512
FF_CHUNK = MXU_DIM
Q_TILE = 512
TILES_PER_ATTN_STEP = 2
KV_TILE = 512
QK_DIM = 2 * HEAD_DIM
CAST_ROWS_WIDE = 128
CAST_ROWS = 256
CAST_SLOTS = 4
STAT_ROWS = 8
MAX_SINGLE_PASS_JUMP = 32.0


def _rmsnorm(x, g):
    ms = jnp.mean(x * x, axis=-1, keepdims=True)
    return (x * lax.rsqrt(ms + RMS_EPS)) * g


def _dot(a, b):
    return jnp.dot(a, b, preferred_element_type=F32)


def _dot_nt(a, b):
    return lax.dot_general(a, b, (((1,), (1,)), ((), ())), preferred_element_type=F32)


def _swiglu(xn, wg_ref, wu_ref, wd_ref):
    bounds = list(range(0, D_FF, FF_CHUNK)) + [D_FF]
    acts = []
    for lo, hi in zip(bounds[:-1], bounds[1:]):
        g = _dot(xn, wg_ref[:, lo:hi])
        u = _dot(xn, wu_ref[:, lo:hi])
        acts.append(((g / (1.0 + jnp.exp(-g))) * u).astype(BF16))
    return _dot(jnp.concatenate(acts, axis=1), wd_ref[...])


def _row_jobs(src, src_row0, n_rows, dst, chunk):
    assert n_rows % chunk == 0
    return [(src, src_row0 + r, dst, r) for r in range(0, n_rows, chunk)]


def _cast_weights_into_vmem(jobs, stage, sems):
    slots, rows = stage.shape[0], stage.shape[1]
    ahead = slots - 1

    def copy(j):
        src, r0, _, _ = jobs[j]
        return pltpu.make_async_copy(src.at[pl.ds(r0, rows), :], stage.at[j % slots], sems.at[j % slots])

    for j in range(min(ahead, len(jobs))):
        copy(j).start()
    for j in range(len(jobs)):
        if j + ahead < len(jobs):
            copy(j + ahead).start()
        copy(j).wait()
        _, _, dst, d0 = jobs[j]
        dst[d0:d0 + rows, :] = stage[j % slots].astype(BF16)


def _cast_scratch(wide_rows, narrow_rows):
    return [pltpu.VMEM((CAST_SLOTS, wide_rows, D_FF), F32), pltpu.VMEM((CAST_SLOTS, narrow_rows, D_MODEL), F32),
            pltpu.SemaphoreType.DMA((CAST_SLOTS,)), pltpu.SemaphoreType.DMA((CAST_SLOTS,))]


def _rope(x, cos4, sin4):
    return x * cos4 + pltpu.roll(x, 2 * ROPE_HALF, 1) * sin4


def _pre_body(tiles_per_seq, n_tiles,
              x_ref, pos_ref, freq_ref, n1_ref, mixn_ref, win_ref, qn_ref, wq_ref, kvn_ref, wk_ref, wv_ref,
              poolw_ref, pscale_ref, wg_hbm, wu_hbm, wd_hbm, wout_hbm,
              h_ref, q_ref, k_ref, v_ref,
              ext_ref, h1_ref, z_ref, cos_ref, sin_ref, wg_ref, wu_ref, wd_ref, woutp_ref,
              stage_wide, stage, sem_wide, sem):
    step = pl.program_id(0)
    tm = x_ref.shape[0]

    @pl.when(step == 0)
    def _():
        _cast_weights_into_vmem(_row_jobs(wg_hbm, 0, D_MODEL, wg_ref, CAST_ROWS_WIDE)
                                + _row_jobs(wu_hbm, 0, D_MODEL, wu_ref, CAST_ROWS_WIDE), stage_wide, sem_wide)
        _cast_weights_into_vmem(_row_jobs(wd_hbm, 0, D_FF, wd_ref, CAST_ROWS)
                                + _row_jobs(wout_hbm, ATT_WIDTH, POOL_WIDTH, woutp_ref, CAST_ROWS), stage, sem)
        h1_ref[...] = jnp.zeros(h1_ref.shape, F32)
        z_ref[...] = jnp.zeros(z_ref.shape, F32)
        ext_ref[...] = jnp.zeros(ext_ref.shape, F32)
        cos_ref[...] = jnp.zeros(cos_ref.shape, F32)
        sin_ref[...] = jnp.zeros(sin_ref.shape, F32)

    def ffn_and_project():
        x = x_ref[...]
        xn = _rmsnorm(x, n1_ref[...]).astype(BF16)
        h1 = x + 0.5 * _swiglu(xn, wg_ref, wu_ref, wd_ref)
        un = _rmsnorm(h1, mixn_ref[...]).astype(BF16)
        z = _dot(un, win_ref[...])
        batch_row = jnp.minimum(step, n_tiles - 1) // tiles_per_seq
        ang = freq_ref[...] * pos_ref[pl.ds(batch_row, 1), :].astype(F32)
        return h1, z, jnp.cos(ang), jnp.sin(ang)

    def mixer_inputs():
        seq_tile = jnp.maximum(step - 1, 0) % tiles_per_seq

        cos = cos_ref[...]
        sin = sin_ref[...]
        cos4_t = jnp.concatenate([cos, cos, cos, cos], axis=0)
        sin4_t = jnp.concatenate([-sin, -sin, sin, sin], axis=0)

        qn = _rmsnorm(z_ref[:, :Q_RANK], qn_ref[...]).astype(BF16)
        q_t = _dot_nt(wq_ref[...], qn)
        for h in range(N_HEADS):
            base = h * QK_DIM
            q_ref[0, h, :HEAD_DIM, :] = (q_t[base:base + HEAD_DIM] * Q_PRESCALE).astype(BF16)
            pe = q_t[base + HEAD_DIM:base + QK_DIM]
            pe = pe * cos4_t + pltpu.roll(pe, 2 * ROPE_HALF, 0) * sin4_t
            q_ref[0, h, HEAD_DIM:, :] = (pe * Q_PRESCALE).astype(BF16)

        kvn = _rmsnorm(z_ref[:, Q_RANK:Q_RANK + KV_RANK], kvn_ref[...]).astype(BF16)
        k_nope = _dot(kvn, wk_ref[...])
        v_t = _dot_nt(wv_ref[...], kvn)
        k_pe = _rope(z_ref[:, Q_RANK + KV_RANK:Q_RANK + KV_RANK + LANES], cos4_t.T, sin4_t.T).astype(BF16)
        for h in range(N_HEADS):
            k_ref[0, h, :, :HEAD_DIM] = k_nope[:, h * HEAD_DIM:(h + 1) * HEAD_DIM].astype(BF16)
            k_ref[0, h, :, HEAD_DIM:] = k_pe
            v_ref[0, h, 0] = v_t[h * HEAD_DIM:(h + 1) * HEAD_DIM].astype(BF16)

        zp = z_ref[:, D_MODEL - POOL_WIDTH:]
        ext_ref[0:POOL_HALO, :] = jnp.where(seq_tile == 0, 0.0, ext_ref[0:POOL_HALO, :])
        ext_ref[POOL_HALO:, :] = zp
        t1 = (seq_tile * tm + 1 + lax.broadcasted_iota(jnp.int32, (tm, 1), 0)).astype(F32)
        assert POOL_WINDOWS == (2, 4, 8, 16) and POOL_HALO >= sum(POOL_WINDOWS) // 2
        level = ext_ref[...]
        diffs = []
        for g, w in enumerate(POOL_WINDOWS):
            level = level + pltpu.roll(level, w // 2, 0)
            mean = level[POOL_HALO:, :POOL_CH] / jnp.minimum(t1, float(w))
            diffs.append(mean - zp[:, g * POOL_CH:(g + 1) * POOL_CH])
            if g + 1 < len(POOL_WINDOWS):
                level = level[:, POOL_CH:]
        ext_ref[0:POOL_HALO, :] = ext_ref[tm:tm + POOL_HALO, :]

        y01 = _dot(jnp.concatenate(diffs[:2], axis=1).astype(BF16), poolw_ref[0])
        y23 = _dot(jnp.concatenate(diffs[2:], axis=1).astype(BF16), poolw_ref[1])
        p = jnp.concatenate([y01, y23], axis=1) * pscale_ref[...]
        h_ref[...] = h1_ref[...] + _dot(p.astype(BF16), woutp_ref[...])

    @pl.when(step < n_tiles)
    def _():
        mixer_inputs()
        h1, z, cos, sin = ffn_and_project()
        h1_ref[...] = h1
        z_ref[...] = z
        cos_ref[...] = cos
        sin_ref[...] = sin

    @pl.when(step == n_tiles)
    def _():
        mixer_inputs()


def _const_spec(shape):
    nd = len(shape)
    return pl.BlockSpec(shape, lambda *_: (0,) * nd, pipeline_mode=pl.Buffered(1))


def _pre(x2d, pos, freqs, n1, mixn, win, qn, wq, kvn, wk, wv, poolw, pscale, wg, wu, wd, wout, batch, seq):
    tm = TOKEN_TILE
    assert tm == KV_TILE
    n_tok = x2d.shape[0]
    tiles_per_seq = seq // tm
    n_tiles = n_tok // tm
    lag = lambda i: jnp.maximum(i - 1, 0)
    bt = lambda i: (lag(i) // tiles_per_seq, lag(i) % tiles_per_seq)
    x_spec = pl.BlockSpec((tm, D_MODEL), lambda i: (jnp.minimum(i, n_tiles - 1), 0))
    h_spec = pl.BlockSpec((tm, D_MODEL), lambda i: (lag(i), 0))
    pos_spec = pl.BlockSpec((batch, tm), lambda i: (0, jnp.minimum(i, n_tiles - 1) % tiles_per_seq))
    q_spec = pl.BlockSpec((1, N_HEADS, QK_DIM, tm), lambda i: (bt(i)[0], 0, 0, bt(i)[1]))
    k_spec = pl.BlockSpec((1, N_HEADS, tm, QK_DIM), lambda i: (bt(i)[0], 0, bt(i)[1], 0))
    v_spec = pl.BlockSpec((1, N_HEADS, 1, HEAD_DIM, tm), lambda i: (bt(i)[0], 0, bt(i)[1], 0, 0))
    weights = (n1, mixn, win, qn, wq, kvn, wk, wv, poolw, pscale)
    hbm_weights = (wg, wu, wd, wout)
    return pl.pallas_call(
        functools.partial(_pre_body, tiles_per_seq, n_tiles),
        grid=(n_tiles + 1,),
        in_specs=[x_spec, pos_spec, _const_spec(freqs.shape)] + [_const_spec(w.shape) for w in weights]
                 + [pl.BlockSpec(memory_space=pl.ANY)] * len(hbm_weights),
        out_specs=[h_spec, q_spec, k_spec, v_spec],
        out_shape=[jax.ShapeDtypeStruct((n_tok, D_MODEL), F32),
                   jax.ShapeDtypeStruct((batch, N_HEADS, QK_DIM, seq), BF16),
                   jax.ShapeDtypeStruct((batch, N_HEADS, seq, QK_DIM), BF16),
                   jax.ShapeDtypeStruct((batch, N_HEADS, tiles_per_seq, HEAD_DIM, tm), BF16)],
        scratch_shapes=[pltpu.VMEM((tm + POOL_HALO, POOL_WIDTH), F32),
                        pltpu.VMEM((tm, D_MODEL), F32), pltpu.VMEM((tm, D_MODEL), F32),
                        pltpu.VMEM((ROPE_HALF, tm), F32), pltpu.VMEM((ROPE_HALF, tm), F32),
                        pltpu.VMEM((D_MODEL, D_FF), BF16), pltpu.VMEM((D_MODEL, D_FF), BF16),
                        pltpu.VMEM((D_FF, D_MODEL), BF16), pltpu.VMEM((POOL_WIDTH, D_MODEL), BF16)]
                       + _cast_scratch(CAST_ROWS_WIDE, CAST_ROWS),
        compiler_params=pltpu.CompilerParams(dimension_semantics=("arbitrary",),
                                             vmem_limit_bytes=VMEM_LIMIT_BYTES),
        name="pre",
    )(x2d, pos, freqs, *weights, *hbm_weights)


def _attn_body(n_cast, q_ref, k_ref, v_ref, mem_ref, memn_ref, wmkv_ref, *refs):
    cast_in, o_ref, mkv_ref = refs[:n_cast], refs[n_cast], refs[n_cast + 1]
    cast_out = refs[n_cast + 2:2 * n_cast + 2]
    m_ref, l_ref, acc_ref, gap_ref = refs[2 * n_cast + 2:]

    def cast_slices():
        for src, dst in zip(cast_in, cast_out):
            dst[...] = src[...].astype(BF16)

    step = pl.program_id(1)

    @pl.when(step == 0)
    def _():
        mn = _rmsnorm(mem_ref[0], memn_ref[...]).astype(BF16)
        mkv_ref[0] = _dot(mn, wmkv_ref[...]).astype(BF16)

    cast_slices()
    for t in range(TILES_PER_ATTN_STEP):
        _attn_tile(TILES_PER_ATTN_STEP * step + t, q_ref.at[:, :, :, pl.ds(t * Q_TILE, Q_TILE)], k_ref, v_ref,
                   o_ref.at[:, pl.ds(t * Q_TILE, Q_TILE), :], m_ref, l_ref, acc_ref, gap_ref)


def _attn_tile(qi, q_ref, k_ref, v_ref, o_ref, m_ref, l_ref, acc_ref, gap_ref):
    tq = q_ref.shape[3]
    tk = KV_TILE
    half = tk // 2

    def scores(h, c):
        return _dot(k_ref[0, h, pl.ds(pl.multiple_of(c * tk, tk), tk), :], q_ref[0, h])


    def init_with_diagonal():
        causal = (lax.broadcasted_iota(jnp.int32, (half, half), 0)
                  <= lax.broadcasted_iota(jnp.int32, (half, half), 1))
        start = pl.multiple_of(qi * tk, tk)
        s_lo = [_dot(k_ref[0, h, pl.ds(start, half), :], q_ref[0, h]) for h in range(N_HEADS)]
        s_hi = [_dot(k_ref[0, h, pl.ds(start + half, half), :], q_ref[0, h, :, half:]) for h in range(N_HEADS)]
        probs = []
        for h in range(N_HEADS):
            s_ll = jnp.where(causal, s_lo[h][:, :half], -jnp.inf)
            s_lu = s_lo[h][:, half:]
            s_uu = jnp.where(causal, s_hi[h], -jnp.inf)
            m_l = jnp.max(s_ll, axis=0, keepdims=True)
            m_u = jnp.maximum(jnp.max(s_lu, axis=0, keepdims=True), jnp.max(s_uu, axis=0, keepdims=True))
            p_ll = jnp.exp2(s_ll - m_l)
            p_lu = jnp.exp2(s_lu - m_u)
            p_uu = jnp.exp2(s_uu - m_u)
            l_l = jnp.sum(p_ll, axis=0, keepdims=True)
            l_u = jnp.sum(p_lu, axis=0, keepdims=True) + jnp.sum(p_uu, axis=0, keepdims=True)
            m_ref[h, :, :half] = jnp.broadcast_to(m_l, (STAT_ROWS, half))
            m_ref[h, :, half:] = jnp.broadcast_to(m_u, (STAT_ROWS, half))
            l_ref[h, :, :half] = jnp.broadcast_to(l_l, (STAT_ROWS, half))
            l_ref[h, :, half:] = jnp.broadcast_to(l_u, (STAT_ROWS, half))
            probs.append((jnp.concatenate([p_ll, p_lu], axis=1).astype(BF16), p_uu.astype(BF16)))
        for h in range(N_HEADS):
            v_t = v_ref[0, h, qi]
            acc_l = _dot(v_t[:, :half], probs[h][0])
            acc_u = _dot(v_t[:, half:], probs[h][1])
            acc_ref[h, :, :half] = acc_l[:, :half]
            acc_ref[h, :, half:] = acc_l[:, half:] + acc_u

    def update_single_pass(blocks):
        s_all = [[scores(h, c) for c in blocks] for h in range(N_HEADS)]
        betas, probs = [], []
        for h in range(N_HEADS):
            m_old = m_ref[h, 0:1, :]
            ps = [jnp.exp2(s - m_old) for s in s_all[h]]
            m_blk = jnp.max(s_all[h][0], axis=0, keepdims=True)
            for s in s_all[h][1:]:
                m_blk = jnp.maximum(m_blk, jnp.max(s, axis=0, keepdims=True))
            m_new = jnp.maximum(m_old, m_blk)
            beta = jnp.exp2(m_old - m_new)
            l_new = l_ref[h, 0:1, :]
            for p in ps:
                l_new = l_new + jnp.sum(p, axis=0, keepdims=True)
            m_ref[h] = jnp.broadcast_to(m_new, (STAT_ROWS, tq))
            l_ref[h] = jnp.broadcast_to(l_new * beta, (STAT_ROWS, tq))
            gap_ref[h] = jnp.maximum(gap_ref[h], jnp.broadcast_to(m_blk - m_old, (STAT_ROWS, tq)))
            betas.append(beta)
            probs.append([p.astype(BF16) for p in ps])
        for h in range(N_HEADS):
            acc = acc_ref[h]
            for c, p in zip(blocks, probs[h]):
                acc = acc + _dot(v_ref[0, h, c], p)
            acc_ref[h] = acc * betas[h]

    def update_two_pass(c):
        s_all = [scores(h, c) for h in range(N_HEADS)]
        alphas, probs = [], []
        for h in range(N_HEADS):
            m_old = m_ref[h, 0:1, :]
            m_new = jnp.maximum(m_old, jnp.max(s_all[h], axis=0, keepdims=True))
            alpha = jnp.exp2(m_old - m_new)
            p = jnp.exp2(s_all[h] - m_new)
            l_new = alpha * l_ref[h, 0:1, :] + jnp.sum(p, axis=0, keepdims=True)
            m_ref[h] = jnp.broadcast_to(m_new, (STAT_ROWS, tq))
            l_ref[h] = jnp.broadcast_to(l_new, (STAT_ROWS, tq))
            alphas.append(alpha)
            probs.append(p.astype(BF16))
        for h in range(N_HEADS):
            acc_ref[h] = alphas[h] * acc_ref[h] + _dot(v_ref[0, h, c], probs[h])

    @pl.when(qi % 2 == 0)
    def _():
        init_with_diagonal()
        gap_ref[...] = jnp.zeros(gap_ref.shape, F32)

    @pl.when(qi % 2 == 1)
    def _():
        init_with_diagonal()
        gap_ref[...] = jnp.zeros(gap_ref.shape, F32)
        update_single_pass([qi - 1])

    def pair(j, _):
        update_single_pass([2 * j, 2 * j + 1])
        return 0

    lax.fori_loop(0, lax.shift_right_logical(qi, 1), pair, 0)

    def write_output():
        for h in range(N_HEADS):
            o_ref[0, :, h * HEAD_DIM:(h + 1) * HEAD_DIM] = (acc_ref[h] / l_ref[h, 0:1, :]).T.astype(BF16)

    worst = gap_ref[0]
    for h in range(1, N_HEADS):
        worst = jnp.maximum(worst, gap_ref[h])
    write_output()

    @pl.when(jnp.max(worst) > MAX_SINGLE_PASS_JUMP)
    def _():
        init_with_diagonal()

        def one(c, _):
            update_two_pass(c)
            return 0

        lax.fori_loop(0, qi, one, 0)
        write_output()


def _cast_block_specs(w, n_rows, steps, n_q):
    cols = w.shape[1]
    per_step = n_rows // steps
    blk = per_step if n_rows % steps == 0 and per_step % BF16_SUBLANES == 0 else LANES
    n_blk = n_rows // blk
    assert n_rows % blk == 0 and n_blk <= steps
    index = lambda b, i: (jnp.minimum(b * n_q + i, n_blk - 1), 0)
    return (pl.BlockSpec((blk, cols), index), pl.BlockSpec((blk, cols), index),
            jax.ShapeDtypeStruct((n_rows, cols), BF16))


def _mla_attention(q, k, v, mem, mem_norm, w_mkv, cast_jobs):
    batch, _, seq, _ = k.shape
    assert Q_TILE == KV_TILE
    step_rows = TILES_PER_ATTN_STEP * Q_TILE
    n_steps = seq // step_rows
    cast_specs = [_cast_block_specs(w, rows, batch * n_steps, n_steps) for w, rows in cast_jobs]
    outs = pl.pallas_call(
        functools.partial(_attn_body, len(cast_jobs)),
        grid=(batch, n_steps),
        in_specs=[pl.BlockSpec((1, N_HEADS, QK_DIM, step_rows), lambda b, i: (b, 0, 0, i)),
                  pl.BlockSpec((1, N_HEADS, seq, QK_DIM), lambda b, i: (b, 0, 0, 0)),
                  pl.BlockSpec((1, N_HEADS, seq // KV_TILE, HEAD_DIM, KV_TILE), lambda b, i: (b, 0, 0, 0, 0)),
                  pl.BlockSpec((1, N_MEM, D_MODEL), lambda b, i: (b, 0, 0)),
                  _const_spec(mem_norm.shape), _const_spec(w_mkv.shape)]
                 + [c[0] for c in cast_specs],
        out_specs=[pl.BlockSpec((1, step_rows, ATT_WIDTH), lambda b, i: (b, i, 0)),
                   pl.BlockSpec((1, N_MEM, 2 * D_MODEL), lambda b, i: (b, 0, 0))] + [c[1] for c in cast_specs],
        out_shape=[jax.ShapeDtypeStruct((batch, seq, ATT_WIDTH), BF16),
                   jax.ShapeDtypeStruct((batch, N_MEM, 2 * D_MODEL), BF16)] + [c[2] for c in cast_specs],
        scratch_shapes=[pltpu.VMEM((N_HEADS, STAT_ROWS, Q_TILE), F32),
                        pltpu.VMEM((N_HEADS, STAT_ROWS, Q_TILE), F32),
                        pltpu.VMEM((N_HEADS, HEAD_DIM, Q_TILE), F32),
                        pltpu.VMEM((N_HEADS, STAT_ROWS, Q_TILE), F32)],
        compiler_params=pltpu.CompilerParams(dimension_semantics=("arbitrary", "arbitrary"),
                                             vmem_limit_bytes=VMEM_LIMIT_BYTES),
        name="mla_attn",
    )(q, k, v, mem, mem_norm, w_mkv, *[w for w, _ in cast_jobs])
    return outs[0], outs[1], outs[2:]


def _post_body(h_ref, a_ref, mkv_ref, xn_ref, n2_ref, fn_ref, wouta_ref, wmq_ref, wmo_ref, wg_ref, wu_ref, wd_ref,
               out_ref):
    def mix(rows):
        h2 = h_ref[rows, :] + _dot(a_ref[rows, :], wouta_ref[...])

        hn = _rmsnorm(h2, xn_ref[...]).astype(BF16)
        q = (_dot(hn, wmq_ref[...]) * MEM_SCALE).astype(BF16)
        cols = [slice(h * MEM_HEAD_DIM, (h + 1) * MEM_HEAD_DIM) for h in range(MEM_HEADS)]
        scores = [_dot_nt(q[:, c], mkv_ref[0, :, c]) for c in cols]
        probs = []
        for s in scores:
            e = jnp.exp(s - jnp.max(s, axis=-1, keepdims=True))
            probs.append((e / jnp.sum(e, axis=-1, keepdims=True)).astype(BF16))
        heads = [_dot(p, mkv_ref[0, :, D_MODEL + h * MEM_HEAD_DIM:D_MODEL + (h + 1) * MEM_HEAD_DIM]).astype(BF16)
                 for h, p in enumerate(probs)]
        return h2 + _dot(jnp.concatenate(heads, axis=1), wmo_ref[...])

    def ffn(rows, h3):
        xn = _rmsnorm(h3, n2_ref[...]).astype(BF16)
        h4 = h3 + 0.5 * _swiglu(xn, wg_ref, wu_ref, wd_ref)
        out_ref[rows, :] = _rmsnorm(h4, fn_ref[...])

    tm = h_ref.shape[0]
    groups = [slice(r, r + POST_ROW_GROUP) for r in range(0, tm, POST_ROW_GROUP)]
    mixed = [mix(rows) for rows in groups]
    for rows, h3 in zip(groups, mixed):
        ffn(rows, h3)


def _post(h2d, a2d, mkv, xn, n2, fn, wouta, wmq, wmo, wg, wu, wd, seq):
    tm = POST_TILE
    n_tok = h2d.shape[0]
    tiles_per_seq = seq // tm
    weights = (xn, n2, fn, wouta, wmq, wmo, wg, wu, wd)
    return pl.pallas_call(
        _post_body,
        grid=(n_tok // tm,),
        in_specs=[pl.BlockSpec((tm, D_MODEL), lambda i: (i, 0)),
                  pl.BlockSpec((tm, ATT_WIDTH), lambda i: (i, 0)),
                  pl.BlockSpec((1, N_MEM, 2 * D_MODEL), lambda i: (i // tiles_per_seq, 0, 0))]
                 + [_const_spec(w.shape) for w in weights],
        out_specs=pl.BlockSpec((tm, D_MODEL), lambda i: (i, 0)),
        out_shape=jax.ShapeDtypeStruct((n_tok, D_MODEL), F32),
        compiler_params=pltpu.CompilerParams(dimension_semantics=("arbitrary",),
                                             vmem_limit_bytes=VMEM_LIMIT_BYTES),
        name="post",
    )(h2d, a2d, mkv, *weights)


def _pad_rope_cols(w):
    zeros = jnp.zeros(w.shape[:-1] + (ROPE_HALF,), w.dtype)
    return jnp.concatenate([w[..., :ROPE_HALF], zeros, w[..., ROPE_HALF:], zeros], axis=-1)


def _block_diag2(a, b):
    za = jnp.zeros((a.shape[0], b.shape[1]), a.dtype)
    zb = jnp.zeros((b.shape[0], a.shape[1]), a.dtype)
    return jnp.concatenate([jnp.concatenate([a, za], axis=1), jnp.concatenate([zb, b], axis=1)], axis=0)


def _layer(h2d, mem, pos, freqs, batch, seq, ffn1_norm, ffn1_w_gate, ffn1_w_up, ffn1_w_down, mix_norm, w_in,
           q_norm, w_q_up, kv_norm, w_kv_up, pool_w, pool_scale, w_out, xattn_norm, mem_norm,
           w_mq, w_mkv, w_mo, ffn2_norm, ffn2_w_gate, ffn2_w_up, ffn2_w_down, out_norm):
    row = lambda g: g.reshape(1, -1).astype(F32)
    rope_lo = Q_RANK + KV_RANK
    w_in_p = jnp.concatenate([w_in[:, :rope_lo], _pad_rope_cols(w_in[:, rope_lo:rope_lo + ROPE_DIM]),
                              w_in[:, rope_lo + ROPE_DIM:]], axis=1).astype(BF16)
    wq = w_q_up.reshape(Q_RANK, N_HEADS, HEAD_DIM + ROPE_DIM)
    wq_p = jnp.concatenate([wq[..., :HEAD_DIM], _pad_rope_cols(wq[..., HEAD_DIM:])], axis=-1)
    wq_t = wq_p.reshape(Q_RANK, N_HEADS * QK_DIM).T.astype(BF16)
    wkv = w_kv_up.reshape(KV_RANK, N_HEADS, 2 * HEAD_DIM)
    wk = wkv[..., :HEAD_DIM].reshape(KV_RANK, ATT_WIDTH).astype(BF16)
    wv_t = wkv[..., HEAD_DIM:].reshape(KV_RANK, ATT_WIDTH).T.astype(BF16)
    poolw = jnp.stack([_block_diag2(pool_w[0], pool_w[1]), _block_diag2(pool_w[2], pool_w[3])]).astype(BF16)

    h1p, q, k, v = _pre(h2d, pos, freqs, row(ffn1_norm), row(mix_norm), w_in_p, row(q_norm), wq_t, row(kv_norm),
                        wk, wv_t, poolw, row(pool_scale), ffn1_w_gate, ffn1_w_up, ffn1_w_down, w_out, batch, seq)
    a, mkv, post_weights = _mla_attention(
        q, k, v, mem, row(mem_norm), w_mkv.astype(BF16),
        [(w_out, ATT_WIDTH), (w_mq, D_MODEL), (w_mo, D_MODEL),
         (ffn2_w_gate, D_MODEL), (ffn2_w_up, D_MODEL), (ffn2_w_down, D_FF)])
    return _post(h1p, a.reshape(batch * seq, ATT_WIDTH), mkv, row(xattn_norm), row(ffn2_norm), row(out_norm),
                 *post_weights, seq)


def kernel(x, mem, positions, ffn1_norm, ffn1_w_gate, ffn1_w_up, ffn1_w_down, mix_norm, w_in, q_norm, w_q_up,
           kv_norm, w_kv_up, pool_w, pool_scale, w_out, xattn_norm, mem_norm, w_mq, w_mkv, w_mo, ffn2_norm,
           ffn2_w_gate, ffn2_w_up, ffn2_w_down, final_norm):
    batch, seq, d_model = x.shape
    depth = ffn1_norm.shape[0]
    assert d_model == D_MODEL and depth == 1
    assert seq % TOKEN_TILE == 0 and seq % POST_TILE == 0 and seq % (TILES_PER_ATTN_STEP * Q_TILE) == 0
    freqs = 1.0 / (ROPE_BASE ** (jnp.arange(0, ROPE_DIM, 2, dtype=F32) / ROPE_DIM))
    layer = (ffn1_norm, ffn1_w_gate, ffn1_w_up, ffn1_w_down, mix_norm, w_in, q_norm, w_q_up, kv_norm, w_kv_up,
             pool_w, pool_scale, w_out, xattn_norm, mem_norm, w_mq, w_mkv, w_mo, ffn2_norm, ffn2_w_gate,
             ffn2_w_up, ffn2_w_down)
    out = _layer(x.reshape(batch * seq, d_model), mem, positions,
                 freqs.reshape(ROPE_HALF, 1), batch, seq, *[w[0] for w in layer], final_norm)
    return out.reshape(batch, seq, d_model)
```

```python
import functools

import jax
import jax.numpy as jnp
from jax import lax
from jax.experimental import pallas as pl
from jax.experimental.pallas import tpu as pltpu

F32 = jnp.float32
BF16 = jnp.bfloat16

D_MODEL = 1024
N_HEADS = 4
HEAD_DIM = 128
ROPE_DIM = 64
ROPE_HALF = ROPE_DIM // 2
Q_RANK = 256
KV_RANK = 128
ATT_WIDTH = N_HEADS * HEAD_DIM
POOL_WIDTH = D_MODEL - ATT_WIDTH
POOL_WINDOWS = (2, 4, 8, 16)
POOL_CH = POOL_WIDTH // len(POOL_WINDOWS)
POOL_HALO = 16
D_FF = 2816
N_MEM = 256
MEM_HEADS = 4
MEM_HEAD_DIM = D_MODEL // MEM_HEADS
ROPE_BASE = 10000.0
RMS_EPS = 1e-6
ATT_SCALE = (HEAD_DIM + ROPE_DIM) ** -0.5
LOG2_E = 1.4426950408889634
Q_PRESCALE = ATT_SCALE * LOG2_E
MEM_SCALE = MEM_HEAD_DIM ** -0.5

LANES = 128
BF16_SUBLANES = 16
MXU_DIM = 256
VMEM_LIMIT_BYTES = 56 * 1024 * 1024

TOKEN_TILE = 512
POST_TILE = 1024
POST_ROW_GROUP = 512
FF_CHUNK = MXU_DIM
Q_TILE = 512
KV_TILE = 512
QK_DIM = 2 * HEAD_DIM
CAST_ROWS_WIDE = 128
CAST_ROWS = 256
CAST_SLOTS = 4
STAT_ROWS = 8
MAX_SINGLE_PASS_JUMP = 32.0


def _rmsnorm(x, g):
    ms = jnp.mean(x * x, axis=-1, keepdims=True)
    return (x * lax.rsqrt(ms + RMS_EPS)) * g


def _dot(a, b):
    return jnp.dot(a, b, preferred_element_type=F32)


def _dot_nt(a, b):
    return lax.dot_general(a, b, (((1,), (1,)), ((), ())), preferred_element_type=F32)


def _swiglu(xn, wg_ref, wu_ref, wd_ref):
    bounds = list(range(0, D_FF, FF_CHUNK)) + [D_FF]
    acts = []
    for lo, hi in zip(bounds[:-1], bounds[1:]):
        g = _dot(xn, wg_ref[:, lo:hi])
        u = _dot(xn, wu_ref[:, lo:hi])
        acts.append(((g / (1.0 + jnp.exp(-g))) * u).astype(BF16))
    return _dot(jnp.concatenate(acts, axis=1), wd_ref[...])


def _row_jobs(src, src_row0, n_rows, dst, chunk):
    assert n_rows % chunk == 0
    return [(src, src_row0 + r, dst, r) for r in range(0, n_rows, chunk)]


def _cast_weights_into_vmem(jobs, stage, sems):
    slots, rows = stage.shape[0], stage.shape[1]
    ahead = slots - 1

    def copy(j):
        src, r0, _, _ = jobs[j]
        return pltpu.make_async_copy(src.at[pl.ds(r0, rows), :], stage.at[j % slots], sems.at[j % slots])

    for j in range(min(ahead, len(jobs))):
        copy(j).start()
    for j in range(len(jobs)):
        if j + ahead < len(jobs):
            copy(j + ahead).start()
        copy(j).wait()
        _, _, dst, d0 = jobs[j]
        dst[d0:d0 + rows, :] = stage[j % slots].astype(BF16)


def _cast_scratch(wide_rows, narrow_rows):
    return [pltpu.VMEM((CAST_SLOTS, wide_rows, D_FF), F32), pltpu.VMEM((CAST_SLOTS, narrow_rows, D_MODEL), F32),
            pltpu.SemaphoreType.DMA((CAST_SLOTS,)), pltpu.SemaphoreType.DMA((CAST_SLOTS,))]


def _rope(x, cos4, sin4):
    return x * cos4 + pltpu.roll(x, 2 * ROPE_HALF, 1) * sin4


def _pre_body(tiles_per_seq, n_tiles,
              x_ref, pos_ref, freq_ref, n1_ref, mixn_ref, win_ref, qn_ref, wq_ref, kvn_ref, wk_ref, wv_ref,
              poolw_ref, pscale_ref, wg_hbm, wu_hbm, wd_hbm, wout_hbm,
              h_ref, q_ref, k_ref, v_ref,
              ext_ref, h1_ref, z_ref, cos_ref, sin_ref, wg_ref, wu_ref, wd_ref, woutp_ref,
              stage_wide, stage, sem_wide, sem):
    step = pl.program_id(0)
    tm = x_ref.shape[0]

    @pl.when(step == 0)
    def _():
        _cast_weights_into_vmem(_row_jobs(wg_hbm, 0, D_MODEL, wg_ref, CAST_ROWS_WIDE)
                                + _row_jobs(wu_hbm, 0, D_MODEL, wu_ref, CAST_ROWS_WIDE), stage_wide, sem_wide)
        _cast_weights_into_vmem(_row_jobs(wd_hbm, 0, D_FF, wd_ref, CAST_ROWS)
                                + _row_jobs(wout_hbm, ATT_WIDTH, POOL_WIDTH, woutp_ref, CAST_ROWS), stage, sem)
        h1_ref[...] = jnp.zeros(h1_ref.shape, F32)
        z_ref[...] = jnp.zeros(z_ref.shape, F32)
        ext_ref[...] = jnp.zeros(ext_ref.shape, F32)
        cos_ref[...] = jnp.zeros(cos_ref.shape, F32)
        sin_ref[...] = jnp.zeros(sin_ref.shape, F32)

    def ffn_and_project():
        x = x_ref[...]
        xn = _rmsnorm(x, n1_ref[...]).astype(BF16)
        h1 = x + 0.5 * _swiglu(xn, wg_ref, wu_ref, wd_ref)
        un = _rmsnorm(h1, mixn_ref[...]).astype(BF16)
        z = _dot(un, win_ref[...])
        batch_row = jnp.minimum(step, n_tiles - 1) // tiles_per_seq
        ang = freq_ref[...] * pos_ref[pl.ds(batch_row, 1), :].astype(F32)
        return h1, z, jnp.cos(ang), jnp.sin(ang)

    def mixer_inputs():
        seq_tile = jnp.maximum(step - 1, 0) % tiles_per_seq

        cos = cos_ref[...]
        sin = sin_ref[...]
        cos4_t = jnp.concatenate([cos, cos, cos, cos], axis=0)
        sin4_t = jnp.concatenate([-sin, -sin, sin, sin], axis=0)

        qn = _rmsnorm(z_ref[:, :Q_RANK], qn_ref[...]).astype(BF16)
        q_t = _dot_nt(wq_ref[...], qn)
        for h in range(N_HEADS):
            base = h * QK_DIM
            q_ref[0, h, :HEAD_DIM, :] = (q_t[base:base + HEAD_DIM] * Q_PRESCALE).astype(BF16)
            pe = q_t[base + HEAD_DIM:base + QK_DIM]
            pe = pe * cos4_t + pltpu.roll(pe, 2 * ROPE_HALF, 0) * sin4_t
            q_ref[0, h, HEAD_DIM:, :] = (pe * Q_PRESCALE).astype(BF16)

        kvn = _rmsnorm(z_ref[:, Q_RANK:Q_RANK + KV_RANK], kvn_ref[...]).astype(BF16)
        k_nope = _dot(kvn, wk_ref[...])
        v_t = _dot_nt(wv_ref[...], kvn)
        k_pe = _rope(z_ref[:, Q_RANK + KV_RANK:Q_RANK + KV_RANK + LANES], cos4_t.T, sin4_t.T).astype(BF16)
        for h in range(N_HEADS):
            k_ref[0, h, :, :HEAD_DIM] = k_nope[:, h * HEAD_DIM:(h + 1) * HEAD_DIM].astype(BF16)
            k_ref[0, h, :, HEAD_DIM:] = k_pe
            v_ref[0, h, 0] = v_t[h * HEAD_DIM:(h + 1) * HEAD_DIM].astype(BF16)

        zp = z_ref[:, D_MODEL - POOL_WIDTH:]
        ext_ref[0:POOL_HALO, :] = jnp.where(seq_tile == 0, 0.0, ext_ref[0:POOL_HALO, :])
        ext_ref[POOL_HALO:, :] = zp
        t1 = (seq_tile * tm + 1 + lax.broadcasted_iota(jnp.int32, (tm, 1), 0)).astype(F32)
        assert POOL_WINDOWS == (2, 4, 8, 16) and POOL_HALO >= sum(POOL_WINDOWS) // 2
        level = ext_ref[...]
        diffs = []
        for g, w in enumerate(POOL_WINDOWS):
            level = level + pltpu.roll(level, w // 2, 0)
            mean = level[POOL_HALO:, :POOL_CH] / jnp.minimum(t1, float(w))
            diffs.append(mean - zp[:, g * POOL_CH:(g + 1) * POOL_CH])
            if g + 1 < len(POOL_WINDOWS):
                level = level[:, POOL_CH:]
        ext_ref[0:POOL_HALO, :] = ext_ref[tm:tm + POOL_HALO, :]

        y01 = _dot(jnp.concatenate(diffs[:2], axis=1).astype(BF16), poolw_ref[0])
        y23 = _dot(jnp.concatenate(diffs[2:], axis=1).astype(BF16), poolw_ref[1])
        p = jnp.concatenate([y01, y23], axis=1) * pscale_ref[...]
        h_ref[...] = h1_ref[...] + _dot(p.astype(BF16), woutp_ref[...])

    @pl.when(step < n_tiles)
    def _():
        mixer_inputs()
        h1, z, cos, sin = ffn_and_project()
        h1_ref[...] = h1
        z_ref[...] = z
        cos_ref[...] = cos
        sin_ref[...] = sin

    @pl.when(step == n_tiles)
    def _():
        mixer_inputs()


def _const_spec(shape):
    nd = len(shape)
    return pl.BlockSpec(shape, lambda *_: (0,) * nd, pipeline_mode=pl.Buffered(1))


def _pre(x2d, pos, freqs, n1, mixn, win, qn, wq, kvn, wk, wv, poolw, pscale, wg, wu, wd, wout, batch, seq):
    tm = TOKEN_TILE
    assert tm == KV_TILE
    n_tok = x2d.shape[0]
    tiles_per_seq = seq // tm
    n_tiles = n_tok // tm
    lag = lambda i: jnp.maximum(i - 1, 0)
    bt = lambda i: (lag(i) // tiles_per_seq, lag(i) % tiles_per_seq)
    x_spec = pl.BlockSpec((tm, D_MODEL), lambda i: (jnp.minimum(i, n_tiles - 1), 0))
    h_spec = pl.BlockSpec((tm, D_MODEL), lambda i: (lag(i), 0))
    pos_spec = pl.BlockSpec((batch, tm), lambda i: (0, jnp.minimum(i, n_tiles - 1) % tiles_per_seq))
    q_spec = pl.BlockSpec((1, N_HEADS, QK_DIM, tm), lambda i: (bt(i)[0], 0, 0, bt(i)[1]))
    k_spec = pl.BlockSpec((1, N_HEADS, tm, QK_DIM), lambda i: (bt(i)[0], 0, bt(i)[1], 0))
    v_spec = pl.BlockSpec((1, N_HEADS, 1, HEAD_DIM, tm), lambda i: (bt(i)[0], 0, bt(i)[1], 0, 0))
    weights = (n1, mixn, win, qn, wq, kvn, wk, wv, poolw, pscale)
    hbm_weights = (wg, wu, wd, wout)
    return pl.pallas_call(
        functools.partial(_pre_body, tiles_per_seq, n_tiles),
        grid=(n_tiles + 1,),
        in_specs=[x_spec, pos_spec, _const_spec(freqs.shape)] + [_const_spec(w.shape) for w in weights]
                 + [pl.BlockSpec(memory_space=pl.ANY)] * len(hbm_weights),
        out_specs=[h_spec, q_spec, k_spec, v_spec],
        out_shape=[jax.ShapeDtypeStruct((n_tok, D_MODEL), F32),
                   jax.ShapeDtypeStruct((batch, N_HEADS, QK_DIM, seq), BF16),
                   jax.ShapeDtypeStruct((batch, N_HEADS, seq, QK_DIM), BF16),
                   jax.ShapeDtypeStruct((batch, N_HEADS, tiles_per_seq, HEAD_DIM, tm), BF16)],
        scratch_shapes=[pltpu.VMEM((tm + POOL_HALO, POOL_WIDTH), F32),
                        pltpu.VMEM((tm, D_MODEL), F32), pltpu.VMEM((tm, D_MODEL), F32),
                        pltpu.VMEM((ROPE_HALF, tm), F32), pltpu.VMEM((ROPE_HALF, tm), F32),
                        pltpu.VMEM((D_MODEL, D_FF), BF16), pltpu.VMEM((D_MODEL, D_FF), BF16),
                        pltpu.VMEM((D_FF, D_MODEL), BF16), pltpu.VMEM((POOL_WIDTH, D_MODEL), BF16)]
                       + _cast_scratch(CAST_ROWS_WIDE, CAST_ROWS),
        compiler_params=pltpu.CompilerParams(dimension_semantics=("arbitrary",),
                                             vmem_limit_bytes=VMEM_LIMIT_BYTES),
        name="pre",
    )(x2d, pos, freqs, *weights, *hbm_weights)


def _attn_body(n_cast, q_ref, k_ref, v_ref, mem_ref, memn_ref, wmkv_ref, *refs):
    cast_in, o_ref, mkv_ref = refs[:n_cast], refs[n_cast], refs[n_cast + 1]
    cast_out = refs[n_cast + 2:2 * n_cast + 2]
    m_ref, l_ref, acc_ref, gap_ref = refs[2 * n_cast + 2:]

    def cast_slices():
        for src, dst in zip(cast_in, cast_out):
            dst[...] = src[...].astype(BF16)

    qi = pl.program_id(1)

    @pl.when(qi == 0)
    def _():
        mn = _rmsnorm(mem_ref[0], memn_ref[...]).astype(BF16)
        mkv_ref[0] = _dot(mn, wmkv_ref[...]).astype(BF16)

    tq = q_ref.shape[3]
    tk = KV_TILE
    half = tk // 2

    def scores(h, c):
        return _dot(k_ref[0, h, pl.ds(pl.multiple_of(c * tk, tk), tk), :], q_ref[0, h])


    def init_with_diagonal():
        causal = (lax.broadcasted_iota(jnp.int32, (half, half), 0)
                  <= lax.broadcasted_iota(jnp.int32, (half, half), 1))
        start = pl.multiple_of(qi * tk, tk)
        s_lo = [_dot(k_ref[0, h, pl.ds(start, half), :], q_ref[0, h]) for h in range(N_HEADS)]
        s_hi = [_dot(k_ref[0, h, pl.ds(start + half, half), :], q_ref[0, h, :, half:]) for h in range(N_HEADS)]
        probs = []
        for h in range(N_HEADS):
            s_ll = jnp.where(causal, s_lo[h][:, :half], -jnp.inf)
            s_lu = s_lo[h][:, half:]
            s_uu = jnp.where(causal, s_hi[h], -jnp.inf)
            m_l = jnp.max(s_ll, axis=0, keepdims=True)
            m_u = jnp.maximum(jnp.max(s_lu, axis=0, keepdims=True), jnp.max(s_uu, axis=0, keepdims=True))
            p_ll = jnp.exp2(s_ll - m_l)
            p_lu = jnp.exp2(s_lu - m_u)
            p_uu = jnp.exp2(s_uu - m_u)
            l_l = jnp.sum(p_ll, axis=0, keepdims=True)
            l_u = jnp.sum(p_lu, axis=0, keepdims=True) + jnp.sum(p_uu, axis=0, keepdims=True)
            m_ref[h, :, :half] = jnp.broadcast_to(m_l, (STAT_ROWS, half))
            m_ref[h, :, half:] = jnp.broadcast_to(m_u, (STAT_ROWS, half))
            l_ref[h, :, :half] = jnp.broadcast_to(l_l, (STAT_ROWS, half))
            l_ref[h, :, half:] = jnp.broadcast_to(l_u, (STAT_ROWS, half))
            probs.append((jnp.concatenate([p_ll, p_lu], axis=1).astype(BF16), p_uu.astype(BF16)))
        for h in range(N_HEADS):
            v_t = v_ref[0, h, qi]
            acc_l = _dot(v_t[:, :half], probs[h][0])
            acc_u = _dot(v_t[:, half:], probs[h][1])
            acc_ref[h, :, :half] = acc_l[:, :half]
            acc_ref[h, :, half:] = acc_l[:, half:] + acc_u

    def update_single_pass(blocks):
        s_all = [[scores(h, c) for c in blocks] for h in range(N_HEADS)]
        betas, probs = [], []
        for h in range(N_HEADS):
            m_old = m_ref[h, 0:1, :]
            ps = [jnp.exp2(s - m_old) for s in s_all[h]]
            m_blk = jnp.max(s_all[h][0], axis=0, keepdims=True)
            for s in s_all[h][1:]:
                m_blk = jnp.maximum(m_blk, jnp.max(s, axis=0, keepdims=True))
            m_new = jnp.maximum(m_old, m_blk)
            beta = jnp.exp2(m_old - m_new)
            l_new = l_ref[h, 0:1, :]
            for p in ps:
                l_new = l_new + jnp.sum(p, axis=0, keepdims=True)
            m_ref[h] = jnp.broadcast_to(m_new, (STAT_ROWS, tq))
            l_ref[h] = jnp.broadcast_to(l_new * beta, (STAT_ROWS, tq))
            gap_ref[h] = jnp.maximum(gap_ref[h], jnp.broadcast_to(m_blk - m_old, (STAT_ROWS, tq)))
            betas.append(beta)
            probs.append([p.astype(BF16) for p in ps])
        for h in range(N_HEADS):
            acc = acc_ref[h]
            for c, p in zip(blocks, probs[h]):
                acc = acc + _dot(v_ref[0, h, c], p)
            acc_ref[h] = acc * betas[h]

    def update_two_pass(c):
        s_all = [scores(h, c) for h in range(N_HEADS)]
        alphas, probs = [], []
        for h in range(N_HEADS):
            m_old = m_ref[h, 0:1, :]
            m_new = jnp.maximum(m_old, jnp.max(s_all[h], axis=0, keepdims=True))
            alpha = jnp.exp2(m_old - m_new)
            p = jnp.exp2(s_all[h] - m_new)
            l_new = alpha * l_ref[h, 0:1, :] + jnp.sum(p, axis=0, keepdims=True)
            m_ref[h] = jnp.broadcast_to(m_new, (STAT_ROWS, tq))
            l_ref[h] = jnp.broadcast_to(l_new, (STAT_ROWS, tq))
            alphas.append(alpha)
            probs.append(p.astype(BF16))
        for h in range(N_HEADS):
            acc_ref[h] = alphas[h] * acc_ref[h] + _dot(v_ref[0, h, c], probs[h])

    @pl.when(qi % 2 == 0)
    def _():
        cast_slices()
        init_with_diagonal()
        gap_ref[...] = jnp.zeros(gap_ref.shape, F32)

    @pl.when(qi % 2 == 1)
    def _():
        cast_slices()
        init_with_diagonal()
        gap_ref[...] = jnp.zeros(gap_ref.shape, F32)
        update_single_pass([qi - 1])

    def pair(j, _):
        update_single_pass([2 * j, 2 * j + 1])
        return 0

    lax.fori_loop(0, lax.shift_right_logical(qi, 1), pair, 0)

    def write_output():
        for h in range(N_HEADS):
            o_ref[0, :, h * HEAD_DIM:(h + 1) * HEAD_DIM] = (acc_ref[h] / l_ref[h, 0:1, :]).T.astype(BF16)

    worst = gap_ref[0]
    for h in range(1, N_HEADS):
        worst = jnp.maximum(worst, gap_ref[h])
    write_output()

    @pl.when(jnp.max(worst) > MAX_SINGLE_PASS_JUMP)
    def _():
        init_with_diagonal()

        def one(c, _):
            update_two_pass(c)
            return 0

        lax.fori_loop(0, qi, one, 0)
        write_output()


def _cast_block_specs(w, n_rows, steps, n_q):
    cols = w.shape[1]
    per_step = n_rows // steps
    blk = per_step if n_rows % steps == 0 and per_step % BF16_SUBLANES == 0 else LANES
    n_blk = n_rows // blk
    assert n_rows % blk == 0 and n_blk <= steps
    index = lambda b, i: (jnp.minimum(b * n_q + i, n_blk - 1), 0)
    return (pl.BlockSpec((blk, cols), index), pl.BlockSpec((blk, cols), index),
            jax.ShapeDtypeStruct((n_rows, cols), BF16))


def _mla_attention(q, k, v, mem, mem_norm, w_mkv, cast_jobs):
    batch, _, seq, _ = k.shape
    assert Q_TILE == KV_TILE
    n_q = seq // Q_TILE
    cast_specs = [_cast_block_specs(w, rows, batch * n_q, n_q) for w, rows in cast_jobs]
    outs = pl.pallas_call(
        functools.partial(_attn_body, len(cast_jobs)),
        grid=(batch, n_q),
        in_specs=[pl.BlockSpec((1, N_HEADS, QK_DIM, Q_TILE), lambda b, i: (b, 0, 0, i)),
                  pl.BlockSpec((1, N_HEADS, seq, QK_DIM), lambda b, i: (b, 0, 0, 0)),
                  pl.BlockSpec((1, N_HEADS, seq // KV_TILE, HEAD_DIM, KV_TILE), lambda b, i: (b, 0, 0, 0, 0)),
                  pl.BlockSpec((1, N_MEM, D_MODEL), lambda b, i: (b, 0, 0)),
                  _const_spec(mem_norm.shape), _const_spec(w_mkv.shape)]
                 + [c[0] for c in cast_specs],
        out_specs=[pl.BlockSpec((1, Q_TILE, ATT_WIDTH), lambda b, i: (b, i, 0)),
                   pl.BlockSpec((1, N_MEM, 2 * D_MODEL), lambda b, i: (b, 0, 0))] + [c[1] for c in cast_specs],
        out_shape=[jax.ShapeDtypeStruct((batch, seq, ATT_WIDTH), BF16),
                   jax.ShapeDtypeStruct((batch, N_MEM, 2 * D_MODEL), BF16)] + [c[2] for c in cast_specs],
        scratch_shapes=[pltpu.VMEM((N_HEADS, STAT_ROWS, Q_TILE), F32),
                        pltpu.VMEM((N_HEADS, STAT_ROWS, Q_TILE), F32),
                        pltpu.VMEM((N_HEADS, HEAD_DIM, Q_TILE), F32),
                        pltpu.VMEM((N_HEADS, STAT_ROWS, Q_TILE), F32)],
        compiler_params=pltpu.CompilerParams(dimension_semantics=("arbitrary", "arbitrary"),
                                             vmem_limit_bytes=VMEM_LIMIT_BYTES),
        name="mla_attn",
    )(q, k, v, mem, mem_norm, w_mkv, *[w for w, _ in cast_jobs])
    return outs[0], outs[1], outs[2:]


def _post_body(h_ref, a_ref, mkv_ref, xn_ref, n2_ref, fn_ref, wouta_ref, wmq_ref, wmo_ref, wg_hbm, wu_hbm, wd_hbm,
               out_ref, wg_ref, wu_ref, wd_ref, ffn_sem):
    def mix(rows):
        h2 = h_ref[rows, :] + _dot(a_ref[rows, :], wouta_ref[...])

        hn = _rmsnorm(h2, xn_ref[...]).astype(BF16)
        q = (_dot(hn, wmq_ref[...]) * MEM_SCALE).astype(BF16)
        cols = [slice(h * MEM_HEAD_DIM, (h + 1) * MEM_HEAD_DIM) for h in range(MEM_HEADS)]
        scores = [_dot_nt(q[:, c], mkv_ref[0, :, c]) for c in cols]
        probs = []
        for s in scores:
            e = jnp.exp(s - jnp.max(s, axis=-1, keepdims=True))
            probs.append((e / jnp.sum(e, axis=-1, keepdims=True)).astype(BF16))
        heads = [_dot(p, mkv_ref[0, :, D_MODEL + h * MEM_HEAD_DIM:D_MODEL + (h + 1) * MEM_HEAD_DIM]).astype(BF16)
                 for h, p in enumerate(probs)]
        return h2 + _dot(jnp.concatenate(heads, axis=1), wmo_ref[...])

    def ffn(rows, h3):
        xn = _rmsnorm(h3, n2_ref[...]).astype(BF16)
        h4 = h3 + 0.5 * _swiglu(xn, wg_ref, wu_ref, wd_ref)
        out_ref[rows, :] = _rmsnorm(h4, fn_ref[...])

    tm = h_ref.shape[0]
    groups = [slice(r, r + POST_ROW_GROUP) for r in range(0, tm, POST_ROW_GROUP)]
    ffn_weight_copies = [pltpu.make_async_copy(src, dst, ffn_sem.at[i])
                         for i, (src, dst) in enumerate(((wg_hbm, wg_ref), (wu_hbm, wu_ref), (wd_hbm, wd_ref)))]

    def tile(first_step):
        if first_step:
            for copy in ffn_weight_copies:
                copy.start()
        mixed = [mix(rows) for rows in groups]
        if first_step:
            for copy in ffn_weight_copies:
                copy.wait()
        for rows, h3 in zip(groups, mixed):
            ffn(rows, h3)

    step = pl.program_id(0)
    pl.when(step == 0)(functools.partial(tile, True))
    pl.when(step > 0)(functools.partial(tile, False))


def _post(h2d, a2d, mkv, xn, n2, fn, wouta, wmq, wmo, wg, wu, wd, seq):
    tm = POST_TILE
    n_tok = h2d.shape[0]
    tiles_per_seq = seq // tm
    weights = (xn, n2, fn, wouta, wmq, wmo)
    ffn_weights = (wg, wu, wd)
    return pl.pallas_call(
        _post_body,
        grid=(n_tok // tm,),
        in_specs=[pl.BlockSpec((tm, D_MODEL), lambda i: (i, 0)),
                  pl.BlockSpec((tm, ATT_WIDTH), lambda i: (i, 0)),
                  pl.BlockSpec((1, N_MEM, 2 * D_MODEL), lambda i: (i // tiles_per_seq, 0, 0))]
                 + [_const_spec(w.shape) for w in weights]
                 + [pl.BlockSpec(memory_space=pl.ANY)] * len(ffn_weights),
        out_specs=pl.BlockSpec((tm, D_MODEL), lambda i: (i, 0)),
        out_shape=jax.ShapeDtypeStruct((n_tok, D_MODEL), F32),
        scratch_shapes=[pltpu.VMEM(w.shape, BF16) for w in ffn_weights]
                       + [pltpu.SemaphoreType.DMA((len(ffn_weights),))],
        compiler_params=pltpu.CompilerParams(dimension_semantics=("arbitrary",),
                                             vmem_limit_bytes=VMEM_LIMIT_BYTES),
        name="post",
    )(h2d, a2d, mkv, *weights, *ffn_weights)


def _pad_rope_cols(w):
    zeros = jnp.zeros(w.shape[:-1] + (ROPE_HALF,), w.dtype)
    return jnp.concatenate([w[..., :ROPE_HALF], zeros, w[..., ROPE_HALF:], zeros], axis=-1)


def _block_diag2(a, b):
    za = jnp.zeros((a.shape[0], b.shape[1]), a.dtype)
    zb = jnp.zeros((b.shape[0], a.shape[1]), a.dtype)
    return jnp.concatenate([jnp.concatenate([a, za], axis=1), jnp.concatenate([zb, b], axis=1)], axis=0)


def _layer(h2d, mem, pos, freqs, batch, seq, ffn1_norm, ffn1_w_gate, ffn1_w_up, ffn1_w_down, mix_norm, w_in,
           q_norm, w_q_up, kv_norm, w_kv_up, pool_w, pool_scale, w_out, xattn_norm, mem_norm,
           w_mq, w_mkv, w_mo, ffn2_norm, ffn2_w_gate, ffn2_w_up, ffn2_w_down, out_norm):
    row = lambda g: g.reshape(1, -1).astype(F32)
    rope_lo = Q_RANK + KV_RANK
    w_in_p = jnp.concatenate([w_in[:, :rope_lo], _pad_rope_cols(w_in[:, rope_lo:rope_lo + ROPE_DIM]),
                              w_in[:, rope_lo + ROPE_DIM:]], axis=1).astype(BF16)
    wq = w_q_up.reshape(Q_RANK, N_HEADS, HEAD_DIM + ROPE_DIM)
    wq_p = jnp.concatenate([wq[..., :HEAD_DIM], _pad_rope_cols(wq[..., HEAD_DIM:])], axis=-1)
    wq_t = wq_p.reshape(Q_RANK, N_HEADS * QK_DIM).T.astype(BF16)
    wkv = w_kv_up.reshape(KV_RANK, N_HEADS, 2 * HEAD_DIM)
    wk = wkv[..., :HEAD_DIM].reshape(KV_RANK, ATT_WIDTH).astype(BF16)
    wv_t = wkv[..., HEAD_DIM:].reshape(KV_RANK, ATT_WIDTH).T.astype(BF16)
    poolw = jnp.stack([_block_diag2(pool_w[0], pool_w[1]), _block_diag2(pool_w[2], pool_w[3])]).astype(BF16)

    h1p, q, k, v = _pre(h2d, pos, freqs, row(ffn1_norm), row(mix_norm), w_in_p, row(q_norm), wq_t, row(kv_norm),
                        wk, wv_t, poolw, row(pool_scale), ffn1_w_gate, ffn1_w_up, ffn1_w_down, w_out, batch, seq)
    a, mkv, post_weights = _mla_attention(
        q, k, v, mem, row(mem_norm), w_mkv.astype(BF16),
        [(w_out, ATT_WIDTH), (w_mq, D_MODEL), (w_mo, D_MODEL),
         (ffn2_w_gate, D_MODEL), (ffn2_w_up, D_MODEL), (ffn2_w_down, D_FF)])
    return _post(h1p, a.reshape(batch * seq, ATT_WIDTH), mkv, row(xattn_norm), row(ffn2_norm), row(out_norm),
                 *post_weights, seq)


def kernel(x, mem, positions, ffn1_norm, ffn1_w_gate, ffn1_w_up, ffn1_w_down, mix_norm, w_in, q_norm, w_q_up,
           kv_norm, w_kv_up, pool_w, pool_scale, w_out, xattn_norm, mem_norm, w_mq, w_mkv, w_mo, ffn2_norm,
           ffn2_w_gate, ffn2_w_up, ffn2_w_down, final_norm):
    batch, seq, d_model = x.shape
    depth = ffn1_norm.shape[0]
    assert d_model == D_MODEL and depth == 1
    assert seq % TOKEN_TILE == 0 and seq % POST_TILE == 0 and seq % Q_TILE == 0
    freqs = 1.0 / (ROPE_BASE ** (jnp.arange(0, ROPE_DIM, 2, dtype=F32) / ROPE_DIM))
    layer = (ffn1_norm, ffn1_w_gate, ffn1_w_up, ffn1_w_down, mix_norm, w_in, q_norm, w_q_up, kv_norm, w_kv_up,
             pool_w, pool_scale, w_out, xattn_norm, mem_norm, w_mq, w_mkv, w_mo, ffn2_norm, ffn2_w_gate,
             ffn2_w_up, ffn2_w_down)
    out = _layer(x.reshape(batch * seq, d_model), mem, positions,
                 freqs.reshape(ROPE_HALF, 1), batch, seq, *[w[0] for w in layer], final_norm)
    return out.reshape(batch, seq, d_model)
```
